```python
import math
import jax, jax.numpy as jnp
from jax import lax
import numpy as np

D_MODEL = 1024
BATCH = 32
SEQ = 2048
DEPTH = 1

RWKV_HEAD = 64
RWKV_HEADS = 8
RWKV_DIM = RWKV_HEADS * RWKV_HEAD
W_LORA = 64
A_LORA = 64
G_LORA = 128
DECAY_SCALE = math.exp(-0.5)
LNX_EPS = 64e-5
ATT_HEADS = 8
ATT_KV_HEADS = 2
ATT_HEAD = 64
ATT_DIM = ATT_HEADS * ATT_HEAD
KV_DIM = ATT_KV_HEADS * ATT_HEAD
WINDOW = 128
BLOCK = 128
NEG_INF = -1e30
N_EXPERTS = 16
CAPACITY_FACTOR = 2
EXPERT_FF = 1024
NORM_EPS = 1e-6

RWKV_SPLITS = (RWKV_DIM, 2 * RWKV_DIM, 3 * RWKV_DIM, 3 * RWKV_DIM + G_LORA,
               3 * RWKV_DIM + G_LORA + W_LORA, 3 * RWKV_DIM + G_LORA + 2 * W_LORA,
               3 * RWKV_DIM + G_LORA + 2 * W_LORA + A_LORA)
RWKV_COLS = 3 * RWKV_DIM + G_LORA + 2 * W_LORA + 2 * A_LORA
ATT_SPLITS = (ATT_DIM, ATT_DIM + KV_DIM, ATT_DIM + 2 * KV_DIM, ATT_DIM + 2 * KV_DIM + D_MODEL)
IN_COLS = RWKV_COLS + ATT_DIM + 2 * KV_DIM + 2 * D_MODEL

kernel_name = 'hybrid_rwkv7_swa_ec_moe'


def rms_norm(x, g):
    xf = x.astype(jnp.float32)
    y = xf * lax.rsqrt(jnp.mean(xf * xf, axis=-1, keepdims=True) + NORM_EPS)
    return (y * g.astype(jnp.float32)).astype(x.dtype)


def centred_token_shift(u, mu_prev, mu_next):
    prev = jnp.pad(u, ((0, 0), (1, 0), (0, 0)))[:, :-1]
    nxt = jnp.pad(u, ((0, 0), (0, 1), (0, 0)))[:, 1:]
    return u + mu_prev * (prev - u) + mu_next * (nxt - u)


def wkv7_scan(r, decay, k, v, kk, a, reverse):
    B, T, H, N = r.shape
    xs = tuple(jnp.moveaxis(t.astype(jnp.float32), 1, 0) for t in (r, decay, k, v, kk, a))

    def step(S, inp):
        r_t, w_t, k_t, v_t, kk_t, a_t = inp
        s_kk = jnp.einsum('bhij,bhj->bhi', S, kk_t)
        S = (S * w_t[:, :, None, :]
             - s_kk[..., None] * (a_t * kk_t)[:, :, None, :]
             + v_t[..., None] * k_t[:, :, None, :])
        return S, jnp.einsum('bhij,bhj->bhi', S, r_t)

    S0 = jnp.zeros((B, H, N, N), jnp.float32)
    _, y = lax.scan(step, S0, xs, reverse=reverse)
    return jnp.moveaxis(y, 0, 1)


def rwkv7_bidir(u, w0_f, w_up_f, w0_b, w_up_b, a0_f, a_up_f, a0_b, a_up_b, g_up, k_k, k_a, r_k, ln_w, ln_b):
    f32 = jnp.float32
    B, T, _ = u.shape
    r, k, v, g_lo, wf_lo, wb_lo, af_lo, ab_lo = jnp.split(u, RWKV_SPLITS, axis=-1)

    def heads(t):
        return t.reshape(B, T, RWKV_HEADS, RWKV_HEAD)

    def decay(w0, lo, up):
        return jnp.exp(-DECAY_SCALE * jax.nn.sigmoid((w0 + jnp.tanh(lo) @ up).astype(f32)))

    def icl_rate(a0, lo, up):
        return jax.nn.sigmoid((a0 + lo @ up).astype(f32))

    kf = k.astype(f32)
    kk = heads(kf * k_k.astype(f32))
    kk = kk / jnp.maximum(jnp.sqrt(jnp.sum(kk * kk, axis=-1, keepdims=True)), 1e-12)
    a_f = icl_rate(a0_f, af_lo, a_up_f)
    a_b = icl_rate(a0_b, ab_lo, a_up_b)
    k_f = heads(kf * (1.0 + (a_f - 1.0) * k_a.astype(f32)))
    k_b = heads(kf * (1.0 + (a_b - 1.0) * k_a.astype(f32)))
    r4 = heads(r.astype(f32))
    v4 = heads(v.astype(f32))

    wkv = (wkv7_scan(r4, heads(decay(w0_f, wf_lo, w_up_f)), k_f, v4, kk, heads(a_f), False)
           + wkv7_scan(r4, heads(decay(w0_b, wb_lo, w_up_b)), k_b, v4, kk, heads(a_b), True))

    mean = jnp.mean(wkv, axis=-1, keepdims=True)
    var = jnp.mean(jnp.square(wkv - mean), axis=-1, keepdims=True)
    normed = ((wkv - mean) * lax.rsqrt(var + LNX_EPS)).reshape(B, T, RWKV_DIM)
    normed = normed * ln_w.astype(f32) + ln_b.astype(f32)

    rk = r_k.astype(f32)
    bonus = ((jnp.sum(r4 * k_f * rk, axis=-1, keepdims=True)
              + jnp.sum(r4 * k_b * rk, axis=-1, keepdims=True)) * v4).reshape(B, T, RWKV_DIM)
    g = (jax.nn.sigmoid(g_lo) @ g_up).astype(f32)
    return ((normed + bonus) * g).astype(u.dtype)


def alibi_slopes(n_heads):
    return 2.0 ** (-8.0 * np.arange(1, n_heads + 1) / n_heads)


def banded_gqa_attention(q, k, v, sink):
    f32 = jnp.float32
    B, T, HQ, DH = q.shape
    G = HQ // ATT_KV_HEADS
    nb = T // BLOCK
    span = BLOCK + 2 * WINDOW
    q = q.reshape(B, T, ATT_KV_HEADS, G, DH) * (DH ** -0.5)
    kp = jnp.pad(k, ((0, 0), (WINDOW, WINDOW), (0, 0), (0, 0)))
    vp = jnp.pad(v, ((0, 0), (WINDOW, WINDOW), (0, 0), (0, 0)))
    slopes = jnp.asarray(alibi_slopes(HQ), f32).reshape(ATT_KV_HEADS, G)
    sink_l = sink.astype(f32).reshape(1, ATT_KV_HEADS, G, 1, 1)
    l_idx = jnp.arange(BLOCK)
    j_idx = jnp.arange(span)
    rel = j_idx[None, :] - WINDOW - l_idx[:, None]
    in_window = jnp.abs(rel) <= WINDOW
    alibi = -slopes[:, :, None, None] * jnp.abs(rel).astype(f32)

    def one_block(i):
        start = i * BLOCK
        qb = lax.dynamic_slice_in_dim(q, start, BLOCK, axis=1)
        kb = lax.dynamic_slice_in_dim(kp, start, span, axis=1)
        vb = lax.dynamic_slice_in_dim(vp, start, span, axis=1)
        key_pos = start - WINDOW + j_idx
        valid = in_window & ((key_pos >= 0) & (key_pos < T))[None, :]
        s = jnp.einsum('blkgd,bjkd->bkglj', qb, kb).astype(f32) + alibi
        s = jnp.where(valid, s, NEG_INF)
        m = jnp.maximum(jnp.max(s, axis=-1, keepdims=True), sink_l)
        p = jnp.exp(s - m)
        denom = jnp.sum(p, axis=-1, keepdims=True) + jnp.exp(sink_l - m)
        return jnp.einsum('bkglj,bjkd->blkgd', (p / denom).astype(vb.dtype), vb)

    out = lax.map(one_block, jnp.arange(nb))
    return jnp.moveaxis(out, 0, 1).reshape(B, T, HQ * DH)


def expert_choice_moe(h, w_router, w_gate, w_up, w_down):
    B, T, D = h.shape
    cap = CAPACITY_FACTOR * T // N_EXPERTS
    affinity = jax.nn.softmax((h @ w_router).astype(jnp.float32), axis=-1)
    top_vals, top_idx = lax.top_k(jnp.swapaxes(affinity, 1, 2), cap)
    b_idx = jnp.arange(B)[:, None, None]
    xs = h[b_idx, top_idx]
    hid = jax.nn.silu(jnp.einsum('becd,edf->becf', xs, w_gate)) * jnp.einsum('becd,edf->becf', xs, w_up)
    ys = jnp.einsum('becf,efd->becd', hid, w_down) * top_vals[..., None].astype(h.dtype)
    return jnp.zeros_like(h).at[b_idx, top_idx].add(ys)


def setup_inputs(seed: int = 0) -> dict:
    key = jax.random.key(seed)
    ks = jax.random.split(key, 32)
    L, D, f32 = DEPTH, D_MODEL, jnp.float32
    nrm = lambda k, shape, s: (jax.random.normal(k, shape, f32) * s)
    return {
        'x': nrm(ks[0], (BATCH, SEQ, D), 1.0),
        'norm_mix_g': 1.0 + nrm(ks[1], (L, D), 0.1),
        'w_in': nrm(ks[2], (L, D, IN_COLS), D ** -0.5),
        'mu_prev': jax.random.uniform(ks[3], (L, RWKV_COLS), f32, 0.05, 0.45),
        'mu_next': jax.random.uniform(ks[4], (L, RWKV_COLS), f32, 0.05, 0.45),
        'w0_f': nrm(ks[5], (L, RWKV_DIM), 1.0),
        'w_up_f': nrm(ks[6], (L, W_LORA, RWKV_DIM), 0.5 * W_LORA ** -0.5),
        'w0_b': nrm(ks[7], (L, RWKV_DIM), 1.0),
        'w_up_b': nrm(ks[8], (L, W_LORA, RWKV_DIM), 0.5 * W_LORA ** -0.5),
        'a0_f': nrm(ks[9], (L, RWKV_DIM), 0.5),
        'a_up_f': nrm(ks[10], (L, A_LORA, RWKV_DIM), 0.5 * A_LORA ** -0.5),
        'a0_b': nrm(ks[11], (L, RWKV_DIM), 0.5),
        'a_up_b': nrm(ks[12], (L, A_LORA, RWKV_DIM), 0.5 * A_LORA ** -0.5),
        'g_up': nrm(ks[13], (L, G_LORA, RWKV_DIM), G_LORA ** -0.5),
        'k_k': 0.85 + nrm(ks[14], (L, RWKV_DIM), 0.05),
        'k_a': 1.0 + nrm(ks[15], (L, RWKV_DIM), 0.05),
        'r_k': nrm(ks[16], (L, RWKV_HEADS, RWKV_HEAD), 0.1),
        'ln_x_w': 1.0 + nrm(ks[17], (L, RWKV_DIM), 0.1),
        'ln_x_b': nrm(ks[18], (L, RWKV_DIM), 0.02),
        'attn_sink': nrm(ks[19], (L, ATT_HEADS), 0.5),
        'w_proj_rwkv': nrm(ks[20], (L, RWKV_DIM, D), RWKV_DIM ** -0.5),
        'w_proj_attn': nrm(ks[21], (L, ATT_DIM, D), ATT_DIM ** -0.5),
        'w_out': nrm(ks[22], (L, D, D), D ** -0.5),
        'norm_ffn_g': 1.0 + nrm(ks[23], (L, D), 0.1),
        'w_router': nrm(ks[24], (L, D, N_EXPERTS), D ** -0.5),
        'exp_w_gate': nrm(ks[25], (L, N_EXPERTS, D, EXPERT_FF), D ** -0.5),
        'exp_w_up': nrm(ks[26], (L, N_EXPERTS, D, EXPERT_FF), D ** -0.5),
        'exp_w_down': nrm(ks[27], (L, N_EXPERTS, EXPERT_FF, D), EXPERT_FF ** -0.5),
        'norm_final_g': 1.0 + nrm(ks[28], (D,), 0.1),
    }


def reference(x, norm_mix_g, w_in, mu_prev, mu_next, w0_f, w_up_f, w0_b, w_up_b, a0_f, a_up_f, a0_b, a_up_b,
              g_up, k_k, k_a, r_k, ln_x_w, ln_x_b, attn_sink, w_proj_rwkv, w_proj_attn, w_out, norm_ffn_g,
              w_router, exp_w_gate, exp_w_up, exp_w_down, norm_final_g):
    B, T, _ = x.shape
    for layer in range(DEPTH):
        h = rms_norm(x, norm_mix_g[layer])
        u = h @ w_in[layer]
        u_rwkv = centred_token_shift(u[..., :RWKV_COLS], mu_prev[layer], mu_next[layer])
        q, k_att, v_att, gl_rwkv, gl_attn = jnp.split(u[..., RWKV_COLS:], ATT_SPLITS, axis=-1)
        y_rwkv = rwkv7_bidir(u_rwkv, w0_f[layer], w_up_f[layer], w0_b[layer], w_up_b[layer],
                             a0_f[layer], a_up_f[layer], a0_b[layer], a_up_b[layer], g_up[layer],
                             k_k[layer], k_a[layer], r_k[layer], ln_x_w[layer], ln_x_b[layer])
        y_attn = banded_gqa_attention(q.reshape(B, T, ATT_HEADS, ATT_HEAD),
                                      k_att.reshape(B, T, ATT_KV_HEADS, ATT_HEAD),
                                      v_att.reshape(B, T, ATT_KV_HEADS, ATT_HEAD), attn_sink[layer])
        merged = (jax.nn.sigmoid(gl_rwkv) * (y_rwkv @ w_proj_rwkv[layer])
                  + jax.nn.sigmoid(gl_attn) * (y_attn @ w_proj_attn[layer]))
        x = x + merged @ w_out[layer]
        x = x + expert_choice_moe(rms_norm(x, norm_ffn_g[layer]), w_router[layer],
                                  exp_w_gate[layer], exp_w_up[layer], exp_w_down[layer])
    return rms_norm(x, norm_final_g)
```

```python
import functools
import math

import numpy as np
import jax
import jax.numpy as jnp
from jax import lax
from jax.experimental import pallas as pl
from jax.experimental.pallas import tpu as pltpu

F32 = jnp.float32
BF16 = jnp.bfloat16

RWKV_HEAD = 64
RWKV_HEADS = 8
RWKV_DIM = RWKV_HEADS * RWKV_HEAD
W_LORA = 64
A_LORA = 64
G_LORA = 128
DECAY_SCALE = math.exp(-0.5)
LNX_EPS = 64e-5
ATT_HEADS = 8
ATT_KV_HEADS = 2
ATT_HEAD = 64
ATT_DIM = ATT_HEADS * ATT_HEAD
KV_DIM = ATT_KV_HEADS * ATT_HEAD
WINDOW = 128
BLOCK = 128
NEG_INF = -1e30
N_EXPERTS = 16
CAPACITY_FACTOR = 2
NORM_EPS = 1e-6
RWKV_COLS = 3 * RWKV_DIM + G_LORA + 2 * W_LORA + 2 * A_LORA
QKV_COLS = ATT_DIM + 2 * KV_DIM
OFF_R, OFF_K, OFF_V = 0, RWKV_DIM, 2 * RWKV_DIM
OFF_G = 3 * RWKV_DIM
OFF_WF = OFF_G + G_LORA
OFF_WB = OFF_WF + W_LORA
OFF_AF = OFF_WB + W_LORA
OFF_AB = OFF_AF + A_LORA

LANES = 128
BF16_SUBLANES = 16
VMEM_LIMIT = 48 * 1024 * 1024

CHUNK = 64


def _dot(a, b):
    return jnp.dot(a, b, preferred_element_type=F32)


def _dot_nt(a, b):
    return lax.dot_general(a, b, (((1,), (1,)), ((), ())), preferred_element_type=F32)


def _dot_tn(a, b):
    return lax.dot_general(a, b, (((0,), (0,)), ((), ())), preferred_element_type=F32)


def _split2(x):
    hi = x.astype(BF16)
    lo = (x - hi.astype(F32)).astype(BF16)
    return hi, lo


def _split3(x):
    hi = x.astype(BF16)
    r1 = x - hi.astype(F32)
    mid = r1.astype(BF16)
    lo = (r1 - mid.astype(F32)).astype(BF16)
    return hi, mid, lo


def _seg_sum(x, bd):
    hi, lo = _split2(x)
    return _dot(hi, bd) + _dot(lo, bd)


def _sigmoid(x):
    return 1.0 / (1.0 + jnp.exp(-x))


def _params(*sem):
    return pltpu.CompilerParams(dimension_semantics=sem, vmem_limit_bytes=VMEM_LIMIT)


def _inproj_kernel(x_ref, g_ref, w_ref, ur_ref, qkv_ref, gate_ref, *, col_chunk):
    x = x_ref[...]
    h = x * lax.rsqrt(jnp.mean(x * x, axis=-1, keepdims=True) + NORM_EPS) * g_ref[...]
    hb = h.astype(BF16)
    c0 = 0
    for ref in (ur_ref, qkv_ref, gate_ref):
        width = ref.shape[-1]
        for j in range(0, width, col_chunk):
            w = min(col_chunk, width - j)
            ref[:, j:j + w] = _dot(hb, w_ref[:, c0 + j:c0 + j + w]).astype(BF16)
        c0 += width


def _in_projection(x2, g, w_in_bf, tm):
    n, d = x2.shape
    cols = w_in_bf.shape[1]
    gate_cols = cols - RWKV_COLS - QKV_COLS
    return pl.pallas_call(
        functools.partial(_inproj_kernel, col_chunk=512),
        grid=(n // tm,),
        in_specs=[pl.BlockSpec((tm, d), lambda i: (i, 0)),
                  pl.BlockSpec((1, d), lambda i: (0, 0)),
                  pl.BlockSpec((d, cols), lambda i: (0, 0))],
        out_specs=[pl.BlockSpec((tm, RWKV_COLS), lambda i: (i, 0)),
                   pl.BlockSpec((tm, QKV_COLS), lambda i: (i, 0)),
                   pl.BlockSpec((tm, gate_cols), lambda i: (i, 0))],
        out_shape=[jax.ShapeDtypeStruct((n, RWKV_COLS), BF16),
                   jax.ShapeDtypeStruct((n, QKV_COLS), BF16),
                   jax.ShapeDtypeStruct((n, gate_cols), BF16)],
        compiler_params=_params("parallel"),
    )(x2, g.reshape(1, d), w_in_bf)


def _rwkv_kernel(*refs, reverse, final, n_chunks):
    if final:
        (u_ref, up_ref, un_ref, mup_ref, mun_ref, w0_ref, wup_ref, a0_ref, aup_ref, kk_ref, ka_ref,
         bd_ref, tri_ref, a0o_ref, aupo_ref, gup_ref, rk_ref, lnw_ref, lnb_ref, yb_ref,
         o_ref, st_ref) = refs
    else:
        (u_ref, up_ref, un_ref, mup_ref, mun_ref, w0_ref, wup_ref, a0_ref, aup_ref, kk_ref, ka_ref,
         bd_ref, tri_ref, o_ref, st_ref) = refs
    L = u_ref.shape[1]
    N = RWKV_HEAD
    c = pl.program_id(1)
    chunk = (n_chunks - 1 - c) if reverse else c

    @pl.when(c == 0)
    def _():
        st_ref[...] = jnp.zeros_like(st_ref)

    u = u_ref[0].astype(F32)
    prev_row = up_ref[0].astype(F32)[BF16_SUBLANES - 1:BF16_SUBLANES]
    next_row = un_ref[0].astype(F32)[0:1]
    prev_row = jnp.where(chunk == 0, 0.0, prev_row)
    next_row = jnp.where(chunk == n_chunks - 1, 0.0, next_row)
    row = lax.broadcasted_iota(jnp.int32, (L, 1), 0)
    prev = jnp.where(row == 0, prev_row, pltpu.roll(u, 1, axis=0))
    nxt = jnp.where(row == L - 1, next_row, pltpu.roll(u, L - 1, axis=0))
    us = u + mup_ref[...] * (prev - u) + mun_ref[...] * (nxt - u)

    r = us[:, OFF_R:OFF_R + RWKV_DIM]
    k = us[:, OFF_K:OFF_K + RWKV_DIM]
    v = us[:, OFF_V:OFF_V + RWKV_DIM]
    off_w = OFF_WB if reverse else OFF_WF
    off_a = OFF_AB if reverse else OFF_AF
    w_lo = us[:, off_w:off_w + W_LORA]
    a_lo = us[:, off_a:off_a + A_LORA]
    bd = bd_ref[...]

    lw = -DECAY_SCALE * _sigmoid(w0_ref[...] + _dot(jnp.tanh(w_lo).astype(BF16), wup_ref[...]))
    a = _sigmoid(a0_ref[...] + _dot(a_lo.astype(BF16), aup_ref[...]))
    kkr = k * kk_ref[...]
    kk = kkr / jnp.maximum(jnp.sqrt(_seg_sum(kkr * kkr, bd)), 1e-12)
    kd = k * (1.0 + (a - 1.0) * ka_ref[...])
    b = a * kk

    tri = tri_ref[...]
    l_hi, l_mid, l_lo = _split3(lw)
    cs = _dot(tri, l_hi) + _dot(tri, l_mid) + _dot(tri, l_lo)
    c_end = cs[0:1] if reverse else cs[L - 1:L]
    e_inc = jnp.exp(cs)
    e_exc = jnp.exp(cs - lw)
    e_neg = jnp.exp(-cs)
    e_rem = jnp.exp(c_end - cs)
    e_tot = jnp.exp(c_end)
    a_t = (kk * e_exc).astype(BF16)
    r_t = (r * e_inc).astype(BF16)
    k_t = (kd * e_neg).astype(BF16)
    b_t = (b * e_neg).astype(BF16)
    k_h = (kd * e_rem).astype(BF16)
    b_h = (b * e_rem).astype(BF16)
    v_b = v.astype(BF16)

    ti = lax.broadcasted_iota(jnp.int32, (L, L), 0)
    si = lax.broadcasted_iota(jnp.int32, (L, L), 1)
    if reverse:
        strict = si > ti
        incl = si >= ti
    else:
        strict = si < ti
        incl = si <= ti
    eye = (si == ti).astype(F32)
    level_masks = []
    s = 1
    while s < L:
        same = (ti // (2 * s)) == (si // (2 * s))
        t_hi = (ti & s) != 0
        s_hi = (si & s) != 0
        if reverse:
            level_masks.append(same & jnp.logical_not(t_hi) & s_hi)
        else:
            level_masks.append(same & t_hi & jnp.logical_not(s_hi))
        s *= 2
    ni = lax.broadcasted_iota(jnp.int32, (N, N), 0)
    nj = lax.broadcasted_iota(jnp.int32, (N, N), 1)
    eye_n = ni == nj

    ys = []
    for h in range(RWKV_HEADS):
        sl = slice(h * N, (h + 1) * N)
        ah, rh, kth, bth, khh, bhh, vh = a_t[:, sl], r_t[:, sl], k_t[:, sl], b_t[:, sl], k_h[:, sl], b_h[:, sl], v_b[:, sl]
        ab = _dot_nt(ah, bth)
        ak = _dot_nt(ah, kth)
        rk_ = _dot_nt(rh, kth)
        rb = _dot_nt(rh, bth)
        st = st_ref[h]
        st_b = st.astype(BF16)
        rhs = _dot(ah, st_b) + _dot(jnp.where(strict, ak, 0.0).astype(BF16), vh)
        x = eye - jnp.where(level_masks[0], ab, 0.0)
        for m in level_masks[1:]:
            xb = x.astype(BF16)
            x = x - _dot(_dot(xb, jnp.where(m, ab, 0.0).astype(BF16)).astype(BF16), xb)
        uu = _dot(x.astype(BF16), rhs.astype(BF16)).astype(BF16)
        y = (_dot(rh, st_b) + _dot(jnp.where(incl, rk_, 0.0).astype(BF16), vh)
             - _dot(jnp.where(incl, rb, 0.0).astype(BF16), uu))
        ys.append(y)
        p_col = jnp.sum(jnp.where(eye_n, e_tot[:, sl], 0.0), axis=1, keepdims=True)
        st_ref[h] = p_col * st + _dot_tn(khh, vh) - _dot_tn(bhh, uu)
    wkv = jnp.concatenate(ys, axis=-1)

    if not final:
        o_ref[0] = wkv
        return

    wkv = wkv + yb_ref[0]
    inv_n = 1.0 / N
    mean = _seg_sum(wkv, bd) * inv_n
    xc = wkv - mean
    var = _seg_sum(xc * xc, bd) * inv_n
    normed = xc * lax.rsqrt(var + LNX_EPS) * lnw_ref[...] + lnb_ref[...]
    ao_lo = us[:, OFF_AB:OFF_AB + A_LORA]
    a_o = _sigmoid(a0o_ref[...] + _dot(ao_lo.astype(BF16), aupo_ref[...]))
    kd_o = k * (1.0 + (a_o - 1.0) * ka_ref[...])
    rk = rk_ref[...]
    bonus = (_seg_sum(r * kd * rk, bd) + _seg_sum(r * kd_o * rk, bd)) * v
    g_lo = us[:, OFF_G:OFF_G + G_LORA]
    g = _dot(_sigmoid(g_lo).astype(BF16), gup_ref[...])
    o_ref[0] = ((normed + bonus) * g).astype(o_ref.dtype)


def _rwkv_pass(u_r, consts, *, reverse, final, extra=()):
    b, t, _ = u_r.shape
    L = CHUNK
    n_chunks = t // L
    hb = L // BF16_SUBLANES
    n_halo = t // BF16_SUBLANES

    def cidx(c):
        return (n_chunks - 1 - c) if reverse else c

    def full(shape):
        return pl.BlockSpec(shape, lambda bi, c: (0,) * len(shape))

    in_specs = [
        pl.BlockSpec((1, L, RWKV_COLS), lambda bi, c: (bi, cidx(c), 0)),
        pl.BlockSpec((1, BF16_SUBLANES, RWKV_COLS), lambda bi, c: (bi, jnp.maximum(cidx(c) * hb - 1, 0), 0)),
        pl.BlockSpec((1, BF16_SUBLANES, RWKV_COLS),
                     lambda bi, c: (bi, jnp.minimum((cidx(c) + 1) * hb, n_halo - 1), 0)),
    ]
    args = [u_r, u_r, u_r]
    for arr in consts:
        in_specs.append(full(arr.shape))
        args.append(arr)
    if final:
        for arr in extra[:-1]:
            in_specs.append(full(arr.shape))
            args.append(arr)
        in_specs.append(pl.BlockSpec((1, L, RWKV_DIM), lambda bi, c: (bi, cidx(c), 0)))
        args.append(extra[-1])
    out_dtype = BF16 if final else F32
    return pl.pallas_call(
        functools.partial(_rwkv_kernel, reverse=reverse, final=final, n_chunks=n_chunks),
        grid=(b, n_chunks),
        in_specs=in_specs,
        out_specs=pl.BlockSpec((1, L, RWKV_DIM), lambda bi, c: (bi, cidx(c), 0)),
        out_shape=jax.ShapeDtypeStruct((b, t, RWKV_DIM), out_dtype),
        scratch_shapes=[pltpu.VMEM((RWKV_HEADS, RWKV_HEAD, RWKV_HEAD), F32)],
        compiler_params=_params("parallel", "arbitrary"),
    )(*args)


def _rwkv_branch(u_r, mu_prev, mu_next, w0_f, w_up_f, w0_b, w_up_b, a0_f, a_up_f, a0_b, a_up_b,
                 g_up, k_k, k_a, r_k, ln_w, ln_b):
    L = CHUNK
    row = lambda a: a.reshape(1, -1).astype(F32)
    seg = np.arange(RWKV_DIM) // RWKV_HEAD
    bd = jnp.asarray(seg[:, None] == seg[None, :], BF16)
    ti = np.arange(L)
    tri_f = jnp.asarray(ti[None, :] <= ti[:, None], BF16)
    tri_b = jnp.asarray(ti[None, :] >= ti[:, None], BF16)
    common = lambda w0, wup, a0, aup, tri: [row(mu_prev), row(mu_next), row(w0), wup.astype(BF16), row(a0),
                                            aup.astype(BF16), row(k_k), row(k_a), bd, tri]
    y_b = _rwkv_pass(u_r, common(w0_b, w_up_b, a0_b, a_up_b, tri_b), reverse=True, final=False)
    extra = [row(a0_b), a_up_b.astype(BF16), g_up.astype(BF16), row(r_k), row(ln_w), row(ln_b), y_b]
    return _rwkv_pass(u_r, common(w0_f, w_up_f, a0_f, a_up_f, tri_f), reverse=False, final=True, extra=extra)


def _attn_kernel(sink_ref, q_ref, kp_ref, kc_ref, kn_ref, vp_ref, vc_ref, vn_ref, o_ref, *, seq_len):
    i = pl.program_id(1)
    span = BLOCK + 2 * WINDOW
    kk = jnp.concatenate([kp_ref[0], kc_ref[0], kn_ref[0]], axis=0)
    vv = jnp.concatenate([vp_ref[0], vc_ref[0], vn_ref[0]], axis=0)
    li = lax.broadcasted_iota(jnp.int32, (BLOCK, span), 0)
    ji = lax.broadcasted_iota(jnp.int32, (BLOCK, span), 1)
    rel = jnp.abs(ji - WINDOW - li)
    key_pos = i * BLOCK - WINDOW + ji
    valid = (rel <= WINDOW) & (key_pos >= 0) & (key_pos < seq_len)
    arel = rel.astype(F32)
    group = ATT_HEADS // ATT_KV_HEADS
    scale = ATT_HEAD ** -0.5
    outs = []
    for h in range(ATT_HEADS):
        kv = h // group
        slope = float(2.0 ** (-8.0 * (h + 1) / ATT_HEADS))
        qh = q_ref[0, :, h * ATT_HEAD:(h + 1) * ATT_HEAD]
        kh = kk[:, kv * ATT_HEAD:(kv + 1) * ATT_HEAD]
        vh = vv[:, kv * ATT_HEAD:(kv + 1) * ATT_HEAD]
        s = _dot_nt(qh, kh) * scale - slope * arel
        s = jnp.where(valid, s, NEG_INF)
        sink = sink_ref[h]
        m = jnp.maximum(jnp.max(s, axis=-1, keepdims=True), sink)
        p = jnp.exp(s - m)
        denom = jnp.sum(p, axis=-1, keepdims=True) + jnp.exp(sink - m)
        outs.append(_dot(p.astype(BF16), vh) / denom)
    o_ref[0] = jnp.concatenate(outs, axis=-1).astype(o_ref.dtype)


def _attention(qkv, sink):
    b, t, _ = qkv.shape
    nb = t // BLOCK
    kcol = ATT_DIM // KV_DIM
    prev = lambda i: jnp.maximum(i - 1, 0)
    nxt = lambda i: jnp.minimum(i + 1, nb - 1)
    kv_spec = lambda col, f: pl.BlockSpec((1, BLOCK, KV_DIM), lambda bi, i: (bi, f(i), col))
    same = lambda i: i
    return pl.pallas_call(
        functools.partial(_attn_kernel, seq_len=t),
        grid=(b, nb),
        in_specs=[pl.BlockSpec(memory_space=pltpu.SMEM),
                  pl.BlockSpec((1, BLOCK, ATT_DIM), lambda bi, i: (bi, i, 0)),
                  kv_spec(kcol, prev), kv_spec(kcol, same), kv_spec(kcol, nxt),
                  kv_spec(kcol + 1, prev), kv_spec(kcol + 1, same), kv_spec(kcol + 1, nxt)],
        out_specs=pl.BlockSpec((1, BLOCK, ATT_DIM), lambda bi, i: (bi, i, 0)),
        out_shape=jax.ShapeDtypeStruct((b, t, ATT_DIM), BF16),
        compiler_params=_params("parallel", "parallel"),
    )(sink.astype(F32), qkv, qkv, qkv, qkv, qkv, qkv, qkv)


def _merge_kernel(yr_ref, ya_ref, gate_ref, x_ref, wpr_ref, wpa_ref, wo_ref, g_ref, wrt_ref,
                  x1_ref, h2_ref, aff_ref, afft_ref):
    d = x_ref.shape[-1]
    gl_r = gate_ref[:, 0:d].astype(F32)
    gl_a = gate_ref[:, d:2 * d].astype(F32)
    merged = (_sigmoid(gl_r) * _dot(yr_ref[...], wpr_ref[...])
              + _sigmoid(gl_a) * _dot(ya_ref[...], wpa_ref[...]))
    x1 = x_ref[...] + _dot(merged.astype(BF16), wo_ref[...])
    x1_ref[...] = x1
    h2 = (x1 * lax.rsqrt(jnp.mean(x1 * x1, axis=-1, keepdims=True) + NORM_EPS) * g_ref[...]).astype(BF16)
    h2_ref[...] = h2
    logits_t = _dot_nt(wrt_ref[...], h2)
    mx = jnp.max(logits_t, axis=0, keepdims=True)
    ex = jnp.exp(logits_t - mx)
    aff_t = ex / jnp.sum(ex, axis=0, keepdims=True)
    afft_ref[0] = aff_t
    aff_ref[...] = aff_t.T


def _merge(y_r, y_a, gates, x2, wpr, wpa, wo, g, w_router, tm, seq_len):
    n, d = x2.shape
    e = w_router.shape[1]
    tiles_per_seq = seq_len // tm
    full = lambda shape: pl.BlockSpec(shape, lambda i: (0,) * len(shape))
    rows = lambda w: pl.BlockSpec((tm, w), lambda i: (i, 0))
    return pl.pallas_call(
        _merge_kernel,
        grid=(n // tm,),
        in_specs=[rows(RWKV_DIM), rows(ATT_DIM), rows(2 * d), rows(d),
                  full(wpr.shape), full(wpa.shape), full(wo.shape), full((1, d)), full((e, d))],
        out_specs=[rows(d), rows(d), rows(e),
                   pl.BlockSpec((1, e, tm), lambda i: (i // tiles_per_seq, 0, i % tiles_per_seq))],
        out_shape=[jax.ShapeDtypeStruct((n, d), F32), jax.ShapeDtypeStruct((n, d), BF16),
                   jax.ShapeDtypeStruct((n, e), F32),
                   jax.ShapeDtypeStruct((n // seq_len, e, seq_len), F32)],
        compiler_params=_params("parallel"),
    )(y_r, y_a, gates, x2, wpr.astype(BF16), wpa.astype(BF16), wo.astype(BF16), g.reshape(1, d).astype(F32),
      w_router.T.astype(BF16))


def _select_kernel(a_ref, gsame_ref, gprev_ref, triu_ref, slot_ref, *, cap):
    bits = pltpu.bitcast(a_ref[0], jnp.int32)
    gsame = gsame_ref[...]
    gprev = gprev_ref[...]
    triu = triu_ref[...]
    rows = bits.shape[0]
    capf = float(cap)

    def group_count(mask):
        per_lane = _dot(gsame, jnp.where(mask, 1.0, 0.0).astype(BF16))
        return jnp.sum(per_lane, axis=-1, keepdims=True)

    def search(i, prefix):
        cand = prefix | lax.shift_left(jnp.int32(1), 30 - i)
        return jnp.where(group_count(bits >= cand) >= capf, cand, prefix)

    thr = lax.fori_loop(0, 31, search, jnp.zeros((rows, 1), jnp.int32))

    def prefix_count(mask):
        mb = jnp.where(mask, 1.0, 0.0)
        inc = _dot(mb.astype(BF16), triu)
        tot = jnp.broadcast_to(inc[:, LANES - 1:LANES], inc.shape).astype(BF16)
        return inc - mb + _dot(gprev, tot)

    gt = bits > thr
    eq = bits == thr
    need = capf - group_count(gt)
    sel = gt | (eq & (prefix_count(eq) < need))
    pos = prefix_count(sel)
    slot_ref[0] = jnp.where(sel, pos, -1.0).astype(jnp.int32)


def _select(aff_t, cap):
    b, e, t = aff_t.shape
    tiles = t // LANES
    rows = e * tiles
    rid = np.arange(rows)
    same = (rid[:, None] // tiles) == (rid[None, :] // tiles)
    gsame = jnp.asarray(same, BF16)
    gprev = jnp.asarray(same & (rid[None, :] < rid[:, None]), BF16)
    li = np.arange(LANES)
    triu = jnp.asarray(li[:, None] <= li[None, :], BF16)
    full = lambda shape: pl.BlockSpec(shape, lambda bi: (0,) * len(shape))
    slot = pl.pallas_call(
        functools.partial(_select_kernel, cap=cap),
        grid=(b,),
        in_specs=[pl.BlockSpec((1, rows, LANES), lambda bi: (bi, 0, 0)),
                  full((rows, rows)), full((rows, rows)), full((LANES, LANES))],
        out_specs=pl.BlockSpec((1, rows, LANES), lambda bi: (bi, 0, 0)),
        out_shape=jax.ShapeDtypeStruct((b, rows, LANES), jnp.int32),
        compiler_params=_params("parallel"),
    )(aff_t.reshape(b, rows, LANES), gsame, gprev, triu)
    return slot.reshape(b, e, t)


def _gather_kernel(slot_ref, h_ref, xs_ref):
    cap = xs_ref.shape[2]
    slot = slot_ref[0]
    cidx = lax.broadcasted_iota(jnp.int32, (cap, 1), 0)
    onehot = jnp.where(cidx == slot, 1.0, 0.0).astype(BF16)
    xs_ref[0, 0] = _dot(onehot, h_ref[0]).astype(xs_ref.dtype)


def _gather(slot_rows, h2, cap):
    b, t, d = h2.shape
    e = slot_rows.shape[0] // b
    return pl.pallas_call(
        _gather_kernel,
        grid=(b, e),
        in_specs=[pl.BlockSpec((1, 1, t), lambda bi, ei: (bi * e + ei, 0, 0)),
                  pl.BlockSpec((1, t, d), lambda bi, ei: (bi, 0, 0))],
        out_specs=pl.BlockSpec((1, 1, cap, d), lambda bi, ei: (bi, ei, 0, 0)),
        out_shape=jax.ShapeDtypeStruct((b, e, cap, d), BF16),
        compiler_params=_params("parallel", "arbitrary"),
    )(slot_rows, h2)


def _expert_kernel(xs_ref, wg_ref, wu_ref, wd_ref, ys_ref):
    bb, _, cap, d = xs_ref.shape
    xs = xs_ref[...].reshape(bb * cap, d)
    gate = _dot(xs, wg_ref[0])
    up = _dot(xs, wu_ref[0])
    hid = (gate * _sigmoid(gate) * up).astype(BF16)
    ys_ref[...] = _dot(hid, wd_ref[0]).astype(ys_ref.dtype).reshape(bb, 1, cap, d)


def _experts(xs, wg, wu, wd, bb):
    b, e, cap, d = xs.shape
    ff = wg.shape[-1]
    tok = pl.BlockSpec((bb, 1, cap, d), lambda ei, bi: (bi, ei, 0, 0))
    return pl.pallas_call(
        _expert_kernel,
        grid=(e, b // bb),
        in_specs=[tok,
                  pl.BlockSpec((1, d, ff), lambda ei, bi: (ei, 0, 0)),
                  pl.BlockSpec((1, d, ff), lambda ei, bi: (ei, 0, 0)),
                  pl.BlockSpec((1, ff, d), lambda ei, bi: (ei, 0, 0))],
        out_specs=tok,
        out_shape=jax.ShapeDtypeStruct((b, e, cap, d), BF16),
        compiler_params=_params("parallel", "arbitrary"),
    )(xs, wg, wu, wd)


def _scatter_kernel(slot_ref, aff_ref, ys_ref, x1_ref, g_ref, o_ref, *, final_norm):
    n_exp, cap = ys_ref.shape[1], ys_ref.shape[2]
    slot = slot_ref[0]
    aff = aff_ref[0]
    cidx = lax.broadcasted_iota(jnp.int32, (1, cap), 1)
    acc = x1_ref[0]
    for e in range(n_exp):
        onehot = jnp.where(slot[:, e:e + 1] == cidx, 1.0, 0.0).astype(BF16)
        acc = acc + aff[:, e:e + 1] * _dot(onehot, ys_ref[0, e])
    if final_norm:
        acc = acc * lax.rsqrt(jnp.mean(acc * acc, axis=-1, keepdims=True) + NORM_EPS) * g_ref[...]
    o_ref[0] = acc


def _scatter(slot_cols, aff, ys, x1, g, tt, final_norm):
    b, t, d = x1.shape
    e, cap = ys.shape[1], ys.shape[2]
    return pl.pallas_call(
        functools.partial(_scatter_kernel, final_norm=final_norm),
        grid=(b, t // tt),
        in_specs=[pl.BlockSpec((1, tt, e), lambda bi, ti: (bi, ti, 0)),
                  pl.BlockSpec((1, tt, e), lambda bi, ti: (bi, ti, 0)),
                  pl.BlockSpec((1, e, cap, d), lambda bi, ti: (bi, 0, 0, 0)),
                  pl.BlockSpec((1, tt, d), lambda bi, ti: (bi, ti, 0)),
                  pl.BlockSpec((1, d), lambda bi, ti: (0, 0))],
        out_specs=pl.BlockSpec((1, tt, d), lambda bi, ti: (bi, ti, 0)),
        out_shape=jax.ShapeDtypeStruct((b, t, d), F32),
        compiler_params=_params("parallel", "arbitrary"),
    )(slot_cols, aff, ys, x1, g.reshape(1, d).astype(F32))


def _largest_divisor(n, candidates):
    for c in candidates:
        if n % c == 0:
            return c
    return 1


def kernel(x, norm_mix_g, w_in, mu_prev, mu_next, w0_f, w_up_f, w0_b, w_up_b, a0_f, a_up_f, a0_b, a_up_b, g_up, k_k, k_a, r_k, ln_x_w, ln_x_b, attn_sink, w_proj_rwkv, w_proj_attn, w_out, norm_ffn_g, w_router, exp_w_gate, exp_w_up, exp_w_down, norm_final_g):
    b, t, d = x.shape
    depth = w_in.shape[0]
    n = b * t
    tm = _largest_divisor(t, (512, 256, 128))
    cap = CAPACITY_FACTOR * t // N_EXPERTS
    bb = _largest_divisor(b, (4, 2))
    xc = x
    for l in range(depth):
        x2 = xc.reshape(n, d)
        u_r, qkv, gates = _in_projection(x2, norm_mix_g[l].astype(F32), w_in[l].astype(BF16), tm)
        y_r = _rwkv_branch(u_r.reshape(b, t, RWKV_COLS), mu_prev[l], mu_next[l], w0_f[l], w_up_f[l], w0_b[l],
                           w_up_b[l], a0_f[l], a_up_f[l], a0_b[l], a_up_b[l], g_up[l], k_k[l], k_a[l], r_k[l],
                           ln_x_w[l], ln_x_b[l])
        y_a = _attention(qkv.reshape(b, t, QKV_COLS), attn_sink[l])
        last = l == depth - 1
        x1, h2, aff, aff_t = _merge(y_r.reshape(n, RWKV_DIM), y_a.reshape(n, ATT_DIM), gates, x2, w_proj_rwkv[l],
                                    w_proj_attn[l], w_out[l], norm_ffn_g[l], w_router[l], tm, t)
        slot = _select(aff_t, cap)
        xs = _gather(slot.reshape(b * N_EXPERTS, 1, t), h2.reshape(b, t, d), cap)
        ys = _experts(xs, exp_w_gate[l].astype(BF16), exp_w_up[l].astype(BF16), exp_w_down[l].astype(BF16), bb)
        xc = _scatter(jnp.swapaxes(slot, 1, 2), aff.reshape(b, t, N_EXPERTS), ys, x1.reshape(b, t, d),
                      norm_final_g, tm, final_norm=last)
    return xc
```

```python
import functools
import math

import numpy as np
import jax
import jax.numpy as jnp
from jax import lax
from jax.experimental import pallas as pl
from jax.experimental.pallas import tpu as pltpu

F32 = jnp.float32
BF16 = jnp.bfloat16

RWKV_HEAD = 64
RWKV_HEADS = 8
RWKV_DIM = RWKV_HEADS * RWKV_HEAD
W_LORA = 64
A_LORA = 64
G_LORA = 128
DECAY_SCALE = math.exp(-0.5)
LNX_EPS = 64e-5
ATT_HEADS = 8
ATT_KV_HEADS = 2
ATT_HEAD = 64
ATT_DIM = ATT_HEADS * ATT_HEAD
KV_DIM = ATT_KV_HEADS * ATT_HEAD
WINDOW = 128
BLOCK = 128
NEG_INF = -1e30
N_EXPERTS = 16
CAPACITY_FACTOR = 2
NORM_EPS = 1e-6
RWKV_COLS = 3 * RWKV_DIM + G_LORA + 2 * W_LORA + 2 * A_LORA
QKV_COLS = ATT_DIM + 2 * KV_DIM
OFF_R, OFF_K, OFF_V = 0, RWKV_DIM, 2 * RWKV_DIM
OFF_G = 3 * RWKV_DIM
OFF_WF = OFF_G + G_LORA
OFF_WB = OFF_WF + W_LORA
OFF_AF = OFF_WB + W_LORA
OFF_AB = OFF_AF + A_LORA

LANES = 128
BF16_SUBLANES = 16
VMEM_LIMIT = 48 * 1024 * 1024

CHUNK = 64


def _dot(a, b):
    return jnp.dot(a, b, preferred_element_type=F32)


def _dot_nt(a, b):
    return lax.dot_general(a, b, (((1,), (1,)), ((), ())), preferred_element_type=F32)


def _split2(x):
    hi = x.astype(BF16)
    lo = (x - hi.astype(F32)).astype(BF16)
    return hi, lo


def _split3(x):
    hi = x.astype(BF16)
    r1 = x - hi.astype(F32)
    mid = r1.astype(BF16)
    lo = (r1 - mid.astype(F32)).astype(BF16)
    return hi, mid, lo


def _seg_sum(x, bd):
    hi, lo = _split2(x)
    return _dot(hi, bd) + _dot(lo, bd)


def _sigmoid(x):
    return 1.0 / (1.0 + jnp.exp(-x))


def _largest_divisor(n, candidates):
    for c in candidates:
        if n % c == 0:
            return c
    return 1


def _params(*sem):
    return pltpu.CompilerParams(dimension_semantics=sem, vmem_limit_bytes=VMEM_LIMIT)


def _inproj_kernel(x_ref, g_ref, w_ref, ur_ref, qkv_ref, gate_ref, *, col_chunk):
    x = x_ref[...]
    h = x * lax.rsqrt(jnp.mean(x * x, axis=-1, keepdims=True) + NORM_EPS) * g_ref[...]
    hb = h.astype(BF16)
    c0 = 0
    for ref in (ur_ref, qkv_ref, gate_ref):
        width = ref.shape[-1]
        for j in range(0, width, col_chunk):
            w = min(col_chunk, width - j)
            ref[:, j:j + w] = _dot(hb, w_ref[:, c0 + j:c0 + j + w]).astype(BF16)
        c0 += width


def _in_projection(x2, g, w_in_bf, tm):
    n, d = x2.shape
    cols = w_in_bf.shape[1]
    gate_cols = cols - RWKV_COLS - QKV_COLS
    return pl.pallas_call(
        functools.partial(_inproj_kernel, col_chunk=512),
        grid=(n // tm,),
        in_specs=[pl.BlockSpec((tm, d), lambda i: (i, 0)),
                  pl.BlockSpec((1, d), lambda i: (0, 0)),
                  pl.BlockSpec((d, cols), lambda i: (0, 0))],
        out_specs=[pl.BlockSpec((tm, RWKV_COLS), lambda i: (i, 0)),
                   pl.BlockSpec((tm, QKV_COLS), lambda i: (i, 0)),
                   pl.BlockSpec((tm, gate_cols), lambda i: (i, 0))],
        out_shape=[jax.ShapeDtypeStruct((n, RWKV_COLS), BF16),
                   jax.ShapeDtypeStruct((n, QKV_COLS), BF16),
                   jax.ShapeDtypeStruct((n, gate_cols), BF16)],
        compiler_params=_params("parallel"),
    )(x2, g.reshape(1, d), w_in_bf)


GROUP_HEADS = 4
GROUP = GROUP_HEADS * RWKV_HEAD
N_GROUPS = RWKV_HEADS // GROUP_HEADS
ROWS_PER_STEP = 256


def _rwkv_kernel(*refs, reverse, final, n_steps):
    if final:
        (u_ref, up_ref, un_ref, mup_ref, mun_ref, w0_ref, wup_ref, a0_ref, aup_ref, kk_ref, ka_ref,
         bd_ref, tri_ref, ones_ref, a0o_ref, aupo_ref, gup_ref, rk_ref, lnw_ref, lnb_ref, yb_ref,
         o_ref, st_ref) = refs
    else:
        (u_ref, up_ref, un_ref, mup_ref, mun_ref, w0_ref, wup_ref, a0_ref, aup_ref, kk_ref, ka_ref,
         bd_ref, tri_ref, ones_ref, o_ref, st_ref) = refs
    TB = u_ref.shape[1]
    L = CHUNK
    N = RWKV_HEAD
    n_ck = TB // L
    step = pl.program_id(1)
    blk = (n_steps - 1 - step) if reverse else step

    @pl.when(step == 0)
    def _():
        st_ref[...] = jnp.zeros_like(st_ref)

    u = u_ref[0].astype(F32)
    prev_row = up_ref[0].astype(F32)[BF16_SUBLANES - 1:BF16_SUBLANES]
    next_row = un_ref[0].astype(F32)[0:1]
    prev_row = jnp.where(blk == 0, 0.0, prev_row)
    next_row = jnp.where(blk == n_steps - 1, 0.0, next_row)
    row = lax.broadcasted_iota(jnp.int32, (TB, 1), 0)
    prev = jnp.where(row == 0, prev_row, pltpu.roll(u, 1, axis=0))
    nxt = jnp.where(row == TB - 1, next_row, pltpu.roll(u, TB - 1, axis=0))
    us = u + mup_ref[...] * (prev - u) + mun_ref[...] * (nxt - u)

    r = us[:, OFF_R:OFF_R + RWKV_DIM]
    k = us[:, OFF_K:OFF_K + RWKV_DIM]
    v = us[:, OFF_V:OFF_V + RWKV_DIM]
    off_w = OFF_WB if reverse else OFF_WF
    off_a = OFF_AB if reverse else OFF_AF
    w_lo = us[:, off_w:off_w + W_LORA]
    a_lo = us[:, off_a:off_a + A_LORA]
    bd = bd_ref[...]

    lw = -DECAY_SCALE * _sigmoid(w0_ref[...] + _dot(jnp.tanh(w_lo).astype(BF16), wup_ref[...]))
    a = _sigmoid(a0_ref[...] + _dot(a_lo.astype(BF16), aup_ref[...]))
    kkr = k * kk_ref[...]
    kk = kkr / jnp.maximum(jnp.sqrt(_seg_sum(kkr * kkr, bd)), 1e-12)
    kd = k * (1.0 + (a - 1.0) * ka_ref[...])
    b = a * kk

    tri = tri_ref[...]
    ones = ones_ref[...]
    l_hi, l_mid, l_lo = _split3(lw)
    cs = _dot(tri, l_hi) + _dot(tri, l_mid) + _dot(tri, l_lo)
    tot = _dot(ones, l_hi) + _dot(ones, l_mid) + _dot(ones, l_lo)
    e_tot = jnp.exp(tot)
    a_t = (kk * jnp.exp(cs - lw)).astype(BF16)
    r_t = (r * jnp.exp(cs)).astype(BF16)
    e_neg = jnp.exp(-cs)
    k_t = (kd * e_neg).astype(BF16)
    b_t = (b * e_neg).astype(BF16)
    e_rem = jnp.exp(tot - cs)
    k_h = (kd * e_rem).astype(BF16)
    b_h = (b * e_rem).astype(BF16)
    v_b = v.astype(BF16)

    ii = lax.broadcasted_iota(jnp.int32, (GROUP, GROUP), 0)
    jj = lax.broadcasted_iota(jnp.int32, (GROUP, GROUP), 1)
    same_head = (ii // N) == (jj // N)
    if reverse:
        strict = jj > ii
        incl = jj >= ii
    else:
        strict = jj < ii
        incl = jj <= ii
    eye = (ii == jj).astype(F32)
    level_masks = []
    s = 1
    while s < L:
        same = (ii // (2 * s)) == (jj // (2 * s))
        t_hi = (ii & s) != 0
        s_hi = (jj & s) != 0
        if reverse:
            level_masks.append(same & jnp.logical_not(t_hi) & s_hi)
        else:
            level_masks.append(same & t_hi & jnp.logical_not(s_hi))
        s *= 2

    def stack_rows(xc):
        return jnp.where(same_head, jnp.concatenate([xc] * GROUP_HEADS, axis=0), 0.0).astype(BF16)

    def stack_cols(xt):
        return jnp.where(same_head, jnp.concatenate([xt] * GROUP_HEADS, axis=1), 0.0).astype(BF16)

    order = list(range(n_ck))[::-1] if reverse else list(range(n_ck))
    cgs = [(c, g) for c in order for g in range(N_GROUPS)]

    def tile(x, c, g):
        return x[c * L:(c + 1) * L, g * GROUP:(g + 1) * GROUP]

    a_s = {cg: stack_rows(tile(a_t, *cg)) for cg in cgs}
    b_s = {cg: stack_rows(tile(b_t, *cg)) for cg in cgs}
    k_s = {cg: stack_rows(tile(k_t, *cg)) for cg in cgs}
    r_s = {cg: stack_rows(tile(r_t, *cg)) for cg in cgs}
    ab = {cg: _dot_nt(a_s[cg], b_s[cg]) for cg in cgs}
    ak = {cg: _dot_nt(a_s[cg], k_s[cg]) for cg in cgs}
    rk_ = {cg: _dot_nt(r_s[cg], k_s[cg]) for cg in cgs}
    rb = {cg: _dot_nt(r_s[cg], b_s[cg]) for cg in cgs}
    x = {cg: eye - jnp.where(level_masks[0], ab[cg], 0.0) for cg in cgs}
    for m in level_masks[1:]:
        xb = {cg: x[cg].astype(BF16) for cg in cgs}
        xn = {cg: _dot(xb[cg], jnp.where(m, ab[cg], 0.0).astype(BF16)).astype(BF16) for cg in cgs}
        x = {cg: x[cg] - _dot(xn[cg], xb[cg]) for cg in cgs}
    xb = {cg: x[cg].astype(BF16) for cg in cgs}
    xa = {cg: _dot(xb[cg], a_s[cg]).astype(BF16) for cg in cgs}
    v_s = {cg: stack_rows(tile(v_b, *cg)) for cg in cgs}
    vt_s = {cg: stack_cols(tile(v, *cg).T) for cg in cgs}
    mak = {cg: jnp.where(strict, ak[cg], 0.0).astype(BF16) for cg in cgs}
    mrk = {cg: jnp.where(incl, rk_[cg], 0.0).astype(BF16) for cg in cgs}
    mrb = {cg: jnp.where(incl, rb[cg], 0.0).astype(BF16) for cg in cgs}
    vmt = {cg: _dot_nt(vt_s[cg], mak[cg]).astype(BF16) for cg in cgs}
    xmvt = {cg: _dot_nt(vmt[cg], xb[cg]) for cg in cgs}
    kh_s = {cg: stack_rows(tile(k_h, *cg)) for cg in cgs}
    bh_s = {cg: stack_rows(tile(b_h, *cg)) for cg in cgs}
    vk = {cg: _dot(vt_s[cg], kh_s[cg]) for cg in cgs}
    mrkv = {cg: _dot(mrk[cg], v_s[cg]) for cg in cgs}

    state = [st_ref[g] for g in range(N_GROUPS)]
    y_tiles = {}
    for c in order:
        for g in range(N_GROUPS):
            cg = (c, g)
            sb = state[g].astype(BF16)
            ut = (_dot_nt(sb, xa[cg]) + xmvt[cg]).astype(BF16)
            y_stack = _dot_nt(r_s[cg], sb) + mrkv[cg] - _dot_nt(mrb[cg], ut)
            state[g] = state[g] * tile(e_tot, c, g)[0:1] + vk[cg] - _dot(ut, bh_s[cg])
            y = y_stack[0:L]
            for h in range(1, GROUP_HEADS):
                y = y + y_stack[h * L:(h + 1) * L]
            y_tiles[cg] = y
    for g in range(N_GROUPS):
        st_ref[g] = state[g]
    wkv = jnp.concatenate(
        [jnp.concatenate([y_tiles[(c, g)] for g in range(N_GROUPS)], axis=1) for c in range(n_ck)], axis=0)

    if not final:
        o_ref[0] = wkv
        return

    wkv = wkv + yb_ref[0]
    inv_n = 1.0 / N
    mean = _seg_sum(wkv, bd) * inv_n
    xc = wkv - mean
    var = _seg_sum(xc * xc, bd) * inv_n
    normed = xc * lax.rsqrt(var + LNX_EPS) * lnw_ref[...] + lnb_ref[...]
    ao_lo = us[:, OFF_AB:OFF_AB + A_LORA]
    a_o = _sigmoid(a0o_ref[...] + _dot(ao_lo.astype(BF16), aupo_ref[...]))
    kd_o = k * (1.0 + (a_o - 1.0) * ka_ref[...])
    rk = rk_ref[...]
    bonus = (_seg_sum(r * kd * rk, bd) + _seg_sum(r * kd_o * rk, bd)) * v
    g_lo = us[:, OFF_G:OFF_G + G_LORA]
    gg = _dot(_sigmoid(g_lo).astype(BF16), gup_ref[...])
    o_ref[0] = ((normed + bonus) * gg).astype(o_ref.dtype)


def _rwkv_pass(u_r, consts, *, reverse, final, extra=()):
    b, t, _ = u_r.shape
    tb = _largest_divisor(t, (ROWS_PER_STEP, CHUNK))
    n_steps = t // tb
    hb = tb // BF16_SUBLANES
    n_halo = t // BF16_SUBLANES

    def sidx(s):
        return (n_steps - 1 - s) if reverse else s

    def full(shape):
        return pl.BlockSpec(shape, lambda bi, s: (0,) * len(shape))

    in_specs = [
        pl.BlockSpec((1, tb, RWKV_COLS), lambda bi, s: (bi, sidx(s), 0)),
        pl.BlockSpec((1, BF16_SUBLANES, RWKV_COLS), lambda bi, s: (bi, jnp.maximum(sidx(s) * hb - 1, 0), 0)),
        pl.BlockSpec((1, BF16_SUBLANES, RWKV_COLS),
                     lambda bi, s: (bi, jnp.minimum((sidx(s) + 1) * hb, n_halo - 1), 0)),
    ]
    args = [u_r, u_r, u_r]
    for arr in consts:
        in_specs.append(full(arr.shape))
        args.append(arr)
    if final:
        for arr in extra[:-1]:
            in_specs.append(full(arr.shape))
            args.append(arr)
        in_specs.append(pl.BlockSpec((1, tb, RWKV_DIM), lambda bi, s: (bi, sidx(s), 0)))
        args.append(extra[-1])
    out_dtype = BF16 if final else F32
    return pl.pallas_call(
        functools.partial(_rwkv_kernel, reverse=reverse, final=final, n_steps=n_steps),
        grid=(b, n_steps),
        in_specs=in_specs,
        out_specs=pl.BlockSpec((1, tb, RWKV_DIM), lambda bi, s: (bi, sidx(s), 0)),
        out_shape=jax.ShapeDtypeStruct((b, t, RWKV_DIM), out_dtype),
        scratch_shapes=[pltpu.VMEM((N_GROUPS, GROUP, GROUP), F32)],
        compiler_params=_params("parallel", "arbitrary"),
    )(*args)


def _rwkv_branch(u_r, mu_prev, mu_next, w0_f, w_up_f, w0_b, w_up_b, a0_f, a_up_f, a0_b, a_up_b,
                 g_up, k_k, k_a, r_k, ln_w, ln_b):
    t = u_r.shape[1]
    tb = _largest_divisor(t, (ROWS_PER_STEP, CHUNK))
    row = lambda a: a.reshape(1, -1).astype(F32)
    seg = np.arange(RWKV_DIM) // RWKV_HEAD
    bd = jnp.asarray(seg[:, None] == seg[None, :], BF16)
    ti = np.arange(tb)
    same_chunk = (ti[:, None] // CHUNK) == (ti[None, :] // CHUNK)
    tri_f = jnp.asarray(same_chunk & (ti[None, :] <= ti[:, None]), BF16)
    tri_b = jnp.asarray(same_chunk & (ti[None, :] >= ti[:, None]), BF16)
    ones = jnp.asarray(same_chunk, BF16)
    common = lambda w0, wup, a0, aup, tri: [row(mu_prev), row(mu_next), row(w0), wup.astype(BF16), row(a0),
                                            aup.astype(BF16), row(k_k), row(k_a), bd, tri, ones]
    y_b = _rwkv_pass(u_r, common(w0_b, w_up_b, a0_b, a_up_b, tri_b), reverse=True, final=False)
    extra = [row(a0_b), a_up_b.astype(BF16), g_up.astype(BF16), row(r_k), row(ln_w), row(ln_b), y_b]
    return _rwkv_pass(u_r, common(w0_f, w_up_f, a0_f, a_up_f, tri_f), reverse=False, final=True, extra=extra)


def _attn_kernel(sink_ref, q_ref, kp_ref, kc_ref, kn_ref, vp_ref, vc_ref, vn_ref, o_ref, *, seq_len):
    i = pl.program_id(1)
    span = BLOCK + 2 * WINDOW
    kk = jnp.concatenate([kp_ref[0], kc_ref[0], kn_ref[0]], axis=0)
    vv = jnp.concatenate([vp_ref[0], vc_ref[0], vn_ref[0]], axis=0)
    li = lax.broadcasted_iota(jnp.int32, (BLOCK, span), 0)
    ji = lax.broadcasted_iota(jnp.int32, (BLOCK, span), 1)
    rel = jnp.abs(ji - WINDOW - li)
    key_pos = i * BLOCK - WINDOW + ji
    valid = (rel <= WINDOW) & (key_pos >= 0) & (key_pos < seq_len)
    arel = rel.astype(F32)
    group = ATT_HEADS // ATT_KV_HEADS
    scale = ATT_HEAD ** -0.5
    outs = []
    for h in range(ATT_HEADS):
        kv = h // group
        slope = float(2.0 ** (-8.0 * (h + 1) / ATT_HEADS))
        qh = q_ref[0, :, h * ATT_HEAD:(h + 1) * ATT_HEAD]
        kh = kk[:, kv * ATT_HEAD:(kv + 1) * ATT_HEAD]
        vh = vv[:, kv * ATT_HEAD:(kv + 1) * ATT_HEAD]
        s = _dot_nt(qh, kh) * scale - slope * arel
        s = jnp.where(valid, s, NEG_INF)
        sink = sink_ref[h]
        m = jnp.maximum(jnp.max(s, axis=-1, keepdims=True), sink)
        p = jnp.exp(s - m)
        denom = jnp.sum(p, axis=-1, keepdims=True) + jnp.exp(sink - m)
        outs.append(_dot(p.astype(BF16), vh) / denom)
    o_ref[0] = jnp.concatenate(outs, axis=-1).astype(o_ref.dtype)


def _attention(qkv, sink):
    b, t, _ = qkv.shape
    nb = t // BLOCK
    kcol = ATT_DIM // KV_DIM
    prev = lambda i: jnp.maximum(i - 1, 0)
    nxt = lambda i: jnp.minimum(i + 1, nb - 1)
    kv_spec = lambda col, f: pl.BlockSpec((1, BLOCK, KV_DIM), lambda bi, i: (bi, f(i), col))
    same = lambda i: i
    return pl.pallas_call(
        functools.partial(_attn_kernel, seq_len=t),
        grid=(b, nb),
        in_specs=[pl.BlockSpec(memory_space=pltpu.SMEM),
                  pl.BlockSpec((1, BLOCK, ATT_DIM), lambda bi, i: (bi, i, 0)),
                  kv_spec(kcol, prev), kv_spec(kcol, same), kv_spec(kcol, nxt),
                  kv_spec(kcol + 1, prev), kv_spec(kcol + 1, same), kv_spec(kcol + 1, nxt)],
        out_specs=pl.BlockSpec((1, BLOCK, ATT_DIM), lambda bi, i: (bi, i, 0)),
        out_shape=jax.ShapeDtypeStruct((b, t, ATT_DIM), BF16),
        compiler_params=_params("parallel", "parallel"),
    )(sink.astype(F32), qkv, qkv, qkv, qkv, qkv, qkv, qkv)


def _merge_kernel(yr_ref, ya_ref, gate_ref, x_ref, wpr_ref, wpa_ref, wo_ref, g_ref, wrt_ref,
                  x1_ref, h2_ref, aff_ref, afft_ref):
    d = x_ref.shape[-1]
    gl_r = gate_ref[:, 0:d].astype(F32)
    gl_a = gate_ref[:, d:2 * d].astype(F32)
    merged = (_sigmoid(gl_r) * _dot(yr_ref[...], wpr_ref[...])
              + _sigmoid(gl_a) * _dot(ya_ref[...], wpa_ref[...]))
    x1 = x_ref[...] + _dot(merged.astype(BF16), wo_ref[...])
    x1_ref[...] = x1
    h2 = (x1 * lax.rsqrt(jnp.mean(x1 * x1, axis=-1, keepdims=True) + NORM_EPS) * g_ref[...]).astype(BF16)
    h2_ref[...] = h2
    logits_t = _dot_nt(wrt_ref[...], h2)
    mx = jnp.max(logits_t, axis=0, keepdims=True)
    ex = jnp.exp(logits_t - mx)
    aff_t = ex / jnp.sum(ex, axis=0, keepdims=True)
    afft_ref[0] = aff_t
    aff_ref[...] = aff_t.T


def _merge(y_r, y_a, gates, x2, wpr, wpa, wo, g, w_router, tm, seq_len):
    n, d = x2.shape
    e = w_router.shape[1]
    tiles_per_seq = seq_len // tm
    full = lambda shape: pl.BlockSpec(shape, lambda i: (0,) * len(shape))
    rows = lambda w: pl.BlockSpec((tm, w), lambda i: (i, 0))
    return pl.pallas_call(
        _merge_kernel,
        grid=(n // tm,),
        in_specs=[rows(RWKV_DIM), rows(ATT_DIM), rows(2 * d), rows(d),
                  full(wpr.shape), full(wpa.shape), full(wo.shape), full((1, d)), full((e, d))],
        out_specs=[rows(d), rows(d), rows(e),
                   pl.BlockSpec((1, e, tm), lambda i: (i // tiles_per_seq, 0, i % tiles_per_seq))],
        out_shape=[jax.ShapeDtypeStruct((n, d), F32), jax.ShapeDtypeStruct((n, d), BF16),
                   jax.ShapeDtypeStruct((n, e), F32),
                   jax.ShapeDtypeStruct((n // seq_len, e, seq_len), F32)],
        compiler_params=_params("parallel"),
    )(y_r, y_a, gates, x2, wpr.astype(BF16), wpa.astype(BF16), wo.astype(BF16), g.reshape(1, d).astype(F32),
      w_router.T.astype(BF16))


def _select_kernel(a_ref, gsame_ref, gprev_ref, triu_ref, slot_ref, *, cap):
    bits = pltpu.bitcast(a_ref[0], jnp.int32)
    gsame = gsame_ref[...]
    gprev = gprev_ref[...]
    triu = triu_ref[...]
    rows = bits.shape[0]
    capf = float(cap)

    def group_count(mask):
        per_lane = _dot(gsame, jnp.where(mask, 1.0, 0.0).astype(BF16))
        return jnp.sum(per_lane, axis=-1, keepdims=True)

    def search(i, prefix):
        cand = prefix | lax.shift_left(jnp.int32(1), 30 - i)
        return jnp.where(group_count(bits >= cand) >= capf, cand, prefix)

    thr = lax.fori_loop(0, 31, search, jnp.zeros((rows, 1), jnp.int32))

    def prefix_count(mask):
        mb = jnp.where(mask, 1.0, 0.0)
        inc = _dot(mb.astype(BF16), triu)
        tot = jnp.broadcast_to(inc[:, LANES - 1:LANES], inc.shape).astype(BF16)
        return inc - mb + _dot(gprev, tot)

    gt = bits > thr
    eq = bits == thr
    need = capf - group_count(gt)
    sel = gt | (eq & (prefix_count(eq) < need))
    pos = prefix_count(sel)
    slot_ref[0] = jnp.where(sel, pos, -1.0).astype(jnp.int32)


def _select(aff_t, cap):
    b, e, t = aff_t.shape
    tiles = t // LANES
    rows = e * tiles
    rid = np.arange(rows)
    same = (rid[:, None] // tiles) == (rid[None, :] // tiles)
    gsame = jnp.asarray(same, BF16)
    gprev = jnp.asarray(same & (rid[None, :] < rid[:, None]), BF16)
    li = np.arange(LANES)
    triu = jnp.asarray(li[:, None] <= li[None, :], BF16)
    full = lambda shape: pl.BlockSpec(shape, lambda bi: (0,) * len(shape))
    slot = pl.pallas_call(
        functools.partial(_select_kernel, cap=cap),
        grid=(b,),
        in_specs=[pl.BlockSpec((1, rows, LANES), lambda bi: (bi, 0, 0)),
                  full((rows, rows)), full((rows, rows)), full((LANES, LANES))],
        out_specs=pl.BlockSpec((1, rows, LANES), lambda bi: (bi, 0, 0)),
        out_shape=jax.ShapeDtypeStruct((b, rows, LANES), jnp.int32),
        compiler_params=_params("parallel"),
    )(aff_t.reshape(b, rows, LANES), gsame, gprev, triu)
    return slot.reshape(b, e, t)


def _gather_kernel(slot_ref, h_ref, xs_ref):
    cap = xs_ref.shape[2]
    slot = slot_ref[0]
    cidx = lax.broadcasted_iota(jnp.int32, (cap, 1), 0)
    onehot = jnp.where(cidx == slot, 1.0, 0.0).astype(BF16)
    xs_ref[0, 0] = _dot(onehot, h_ref[0]).astype(xs_ref.dtype)


def _gather(slot_rows, h2, cap):
    b, t, d = h2.shape
    e = slot_rows.shape[0] // b
    return pl.pallas_call(
        _gather_kernel,
        grid=(b, e),
        in_specs=[pl.BlockSpec((1, 1, t), lambda bi, ei: (bi * e + ei, 0, 0)),
                  pl.BlockSpec((1, t, d), lambda bi, ei: (bi, 0, 0))],
        out_specs=pl.BlockSpec((1, 1, cap, d), lambda bi, ei: (bi, ei, 0, 0)),
        out_shape=jax.ShapeDtypeStruct((b, e, cap, d), BF16),
        compiler_params=_params("parallel", "arbitrary"),
    )(slot_rows, h2)


def _expert_kernel(xs_ref, wg_ref, wu_ref, wd_ref, ys_ref):
    bb, _, cap, d = xs_ref.shape
    xs = xs_ref[...].reshape(bb * cap, d)
    gate = _dot(xs, wg_ref[0])
    up = _dot(xs, wu_ref[0])
    hid = (gate * _sigmoid(gate) * up).astype(BF16)
    ys_ref[...] = _dot(hid, wd_ref[0]).astype(ys_ref.dtype).reshape(bb, 1, cap, d)


def _experts(xs, wg, wu, wd, bb):
    b, e, cap, d = xs.shape
    ff = wg.shape[-1]
    tok = pl.BlockSpec((bb, 1, cap, d), lambda ei, bi: (bi, ei, 0, 0))
    return pl.pallas_call(
        _expert_kernel,
        grid=(e, b // bb),
        in_specs=[tok,
                  pl.BlockSpec((1, d, ff), lambda ei, bi: (ei, 0, 0)),
                  pl.BlockSpec((1, d, ff), lambda ei, bi: (ei, 0, 0)),
                  pl.BlockSpec((1, ff, d), lambda ei, bi: (ei, 0, 0))],
        out_specs=tok,
        out_shape=jax.ShapeDtypeStruct((b, e, cap, d), BF16),
        compiler_params=_params("parallel", "arbitrary"),
    )(xs, wg, wu, wd)


def _scatter_kernel(slot_ref, aff_ref, ys_ref, x1_ref, g_ref, o_ref, *, final_norm):
    n_exp, cap = ys_ref.shape[1], ys_ref.shape[2]
    slot = slot_ref[0]
    aff = aff_ref[0]
    cidx = lax.broadcasted_iota(jnp.int32, (1, cap), 1)
    acc = x1_ref[0]
    for e in range(n_exp):
        onehot = jnp.where(slot[:, e:e + 1] == cidx, 1.0, 0.0).astype(BF16)
        acc = acc + aff[:, e:e + 1] * _dot(onehot, ys_ref[0, e])
    if final_norm:
        acc = acc * lax.rsqrt(jnp.mean(acc * acc, axis=-1, keepdims=True) + NORM_EPS) * g_ref[...]
    o_ref[0] = acc


def _scatter(slot_cols, aff, ys, x1, g, tt, final_norm):
    b, t, d = x1.shape
    e, cap = ys.shape[1], ys.shape[2]
    return pl.pallas_call(
        functools.partial(_scatter_kernel, final_norm=final_norm),
        grid=(b, t // tt),
        in_specs=[pl.BlockSpec((1, tt, e), lambda bi, ti: (bi, ti, 0)),
                  pl.BlockSpec((1, tt, e), lambda bi, ti: (bi, ti, 0)),
                  pl.BlockSpec((1, e, cap, d), lambda bi, ti: (bi, 0, 0, 0)),
                  pl.BlockSpec((1, tt, d), lambda bi, ti: (bi, ti, 0)),
                  pl.BlockSpec((1, d), lambda bi, ti: (0, 0))],
        out_specs=pl.BlockSpec((1, tt, d), lambda bi, ti: (bi, ti, 0)),
        out_shape=jax.ShapeDtypeStruct((b, t, d), F32),
        compiler_params=_params("parallel", "arbitrary"),
    )(slot_cols, aff, ys, x1, g.reshape(1, d).astype(F32))


def kernel(x, norm_mix_g, w_in, mu_prev, mu_next, w0_f, w_up_f, w0_b, w_up_b, a0_f, a_up_f, a0_b, a_up_b, g_up, k_k, k_a, r_k, ln_x_w, ln_x_b, attn_sink, w_proj_rwkv, w_proj_attn, w_out, norm_ffn_g, w_router, exp_w_gate, exp_w_up, exp_w_down, norm_final_g):
    b, t, d = x.shape
    depth = w_in.shape[0]
    n = b * t
    tm = _largest_divisor(t, (512, 256, 128))
    cap = CAPACITY_FACTOR * t // N_EXPERTS
    bb = _largest_divisor(b, (4, 2))
    xc = x
    for l in range(depth):
        x2 = xc.reshape(n, d)
        u_r, qkv, gates = _in_projection(x2, norm_mix_g[l].astype(F32), w_in[l].astype(BF16), tm)
        y_r = _rwkv_branch(u_r.reshape(b, t, RWKV_COLS), mu_prev[l], mu_next[l], w0_f[l], w_up_f[l], w0_b[l],
                           w_up_b[l], a0_f[l], a_up_f[l], a0_b[l], a_up_b[l], g_up[l], k_k[l], k_a[l], r_k[l],
                           ln_x_w[l], ln_x_b[l])
        y_a = _attention(qkv.reshape(b, t, QKV_COLS), attn_sink[l])
        last = l == depth - 1
        x1, h2, aff, aff_t = _merge(y_r.reshape(n, RWKV_DIM), y_a.reshape(n, ATT_DIM), gates, x2, w_proj_rwkv[l],
                                    w_proj_attn[l], w_out[l], norm_ffn_g[l], w_router[l], tm, t)
        slot = _select(aff_t, cap)
        xs = _gather(slot.reshape(b * N_EXPERTS, 1, t), h2.reshape(b, t, d), cap)
        ys = _experts(xs, exp_w_gate[l].astype(BF16), exp_w_up[l].astype(BF16), exp_w_down[l].astype(BF16), bb)
        xc = _scatter(jnp.swapaxes(slot, 1, 2), aff.reshape(b, t, N_EXPERTS), ys, x1.reshape(b, t, d),
                      norm_final_g, tm, final_norm=last)
    return xc
```

```python
import functools
import math

import numpy as np
import jax
import jax.numpy as jnp
from jax import lax
from jax.experimental import pallas as pl
from jax.experimental.pallas import tpu as pltpu

F32 = jnp.float32
BF16 = jnp.bfloat16

RWKV_HEAD = 64
RWKV_HEADS = 8
RWKV_DIM = RWKV_HEADS * RWKV_HEAD
W_LORA = 64
A_LORA = 64
G_LORA = 128
DECAY_SCALE = math.exp(-0.5)
LNX_EPS = 64e-5
ATT_HEADS = 8
ATT_KV_HEADS = 2
ATT_HEAD = 64
ATT_DIM = ATT_HEADS * ATT_HEAD
KV_DIM = ATT_KV_HEADS * ATT_HEAD
WINDOW = 128
BLOCK = 128
NEG_INF = -1e30
N_EXPERTS = 16
CAPACITY_FACTOR = 2
NORM_EPS = 1e-6
RWKV_COLS = 3 * RWKV_DIM + G_LORA + 2 * W_LORA + 2 * A_LORA
QKV_COLS = ATT_DIM + 2 * KV_DIM
OFF_R, OFF_K, OFF_V = 0, RWKV_DIM, 2 * RWKV_DIM
OFF_G = 3 * RWKV_DIM
OFF_WF = OFF_G + G_LORA
OFF_WB = OFF_WF + W_LORA
OFF_AF = OFF_WB + W_LORA
OFF_AB = OFF_AF + A_LORA

LANES = 128
BF16_SUBLANES = 16
VMEM_LIMIT = 48 * 1024 * 1024

CHUNK = 64


def _dot(a, b):
    return jnp.dot(a, b, preferred_element_type=F32)


def _dot_nt(a, b):
    return lax.dot_general(a, b, (((1,), (1,)), ((), ())), preferred_element_type=F32)


def _split2(x):
    hi = x.astype(BF16)
    lo = (x - hi.astype(F32)).astype(BF16)
    return hi, lo


def _split3(x):
    hi = x.astype(BF16)
    r1 = x - hi.astype(F32)
    mid = r1.astype(BF16)
    lo = (r1 - mid.astype(F32)).astype(BF16)
    return hi, mid, lo


def _seg_sum(x, bd):
    hi, lo = _split2(x)
    return _dot(hi, bd) + _dot(lo, bd)


def _sigmoid(x):
    return 1.0 / (1.0 + jnp.exp(-x))


def _largest_divisor(n, candidates):
    for c in candidates:
        if n % c == 0:
            return c
    return 1


def _params(*sem):
    return pltpu.CompilerParams(dimension_semantics=sem, vmem_limit_bytes=VMEM_LIMIT)


def _inproj_kernel(x_ref, g_ref, w_ref, ur_ref, qkv_ref, gate_ref, *, col_chunk):
    x = x_ref[...]
    h = x * lax.rsqrt(jnp.mean(x * x, axis=-1, keepdims=True) + NORM_EPS) * g_ref[...]
    hb = h.astype(BF16)
    c0 = 0
    for ref in (ur_ref, qkv_ref, gate_ref):
        width = ref.shape[-1]
        for j in range(0, width, col_chunk):
            w = min(col_chunk, width - j)
            ref[:, j:j + w] = _dot(hb, w_ref[:, c0 + j:c0 + j + w]).astype(BF16)
        c0 += width


def _in_projection(x2, g, w_in_bf, tm):
    n, d = x2.shape
    cols = w_in_bf.shape[1]
    gate_cols = cols - RWKV_COLS - QKV_COLS
    return pl.pallas_call(
        functools.partial(_inproj_kernel, col_chunk=512),
        grid=(n // tm,),
        in_specs=[pl.BlockSpec((tm, d), lambda i: (i, 0)),
                  pl.BlockSpec((1, d), lambda i: (0, 0)),
                  pl.BlockSpec((d, cols), lambda i: (0, 0))],
        out_specs=[pl.BlockSpec((tm, RWKV_COLS), lambda i: (i, 0)),
                   pl.BlockSpec((tm, QKV_COLS), lambda i: (i, 0)),
                   pl.BlockSpec((tm, gate_cols), lambda i: (i, 0))],
        out_shape=[jax.ShapeDtypeStruct((n, RWKV_COLS), BF16),
                   jax.ShapeDtypeStruct((n, QKV_COLS), BF16),
                   jax.ShapeDtypeStruct((n, gate_cols), BF16)],
        compiler_params=_params("parallel"),
    )(x2, g.reshape(1, d), w_in_bf)


GROUP_HEADS = 4
GROUP = GROUP_HEADS * RWKV_HEAD
N_GROUPS = RWKV_HEADS // GROUP_HEADS
ROWS_PER_STEP = 256


def _rwkv_kernel(*refs, reverse, final, n_steps):
    if final:
        (u_ref, up_ref, un_ref, mup_ref, mun_ref, w0_ref, wup_ref, a0_ref, aup_ref, kk_ref, ka_ref,
         bd_ref, tri_ref, ones_ref, a0o_ref, aupo_ref, gup_ref, rk_ref, lnw_ref, lnb_ref, yb_ref,
         o_ref, st_ref) = refs
    else:
        (u_ref, up_ref, un_ref, mup_ref, mun_ref, w0_ref, wup_ref, a0_ref, aup_ref, kk_ref, ka_ref,
         bd_ref, tri_ref, ones_ref, o_ref, st_ref) = refs
    TB = u_ref.shape[1]
    L = CHUNK
    N = RWKV_HEAD
    n_ck = TB // L
    step = pl.program_id(1)
    blk = (n_steps - 1 - step) if reverse else step

    @pl.when(step == 0)
    def _():
        st_ref[...] = jnp.zeros_like(st_ref)

    u = u_ref[0].astype(F32)
    prev_row = up_ref[0].astype(F32)[BF16_SUBLANES - 1:BF16_SUBLANES]
    next_row = un_ref[0].astype(F32)[0:1]
    prev_row = jnp.where(blk == 0, 0.0, prev_row)
    next_row = jnp.where(blk == n_steps - 1, 0.0, next_row)
    row = lax.broadcasted_iota(jnp.int32, (TB, 1), 0)
    prev = jnp.where(row == 0, prev_row, pltpu.roll(u, 1, axis=0))
    nxt = jnp.where(row == TB - 1, next_row, pltpu.roll(u, TB - 1, axis=0))
    us = u + mup_ref[...] * (prev - u) + mun_ref[...] * (nxt - u)

    r = us[:, OFF_R:OFF_R + RWKV_DIM]
    k = us[:, OFF_K:OFF_K + RWKV_DIM]
    v = us[:, OFF_V:OFF_V + RWKV_DIM]
    off_w = OFF_WB if reverse else OFF_WF
    off_a = OFF_AB if reverse else OFF_AF
    w_lo = us[:, off_w:off_w + W_LORA]
    a_lo = us[:, off_a:off_a + A_LORA]
    bd = bd_ref[...]

    lw = -DECAY_SCALE * _sigmoid(w0_ref[...] + _dot(jnp.tanh(w_lo).astype(BF16), wup_ref[...]))
    a = _sigmoid(a0_ref[...] + _dot(a_lo.astype(BF16), aup_ref[...]))
    kkr = k * kk_ref[...]
    kk = kkr / jnp.maximum(jnp.sqrt(_seg_sum(kkr * kkr, bd)), 1e-12)
    kd = k * (1.0 + (a - 1.0) * ka_ref[...])
    b = a * kk

    tri = tri_ref[...]
    ones = ones_ref[...]
    l_hi, l_mid, l_lo = _split3(lw)
    cs = _dot(tri, l_hi) + _dot(tri, l_mid) + _dot(tri, l_lo)
    tot = _dot(ones, l_hi) + _dot(ones, l_mid) + _dot(ones, l_lo)
    e_tot = jnp.exp(tot)
    a_t = (kk * jnp.exp(cs - lw)).astype(BF16)
    r_t = (r * jnp.exp(cs)).astype(BF16)
    e_neg = jnp.exp(-cs)
    k_t = (kd * e_neg).astype(BF16)
    b_t = (b * e_neg).astype(BF16)
    e_rem = jnp.exp(tot - cs)
    k_h = (kd * e_rem).astype(BF16)
    b_hn = (-b * e_rem).astype(BF16)

    ii = lax.broadcasted_iota(jnp.int32, (GROUP, GROUP), 0)
    jj = lax.broadcasted_iota(jnp.int32, (GROUP, GROUP), 1)
    same_head = (ii // N) == (jj // N)
    ti = lax.broadcasted_iota(jnp.int32, (L, GROUP), 0)
    si = lax.broadcasted_iota(jnp.int32, (L, GROUP), 1) % N
    if reverse:
        strict = si > ti
        incl = si >= ti
    else:
        strict = si < ti
        incl = si <= ti
    eye = (si == ti).astype(F32)
    level_masks = []
    s = 1
    while s < L:
        same = (ti // (2 * s)) == (si // (2 * s))
        t_hi = (ti & s) != 0
        s_hi = (si & s) != 0
        if reverse:
            level_masks.append(same & jnp.logical_not(t_hi) & s_hi)
        else:
            level_masks.append(same & t_hi & jnp.logical_not(s_hi))
        s *= 2

    def bdiag(xp):
        return jnp.where(same_head, jnp.concatenate([xp.astype(BF16)] * GROUP_HEADS, axis=0), 0.0)

    def head_transpose(xp):
        return jnp.concatenate([xp[:, h * N:(h + 1) * N].T for h in range(GROUP_HEADS)], axis=1)

    order = list(range(n_ck))[::-1] if reverse else list(range(n_ck))
    cgs = [(c, g) for c in order for g in range(N_GROUPS)]

    def tile(x, c, g):
        return x[c * L:(c + 1) * L, g * GROUP:(g + 1) * GROUP]

    ar = {cg: jnp.concatenate([tile(a_t, *cg), tile(r_t, *cg)], axis=0) for cg in cgs}
    arb = {cg: _dot_nt(ar[cg], bdiag(tile(b_t, *cg))) for cg in cgs}
    ark = {cg: _dot_nt(ar[cg], bdiag(tile(k_t, *cg))) for cg in cgs}
    ab = {cg: arb[cg][0:L] for cg in cgs}
    rb = {cg: arb[cg][L:2 * L] for cg in cgs}
    ak = {cg: ark[cg][0:L] for cg in cgs}
    rk_ = {cg: ark[cg][L:2 * L] for cg in cgs}
    x = {cg: eye - jnp.where(level_masks[0], ab[cg], 0.0) for cg in cgs}
    for m in level_masks[1:]:
        xn = {cg: _dot(x[cg].astype(BF16), bdiag(jnp.where(m, ab[cg], 0.0))).astype(BF16) for cg in cgs}
        x = {cg: x[cg] - _dot(xn[cg], bdiag(x[cg])) for cg in cgs}
    xb = {cg: x[cg].astype(BF16) for cg in cgs}
    xa = {cg: _dot(xb[cg], bdiag(tile(a_t, *cg))) for cg in cgs}
    xa_w = {cg: bdiag(xa[cg]) for cg in cgs}
    mrb_n = {cg: jnp.where(incl, -rb[cg], 0.0).astype(BF16) for cg in cgs}
    v_w = {cg: bdiag(tile(v, *cg)) for cg in cgs}
    mv = {cg: _dot(jnp.concatenate([jnp.where(strict, ak[cg], 0.0).astype(BF16),
                                    jnp.where(incl, rk_[cg], 0.0).astype(BF16)], axis=0), v_w[cg]) for cg in cgs}
    xmv = {cg: _dot(xb[cg], bdiag(mv[cg][0:L])) for cg in cgs}
    rq = {cg: (tile(r_t, *cg).astype(F32) + _dot(mrb_n[cg], xa_w[cg])).astype(BF16) for cg in cgs}
    xmv_w = {cg: bdiag(xmv[cg]) for cg in cgs}
    dd = {cg: mv[cg][L:2 * L] + _dot(mrb_n[cg], xmv_w[cg]) for cg in cgs}
    eye_b = eye.astype(BF16)
    xt = {cg: jnp.concatenate([_dot_nt(eye_b, xa_w[cg]), _dot_nt(eye_b, xmv_w[cg])], axis=0).astype(BF16) for cg in cgs}
    wb = {cg: _dot(xt[cg], bdiag(tile(b_hn, *cg))) for cg in cgs}
    cc = {cg: _dot(head_transpose(tile(v, *cg)).astype(BF16), bdiag(tile(k_h, *cg))) + wb[cg][L:2 * L] for cg in cgs}
    w_w = {cg: bdiag(wb[cg][0:L]) for cg in cgs}

    state = [st_ref[g] for g in range(N_GROUPS)]
    y_tiles = {}
    for c in order:
        for g in range(N_GROUPS):
            cg = (c, g)
            sb = state[g].astype(BF16)
            state[g] = state[g] * tile(e_tot, c, g)[0:1] + _dot(sb, w_w[cg]) + cc[cg]
            y_tiles[cg] = _dot_nt(rq[cg], bdiag(sb)) + dd[cg]
    for g in range(N_GROUPS):
        st_ref[g] = state[g]
    wkv = jnp.concatenate(
        [jnp.concatenate([y_tiles[(c, g)] for g in range(N_GROUPS)], axis=1) for c in range(n_ck)], axis=0)

    if not final:
        o_ref[0] = wkv
        return

    wkv = wkv + yb_ref[0]
    inv_n = 1.0 / N
    mean = _seg_sum(wkv, bd) * inv_n
    xc = wkv - mean
    var = _seg_sum(xc * xc, bd) * inv_n
    normed = xc * lax.rsqrt(var + LNX_EPS) * lnw_ref[...] + lnb_ref[...]
    ao_lo = us[:, OFF_AB:OFF_AB + A_LORA]
    a_o = _sigmoid(a0o_ref[...] + _dot(ao_lo.astype(BF16), aupo_ref[...]))
    kd_o = k * (1.0 + (a_o - 1.0) * ka_ref[...])
    rk = rk_ref[...]
    bonus = (_seg_sum(r * kd * rk, bd) + _seg_sum(r * kd_o * rk, bd)) * v
    g_lo = us[:, OFF_G:OFF_G + G_LORA]
    gg = _dot(_sigmoid(g_lo).astype(BF16), gup_ref[...])
    o_ref[0] = ((normed + bonus) * gg).astype(o_ref.dtype)


def _rwkv_pass(u_r, consts, *, reverse, final, extra=()):
    b, t, _ = u_r.shape
    tb = _largest_divisor(t, (ROWS_PER_STEP, CHUNK))
    n_steps = t // tb
    hb = tb // BF16_SUBLANES
    n_halo = t // BF16_SUBLANES

    def sidx(s):
        return (n_steps - 1 - s) if reverse else s

    def full(shape):
        return pl.BlockSpec(shape, lambda bi, s: (0,) * len(shape))

    in_specs = [
        pl.BlockSpec((1, tb, RWKV_COLS), lambda bi, s: (bi, sidx(s), 0)),
        pl.BlockSpec((1, BF16_SUBLANES, RWKV_COLS), lambda bi, s: (bi, jnp.maximum(sidx(s) * hb - 1, 0), 0)),
        pl.BlockSpec((1, BF16_SUBLANES, RWKV_COLS),
                     lambda bi, s: (bi, jnp.minimum((sidx(s) + 1) * hb, n_halo - 1), 0)),
    ]
    args = [u_r, u_r, u_r]
    for arr in consts:
        in_specs.append(full(arr.shape))
        args.append(arr)
    if final:
        for arr in extra[:-1]:
            in_specs.append(full(arr.shape))
            args.append(arr)
        in_specs.append(pl.BlockSpec((1, tb, RWKV_DIM), lambda bi, s: (bi, sidx(s), 0)))
        args.append(extra[-1])
    out_dtype = BF16 if final else F32
    return pl.pallas_call(
        functools.partial(_rwkv_kernel, reverse=reverse, final=final, n_steps=n_steps),
        grid=(b, n_steps),
        in_specs=in_specs,
        out_specs=pl.BlockSpec((1, tb, RWKV_DIM), lambda bi, s: (bi, sidx(s), 0)),
        out_shape=jax.ShapeDtypeStruct((b, t, RWKV_DIM), out_dtype),
        scratch_shapes=[pltpu.VMEM((N_GROUPS, RWKV_HEAD, GROUP), F32)],
        compiler_params=_params("parallel", "arbitrary"),
    )(*args)


def _rwkv_branch(u_r, mu_prev, mu_next, w0_f, w_up_f, w0_b, w_up_b, a0_f, a_up_f, a0_b, a_up_b,
                 g_up, k_k, k_a, r_k, ln_w, ln_b):
    t = u_r.shape[1]
    tb = _largest_divisor(t, (ROWS_PER_STEP, CHUNK))
    row = lambda a: a.reshape(1, -1).astype(F32)
    seg = np.arange(RWKV_DIM) // RWKV_HEAD
    bd = jnp.asarray(seg[:, None] == seg[None, :], BF16)
    ti = np.arange(tb)
    same_chunk = (ti[:, None] // CHUNK) == (ti[None, :] // CHUNK)
    tri_f = jnp.asarray(same_chunk & (ti[None, :] <= ti[:, None]), BF16)
    tri_b = jnp.asarray(same_chunk & (ti[None, :] >= ti[:, None]), BF16)
    ones = jnp.asarray(same_chunk, BF16)
    common = lambda w0, wup, a0, aup, tri: [row(mu_prev), row(mu_next), row(w0), wup.astype(BF16), row(a0),
                                            aup.astype(BF16), row(k_k), row(k_a), bd, tri, ones]
    y_b = _rwkv_pass(u_r, common(w0_b, w_up_b, a0_b, a_up_b, tri_b), reverse=True, final=False)
    extra = [row(a0_b), a_up_b.astype(BF16), g_up.astype(BF16), row(r_k), row(ln_w), row(ln_b), y_b]
    return _rwkv_pass(u_r, common(w0_f, w_up_f, a0_f, a_up_f, tri_f), reverse=False, final=True, extra=extra)


def _attn_kernel(sink_ref, bias_ref, q_ref, kp_ref, kc_ref, kn_ref, vp_ref, vc_ref, vn_ref, o_ref, *, seq_len):
    i = pl.program_id(1)
    span = BLOCK + 2 * WINDOW
    kk = jnp.concatenate([kp_ref[0], kc_ref[0], kn_ref[0]], axis=0)
    vv = jnp.concatenate([vp_ref[0], vc_ref[0], vn_ref[0]], axis=0)
    key_pos = i * BLOCK - WINDOW + lax.broadcasted_iota(jnp.int32, (1, span), 1)
    in_seq = (key_pos >= 0) & (key_pos < seq_len)
    group = ATT_HEADS // ATT_KV_HEADS
    qs = q_ref[0] * (ATT_HEAD ** -0.5)
    heads = range(ATT_HEADS)
    hsl = lambda h: slice(h * ATT_HEAD, (h + 1) * ATT_HEAD)
    s = [_dot_nt(qs[:, hsl(h)], kk[:, hsl(h // group)]) for h in heads]
    s = [jnp.where(in_seq, s[h] + bias_ref[h], NEG_INF) for h in heads]
    m = [jnp.maximum(jnp.max(s[h], axis=-1, keepdims=True), sink_ref[h]) for h in heads]
    p = [jnp.exp(s[h] - m[h]) for h in heads]
    denom = [jnp.sum(p[h], axis=-1, keepdims=True) + jnp.exp(sink_ref[h] - m[h]) for h in heads]
    outs = [_dot(p[h].astype(BF16), vv[:, hsl(h // group)]) / denom[h] for h in heads]
    o_ref[0] = jnp.concatenate(outs, axis=-1).astype(o_ref.dtype)


def _attention(qkv, sink):
    b, t, _ = qkv.shape
    nb = t // BLOCK
    span = BLOCK + 2 * WINDOW
    kcol = ATT_DIM // KV_DIM
    prev = lambda i: jnp.maximum(i - 1, 0)
    nxt = lambda i: jnp.minimum(i + 1, nb - 1)
    kv_spec = lambda col, f: pl.BlockSpec((1, BLOCK, KV_DIM), lambda bi, i: (bi, f(i), col))
    same = lambda i: i
    rel = np.abs(np.arange(span)[None, :] - WINDOW - np.arange(BLOCK)[:, None])
    slopes = 2.0 ** (-8.0 * np.arange(1, ATT_HEADS + 1) / ATT_HEADS)
    bias = np.where(rel[None] <= WINDOW, -slopes[:, None, None] * rel[None], NEG_INF).astype(np.float32)
    return pl.pallas_call(
        functools.partial(_attn_kernel, seq_len=t),
        grid=(b, nb),
        in_specs=[pl.BlockSpec(memory_space=pltpu.SMEM),
                  pl.BlockSpec((ATT_HEADS, BLOCK, span), lambda bi, i: (0, 0, 0)),
                  pl.BlockSpec((1, BLOCK, ATT_DIM), lambda bi, i: (bi, i, 0)),
                  kv_spec(kcol, prev), kv_spec(kcol, same), kv_spec(kcol, nxt),
                  kv_spec(kcol + 1, prev), kv_spec(kcol + 1, same), kv_spec(kcol + 1, nxt)],
        out_specs=pl.BlockSpec((1, BLOCK, ATT_DIM), lambda bi, i: (bi, i, 0)),
        out_shape=jax.ShapeDtypeStruct((b, t, ATT_DIM), BF16),
        compiler_params=_params("parallel", "parallel"),
    )(sink.astype(F32), jnp.asarray(bias), qkv, qkv, qkv, qkv, qkv, qkv, qkv)


def _merge_kernel(yr_ref, ya_ref, gate_ref, x_ref, wpr_ref, wpa_ref, wo_ref, g_ref, wrt_ref,
                  x1_ref, h2_ref, aff_ref, afft_ref):
    d = x_ref.shape[-1]
    gl_r = gate_ref[:, 0:d].astype(F32)
    gl_a = gate_ref[:, d:2 * d].astype(F32)
    merged = (_sigmoid(gl_r) * _dot(yr_ref[...], wpr_ref[...])
              + _sigmoid(gl_a) * _dot(ya_ref[...], wpa_ref[...]))
    x1 = x_ref[...] + _dot(merged.astype(BF16), wo_ref[...])
    x1_ref[...] = x1
    h2 = (x1 * lax.rsqrt(jnp.mean(x1 * x1, axis=-1, keepdims=True) + NORM_EPS) * g_ref[...]).astype(BF16)
    h2_ref[...] = h2
    logits_t = _dot_nt(wrt_ref[...], h2)
    mx = jnp.max(logits_t, axis=0, keepdims=True)
    ex = jnp.exp(logits_t - mx)
    aff_t = ex / jnp.sum(ex, axis=0, keepdims=True)
    afft_ref[0] = aff_t
    aff_ref[...] = aff_t.T


def _merge(y_r, y_a, gates, x2, wpr, wpa, wo, g, w_router, tm, seq_len):
    n, d = x2.shape
    e = w_router.shape[1]
    tiles_per_seq = seq_len // tm
    full = lambda shape: pl.BlockSpec(shape, lambda i: (0,) * len(shape))
    rows = lambda w: pl.BlockSpec((tm, w), lambda i: (i, 0))
    return pl.pallas_call(
        _merge_kernel,
        grid=(n // tm,),
        in_specs=[rows(RWKV_DIM), rows(ATT_DIM), rows(2 * d), rows(d),
                  full(wpr.shape), full(wpa.shape), full(wo.shape), full((1, d)), full((e, d))],
        out_specs=[rows(d), rows(d), rows(e),
                   pl.BlockSpec((1, e, tm), lambda i: (i // tiles_per_seq, 0, i % tiles_per_seq))],
        out_shape=[jax.ShapeDtypeStruct((n, d), F32), jax.ShapeDtypeStruct((n, d), BF16),
                   jax.ShapeDtypeStruct((n, e), F32),
                   jax.ShapeDtypeStruct((n // seq_len, e, seq_len), F32)],
        compiler_params=_params("parallel"),
    )(y_r, y_a, gates, x2, wpr.astype(BF16), wpa.astype(BF16), wo.astype(BF16), g.reshape(1, d).astype(F32),
      w_router.T.astype(BF16))


def _select_kernel(a_ref, gsame_ref, gprev_ref, triu_ref, slot_ref, *, cap):
    nb, rows_b, _ = a_ref.shape
    aff = a_ref[...].reshape(nb * rows_b, LANES)
    gsame = gsame_ref[...]
    gprev = gprev_ref[...]
    triu = triu_ref[...]
    rows = aff.shape[0]
    capf = float(cap)

    def group_count(mask):
        per_lane = _dot(gsame, jnp.where(mask, 1.0, 0.0).astype(BF16))
        return jnp.sum(per_lane, axis=-1, keepdims=True)

    def search(i, prefix):
        cand = prefix | lax.shift_left(jnp.int32(1), 30 - i)
        return jnp.where(group_count(aff >= pltpu.bitcast(cand, F32)) >= capf, cand, prefix)

    thr = pltpu.bitcast(lax.fori_loop(0, 31, search, jnp.zeros((rows, 1), jnp.int32)), F32)

    def prefix_count(mask):
        mb = jnp.where(mask, 1.0, 0.0)
        inc = _dot(mb.astype(BF16), triu)
        tot = jnp.broadcast_to(inc[:, LANES - 1:LANES], inc.shape).astype(BF16)
        return inc - mb + _dot(gprev, tot)

    gt = aff > thr
    eq = aff == thr
    need = capf - group_count(gt)
    sel = gt | (eq & (prefix_count(eq) < need))
    pos = prefix_count(sel)
    slot_ref[...] = jnp.where(sel, pos, -1.0).astype(jnp.int32).reshape(nb, rows_b, LANES)


def _select(aff_t, cap):
    b, e, t = aff_t.shape
    tiles = t // LANES
    rows = e * tiles
    nb = _largest_divisor(b, (4, 2))
    rid = np.arange(nb * rows)
    same = (rid[:, None] // tiles) == (rid[None, :] // tiles)
    gsame = jnp.asarray(same, BF16)
    gprev = jnp.asarray(same & (rid[None, :] < rid[:, None]), BF16)
    li = np.arange(LANES)
    triu = jnp.asarray(li[:, None] <= li[None, :], BF16)
    full = lambda shape: pl.BlockSpec(shape, lambda bi: (0,) * len(shape))
    slot = pl.pallas_call(
        functools.partial(_select_kernel, cap=cap),
        grid=(b // nb,),
        in_specs=[pl.BlockSpec((nb, rows, LANES), lambda bi: (bi, 0, 0)),
                  full((nb * rows, nb * rows)), full((nb * rows, nb * rows)), full((LANES, LANES))],
        out_specs=pl.BlockSpec((nb, rows, LANES), lambda bi: (bi, 0, 0)),
        out_shape=jax.ShapeDtypeStruct((b, rows, LANES), jnp.int32),
        compiler_params=_params("parallel"),
    )(aff_t.reshape(b, rows, LANES), gsame, gprev, triu)
    return slot.reshape(b, e, t)


def _gather_kernel(slot_ref, h_ref, xs_ref):
    cap = xs_ref.shape[2]
    slot = slot_ref[0]
    cidx = lax.broadcasted_iota(jnp.int32, (cap, 1), 0)
    onehot = jnp.where(cidx == slot, 1.0, 0.0).astype(BF16)
    xs_ref[0, 0] = _dot(onehot, h_ref[0]).astype(xs_ref.dtype)


def _gather(slot_rows, h2, cap):
    b, t, d = h2.shape
    e = slot_rows.shape[0] // b
    return pl.pallas_call(
        _gather_kernel,
        grid=(b, e),
        in_specs=[pl.BlockSpec((1, 1, t), lambda bi, ei: (bi * e + ei, 0, 0)),
                  pl.BlockSpec((1, t, d), lambda bi, ei: (bi, 0, 0))],
        out_specs=pl.BlockSpec((1, 1, cap, d), lambda bi, ei: (bi, ei, 0, 0)),
        out_shape=jax.ShapeDtypeStruct((b, e, cap, d), BF16),
        compiler_params=_params("parallel", "arbitrary"),
    )(slot_rows, h2)


def _expert_kernel(xs_ref, wg_ref, wu_ref, wd_ref, ys_ref):
    bb, _, cap, d = xs_ref.shape
    xs = xs_ref[...].reshape(bb * cap, d)
    gate = _dot(xs, wg_ref[0])
    up = _dot(xs, wu_ref[0])
    hid = (gate * _sigmoid(gate) * up).astype(BF16)
    ys_ref[...] = _dot(hid, wd_ref[0]).astype(ys_ref.dtype).reshape(bb, 1, cap, d)


def _experts(xs, wg, wu, wd, bb):
    b, e, cap, d = xs.shape
    ff = wg.shape[-1]
    tok = pl.BlockSpec((bb, 1, cap, d), lambda ei, bi: (bi, ei, 0, 0))
    return pl.pallas_call(
        _expert_kernel,
        grid=(e, b // bb),
        in_specs=[tok,
                  pl.BlockSpec((1, d, ff), lambda ei, bi: (ei, 0, 0)),
                  pl.BlockSpec((1, d, ff), lambda ei, bi: (ei, 0, 0)),
                  pl.BlockSpec((1, ff, d), lambda ei, bi: (ei, 0, 0))],
        out_specs=tok,
        out_shape=jax.ShapeDtypeStruct((b, e, cap, d), BF16),
        compiler_params=_params("parallel", "arbitrary"),
    )(xs, wg, wu, wd)


def _scatter_kernel(slot_ref, aff_ref, ys_ref, x1_ref, g_ref, o_ref, *, final_norm):
    n_exp, cap = ys_ref.shape[1], ys_ref.shape[2]
    slot = slot_ref[0]
    aff = aff_ref[0]
    cidx = lax.broadcasted_iota(jnp.int32, (1, cap), 1)
    acc = x1_ref[0]
    for e in range(n_exp):
        onehot = jnp.where(slot[:, e:e + 1] == cidx, 1.0, 0.0).astype(BF16)
        acc = acc + aff[:, e:e + 1] * _dot(onehot, ys_ref[0, e])
    if final_norm:
        acc = acc * lax.rsqrt(jnp.mean(acc * acc, axis=-1, keepdims=True) + NORM_EPS) * g_ref[...]
    o_ref[0] = acc


def _scatter(slot_cols, aff, ys, x1, g, tt, final_norm):
    b, t, d = x1.shape
    e, cap = ys.shape[1], ys.shape[2]
    return pl.pallas_call(
        functools.partial(_scatter_kernel, final_norm=final_norm),
        grid=(b, t // tt),
        in_specs=[pl.BlockSpec((1, tt, e), lambda bi, ti: (bi, ti, 0)),
                  pl.BlockSpec((1, tt, e), lambda bi, ti: (bi, ti, 0)),
                  pl.BlockSpec((1, e, cap, d), lambda bi, ti: (bi, 0, 0, 0)),
                  pl.BlockSpec((1, tt, d), lambda bi, ti: (bi, ti, 0)),
                  pl.BlockSpec((1, d), lambda bi, ti: (0, 0))],
        out_specs=pl.BlockSpec((1, tt, d), lambda bi, ti: (bi, ti, 0)),
        out_shape=jax.ShapeDtypeStruct((b, t, d), F32),
        compiler_params=_params("parallel", "arbitrary"),
    )(slot_cols, aff, ys, x1, g.reshape(1, d).astype(F32))


def kernel(x, norm_mix_g, w_in, mu_prev, mu_next, w0_f, w_up_f, w0_b, w_up_b, a0_f, a_up_f, a0_b, a_up_b, g_up, k_k, k_a, r_k, ln_x_w, ln_x_b, attn_sink, w_proj_rwkv, w_proj_attn, w_out, norm_ffn_g, w_router, exp_w_gate, exp_w_up, exp_w_down, norm_final_g):
    b, t, d = x.shape
    depth = w_in.shape[0]
    n = b * t
    tm = _largest_divisor(t, (512, 256, 128))
    cap = CAPACITY_FACTOR * t // N_EXPERTS
    bb = _largest_divisor(b, (4, 2))
    xc = x
    for l in range(depth):
        x2 = xc.reshape(n, d)
        u_r, qkv, gates = _in_projection(x2, norm_mix_g[l].astype(F32), w_in[l].astype(BF16), tm)
        y_r = _rwkv_branch(u_r.reshape(b, t, RWKV_COLS), mu_prev[l], mu_next[l], w0_f[l], w_up_f[l], w0_b[l],
                           w_up_b[l], a0_f[l], a_up_f[l], a0_b[l], a_up_b[l], g_up[l], k_k[l], k_a[l], r_k[l],
                           ln_x_w[l], ln_x_b[l])
        y_a = _attention(qkv.reshape(b, t, QKV_COLS), attn_sink[l])
        last = l == depth - 1
        x1, h2, aff, aff_t = _merge(y_r.reshape(n, RWKV_DIM), y_a.reshape(n, ATT_DIM), gates, x2, w_proj_rwkv[l],
                                    w_proj_attn[l], w_out[l], norm_ffn_g[l], w_router[l], tm, t)
        slot = _select(aff_t, cap)
        xs = _gather(slot.reshape(b * N_EXPERTS, 1, t), h2.reshape(b, t, d), cap)
        ys = _experts(xs, exp_w_gate[l].astype(BF16), exp_w_up[l].astype(BF16), exp_w_down[l].astype(BF16), bb)
        xc = _scatter(jnp.swapaxes(slot, 1, 2), aff.reshape(b, t, N_EXPERTS), ys, x1.reshape(b, t, d),
                      norm_final_g, tm, final_norm=last)
    return xc
```

```python
import functools
import math

import numpy as np
import jax
import jax.numpy as jnp
from jax import lax
from jax.experimental import pallas as pl
from jax.experimental.pallas import tpu as pltpu

F32 = jnp.float32
BF16 = jnp.bfloat16

RWKV_HEAD = 64
RWKV_HEADS = 8
RWKV_DIM = RWKV_HEADS * RWKV_HEAD
W_LORA = 64
A_LORA = 64
G_LORA = 128
DECAY_SCALE = math.exp(-0.5)
LNX_EPS = 64e-5
ATT_HEADS = 8
ATT_KV_HEADS = 2
ATT_HEAD = 64
ATT_DIM = ATT_HEADS * ATT_HEAD
KV_DIM = ATT_KV_HEADS * ATT_HEAD
WINDOW = 128
BLOCK = 128
NEG_INF = -1e30
N_EXPERTS = 16
CAPACITY_FACTOR = 2
NORM_EPS = 1e-6
RWKV_COLS = 3 * RWKV_DIM + G_LORA + 2 * W_LORA + 2 * A_LORA
QKV_COLS = ATT_DIM + 2 * KV_DIM
OFF_R, OFF_K, OFF_V = 0, RWKV_DIM, 2 * RWKV_DIM
OFF_G = 3 * RWKV_DIM
OFF_WF = OFF_G + G_LORA
OFF_WB = OFF_WF + W_LORA
OFF_AF = OFF_WB + W_LORA
OFF_AB = OFF_AF + A_LORA

LANES = 128
VMEM_LIMIT = 48 * 1024 * 1024

CHUNK = 64


def _dot(a, b):
    return jnp.dot(a, b, preferred_element_type=F32)


def _dot_nt(a, b):
    return lax.dot_general(a, b, (((1,), (1,)), ((), ())), preferred_element_type=F32)


def _split2(x):
    hi = x.astype(BF16)
    lo = (x - hi.astype(F32)).astype(BF16)
    return hi, lo


def _split3(x):
    hi = x.astype(BF16)
    r1 = x - hi.astype(F32)
    mid = r1.astype(BF16)
    lo = (r1 - mid.astype(F32)).astype(BF16)
    return hi, mid, lo


def _seg_sum(x, bd):
    width = bd.shape[0]
    outs = []
    for j in range(0, x.shape[1], width):
        hi, lo = _split2(x[:, j:j + width])
        outs.append(_dot(hi, bd) + _dot(lo, bd))
    return jnp.concatenate(outs, axis=1)


def _sigmoid(x):
    return 1.0 / (1.0 + jnp.exp(-x))


def _largest_divisor(n, candidates):
    for c in candidates:
        if n % c == 0:
            return c
    return 1


def _params(*sem):
    return pltpu.CompilerParams(dimension_semantics=sem, vmem_limit_bytes=VMEM_LIMIT)


def _inproj_kernel(x_ref, xp_ref, xn_ref, g_ref, w_ref, mup_ref, mun_ref, ur_ref, qkv_ref, gate_ref, *,
                   col_chunk, tiles_per_seq):
    i = pl.program_id(0)
    tm = x_ref.shape[0]
    halo = xp_ref.shape[0]
    x = jnp.concatenate([xp_ref[...], x_ref[...], xn_ref[...]], axis=0)
    h = x * lax.rsqrt(jnp.mean(x * x, axis=-1, keepdims=True) + NORM_EPS) * g_ref[...]
    row = lax.broadcasted_iota(jnp.int32, (tm + 2 * halo, 1), 0)
    pos = i % tiles_per_seq
    lo = jnp.where(pos == 0, halo, 0)
    hi = jnp.where(pos == tiles_per_seq - 1, tm + halo, tm + 2 * halo)
    h = jnp.where((row < lo) | (row >= hi), 0.0, h)
    hb_all = h.astype(BF16)
    hb = h[halo:halo + tm].astype(BF16)
    for j in range(0, RWKV_COLS, col_chunk):
        w = min(col_chunk, RWKV_COLS - j)
        u = _dot(hb_all, w_ref[:, j:j + w])
        prev = pltpu.roll(u, 1, axis=0)
        nxt = pltpu.roll(u, tm + 2 * halo - 1, axis=0)
        us = u + mup_ref[:, j:j + w] * (prev - u) + mun_ref[:, j:j + w] * (nxt - u)
        ur_ref[:, j:j + w] = us[halo:halo + tm].astype(BF16)
    c0 = RWKV_COLS
    for ref in (qkv_ref, gate_ref):
        width = ref.shape[-1]
        for j in range(0, width, col_chunk):
            w = min(col_chunk, width - j)
            ref[:, j:j + w] = _dot(hb, w_ref[:, c0 + j:c0 + j + w]).astype(BF16)
        c0 += width


def _in_projection(x2, g, w_in_bf, mu_prev, mu_next, tm, seq_len):
    n, d = x2.shape
    cols = w_in_bf.shape[1]
    gate_cols = cols - RWKV_COLS - QKV_COLS
    halo = 8
    hb = tm // halo
    n_halo = n // halo
    return pl.pallas_call(
        functools.partial(_inproj_kernel, col_chunk=512, tiles_per_seq=seq_len // tm),
        grid=(n // tm,),
        in_specs=[pl.BlockSpec((tm, d), lambda i: (i, 0)),
                  pl.BlockSpec((halo, d), lambda i: (jnp.maximum(i * hb - 1, 0), 0)),
                  pl.BlockSpec((halo, d), lambda i: (jnp.minimum((i + 1) * hb, n_halo - 1), 0)),
                  pl.BlockSpec((1, d), lambda i: (0, 0)),
                  pl.BlockSpec((d, cols), lambda i: (0, 0)),
                  pl.BlockSpec((1, RWKV_COLS), lambda i: (0, 0)),
                  pl.BlockSpec((1, RWKV_COLS), lambda i: (0, 0))],
        out_specs=[pl.BlockSpec((tm, RWKV_COLS), lambda i: (i, 0)),
                   pl.BlockSpec((tm, QKV_COLS), lambda i: (i, 0)),
                   pl.BlockSpec((tm, gate_cols), lambda i: (i, 0))],
        out_shape=[jax.ShapeDtypeStruct((n, RWKV_COLS), BF16),
                   jax.ShapeDtypeStruct((n, QKV_COLS), BF16),
                   jax.ShapeDtypeStruct((n, gate_cols), BF16)],
        compiler_params=_params("parallel"),
    )(x2, x2, x2, g.reshape(1, d), w_in_bf, mu_prev.reshape(1, -1).astype(F32), mu_next.reshape(1, -1).astype(F32))


GROUP_HEADS = 4
GROUP = GROUP_HEADS * RWKV_HEAD
N_GROUPS = RWKV_HEADS // GROUP_HEADS
ROWS_PER_STEP = 256


def _rwkv_kernel(*refs, reverse, final):
    if final:
        (u_ref, w0_ref, wup_ref, a0_ref, aup_ref, kk_ref, ka_ref,
         bd_ref, tri_ref, ones_ref, a0o_ref, aupo_ref, gup_ref, rk_ref, lnw_ref, lnb_ref, yb_ref,
         o_ref, st_ref) = refs
    else:
        (u_ref, w0_ref, wup_ref, a0_ref, aup_ref, kk_ref, ka_ref,
         bd_ref, tri_ref, ones_ref, o_ref, st_ref) = refs
    TB = u_ref.shape[1]
    L = CHUNK
    N = RWKV_HEAD
    n_ck = TB // L
    step = pl.program_id(1)

    @pl.when(step == 0)
    def _():
        st_ref[...] = jnp.zeros_like(st_ref)

    col = lambda off, width: u_ref[0, :, off:off + width].astype(F32)
    r = col(OFF_R, RWKV_DIM)
    k = col(OFF_K, RWKV_DIM)
    v = col(OFF_V, RWKV_DIM)
    w_lo = col(OFF_WB if reverse else OFF_WF, W_LORA)
    a_lo = col(OFF_AB if reverse else OFF_AF, A_LORA)
    bd = bd_ref[...]

    lw = -DECAY_SCALE * _sigmoid(w0_ref[...] + _dot(jnp.tanh(w_lo).astype(BF16), wup_ref[...]))
    a = _sigmoid(a0_ref[...] + _dot(a_lo.astype(BF16), aup_ref[...]))
    kkr = k * kk_ref[...]
    kk = kkr / jnp.maximum(jnp.sqrt(_seg_sum(kkr * kkr, bd)), 1e-12)
    kd = k * (1.0 + (a - 1.0) * ka_ref[...])
    b = a * kk

    tri = tri_ref[...]
    ones = ones_ref[...]
    l_hi, l_mid, l_lo = _split3(lw)
    cs = _dot(tri, l_hi) + _dot(tri, l_mid) + _dot(tri, l_lo)
    tot = _dot(ones, l_hi) + _dot(ones, l_mid) + _dot(ones, l_lo)
    e_tot = jnp.exp(tot)
    a_t = (kk * jnp.exp(cs - lw)).astype(BF16)
    r_t = (r * jnp.exp(cs)).astype(BF16)
    e_neg = jnp.exp(-cs)
    k_t = (kd * e_neg).astype(BF16)
    b_t = (b * e_neg).astype(BF16)
    e_rem = jnp.exp(tot - cs)
    k_h = (kd * e_rem).astype(BF16)
    b_hn = (-b * e_rem).astype(BF16)

    ii = lax.broadcasted_iota(jnp.int32, (GROUP, GROUP), 0)
    jj = lax.broadcasted_iota(jnp.int32, (GROUP, GROUP), 1)
    same_head = (ii // N) == (jj // N)
    ti = lax.broadcasted_iota(jnp.int32, (L, GROUP), 0)
    si = lax.broadcasted_iota(jnp.int32, (L, GROUP), 1) % N
    if reverse:
        strict = si > ti
        incl = si >= ti
    else:
        strict = si < ti
        incl = si <= ti
    eye = (si == ti).astype(F32)
    level_masks = []
    s = 1
    while s < L:
        same = (ti // (2 * s)) == (si // (2 * s))
        t_hi = (ti & s) != 0
        s_hi = (si & s) != 0
        if reverse:
            level_masks.append(same & jnp.logical_not(t_hi) & s_hi)
        else:
            level_masks.append(same & t_hi & jnp.logical_not(s_hi))
        s *= 2

    def bdiag(xp):
        return jnp.where(same_head, jnp.concatenate([xp.astype(BF16)] * GROUP_HEADS, axis=0), 0.0)

    def head_transpose(xp):
        return jnp.concatenate([xp[:, h * N:(h + 1) * N].T for h in range(GROUP_HEADS)], axis=1)

    order = list(range(n_ck))[::-1] if reverse else list(range(n_ck))
    cgs = [(c, g) for c in order for g in range(N_GROUPS)]

    def tile(x, c, g):
        return x[c * L:(c + 1) * L, g * GROUP:(g + 1) * GROUP]

    ar = {cg: jnp.concatenate([tile(a_t, *cg), tile(r_t, *cg)], axis=0) for cg in cgs}
    arb = {cg: _dot_nt(ar[cg], bdiag(tile(b_t, *cg))) for cg in cgs}
    ark = {cg: _dot_nt(ar[cg], bdiag(tile(k_t, *cg))) for cg in cgs}
    ab = {cg: arb[cg][0:L] for cg in cgs}
    rb = {cg: arb[cg][L:2 * L] for cg in cgs}
    ak = {cg: ark[cg][0:L] for cg in cgs}
    rk_ = {cg: ark[cg][L:2 * L] for cg in cgs}
    x = {cg: eye - jnp.where(level_masks[0], ab[cg], 0.0) for cg in cgs}
    for m in level_masks[1:]:
        xn = {cg: _dot(x[cg].astype(BF16), bdiag(jnp.where(m, ab[cg], 0.0))).astype(BF16) for cg in cgs}
        x = {cg: x[cg] - _dot(xn[cg], bdiag(x[cg])) for cg in cgs}
    xb = {cg: x[cg].astype(BF16) for cg in cgs}
    xa = {cg: _dot(xb[cg], bdiag(tile(a_t, *cg))) for cg in cgs}
    xa_w = {cg: bdiag(xa[cg]) for cg in cgs}
    mrb_n = {cg: jnp.where(incl, -rb[cg], 0.0).astype(BF16) for cg in cgs}
    v_w = {cg: bdiag(tile(v, *cg)) for cg in cgs}
    mv = {cg: _dot(jnp.concatenate([jnp.where(strict, ak[cg], 0.0).astype(BF16),
                                    jnp.where(incl, rk_[cg], 0.0).astype(BF16)], axis=0), v_w[cg]) for cg in cgs}
    xmv = {cg: _dot(xb[cg], bdiag(mv[cg][0:L])) for cg in cgs}
    rq = {cg: (tile(r_t, *cg).astype(F32) + _dot(mrb_n[cg], xa_w[cg])).astype(BF16) for cg in cgs}
    xmv_w = {cg: bdiag(xmv[cg]) for cg in cgs}
    dd = {cg: mv[cg][L:2 * L] + _dot(mrb_n[cg], xmv_w[cg]) for cg in cgs}
    eye_b = eye.astype(BF16)
    xt = {cg: jnp.concatenate([_dot_nt(eye_b, xa_w[cg]), _dot_nt(eye_b, xmv_w[cg])], axis=0).astype(BF16) for cg in cgs}
    wb = {cg: _dot(xt[cg], bdiag(tile(b_hn, *cg))) for cg in cgs}
    cc = {cg: _dot(head_transpose(tile(v, *cg)).astype(BF16), bdiag(tile(k_h, *cg))) + wb[cg][L:2 * L] for cg in cgs}
    w_w = {cg: bdiag(wb[cg][0:L]) for cg in cgs}

    state = [st_ref[g] for g in range(N_GROUPS)]
    y_tiles = {}
    for c in order:
        for g in range(N_GROUPS):
            cg = (c, g)
            sb = state[g].astype(BF16)
            state[g] = state[g] * tile(e_tot, c, g)[0:1] + _dot(sb, w_w[cg]) + cc[cg]
            y_tiles[cg] = _dot_nt(rq[cg], bdiag(sb)) + dd[cg]
    for g in range(N_GROUPS):
        st_ref[g] = state[g]
    wkv = jnp.concatenate(
        [jnp.concatenate([y_tiles[(c, g)] for g in range(N_GROUPS)], axis=1) for c in range(n_ck)], axis=0)

    if not final:
        o_ref[0] = wkv
        return

    wkv = wkv + yb_ref[0]
    inv_n = 1.0 / N
    mean = _seg_sum(wkv, bd) * inv_n
    xc = wkv - mean
    var = _seg_sum(xc * xc, bd) * inv_n
    normed = xc * lax.rsqrt(var + LNX_EPS) * lnw_ref[...] + lnb_ref[...]
    ao_lo = col(OFF_AB, A_LORA)
    a_o = _sigmoid(a0o_ref[...] + _dot(ao_lo.astype(BF16), aupo_ref[...]))
    kd_o = k * (1.0 + (a_o - 1.0) * ka_ref[...])
    rk = rk_ref[...]
    bonus = _seg_sum(r * rk * (kd + kd_o), bd) * v
    g_lo = col(OFF_G, G_LORA)
    gg = _dot(_sigmoid(g_lo).astype(BF16), gup_ref[...])
    o_ref[0] = ((normed + bonus) * gg).astype(o_ref.dtype)


def _rwkv_pass(u_r, consts, *, reverse, final, extra=()):
    b, t, _ = u_r.shape
    tb = _largest_divisor(t, (ROWS_PER_STEP, CHUNK))
    n_steps = t // tb

    def sidx(s):
        return (n_steps - 1 - s) if reverse else s

    def full(shape):
        return pl.BlockSpec(shape, lambda bi, s: (0,) * len(shape))

    in_specs = [pl.BlockSpec((1, tb, RWKV_COLS), lambda bi, s: (bi, sidx(s), 0))]
    args = [u_r]
    for arr in consts:
        in_specs.append(full(arr.shape))
        args.append(arr)
    if final:
        for arr in extra[:-1]:
            in_specs.append(full(arr.shape))
            args.append(arr)
        in_specs.append(pl.BlockSpec((1, tb, RWKV_DIM), lambda bi, s: (bi, sidx(s), 0)))
        args.append(extra[-1])
    out_dtype = BF16 if final else F32
    return pl.pallas_call(
        functools.partial(_rwkv_kernel, reverse=reverse, final=final),
        grid=(b, n_steps),
        in_specs=in_specs,
        out_specs=pl.BlockSpec((1, tb, RWKV_DIM), lambda bi, s: (bi, sidx(s), 0)),
        out_shape=jax.ShapeDtypeStruct((b, t, RWKV_DIM), out_dtype),
        scratch_shapes=[pltpu.VMEM((N_GROUPS, RWKV_HEAD, GROUP), F32)],
        compiler_params=_params("parallel", "arbitrary"),
    )(*args)


def _rwkv_branch(u_r, w0_f, w_up_f, w0_b, w_up_b, a0_f, a_up_f, a0_b, a_up_b,
                 g_up, k_k, k_a, r_k, ln_w, ln_b):
    t = u_r.shape[1]
    tb = _largest_divisor(t, (ROWS_PER_STEP, CHUNK))
    row = lambda a: a.reshape(1, -1).astype(F32)
    seg = np.arange(GROUP) // RWKV_HEAD
    bd = jnp.asarray(seg[:, None] == seg[None, :], BF16)
    ti = np.arange(tb)
    same_chunk = (ti[:, None] // CHUNK) == (ti[None, :] // CHUNK)
    tri_f = jnp.asarray(same_chunk & (ti[None, :] <= ti[:, None]), BF16)
    tri_b = jnp.asarray(same_chunk & (ti[None, :] >= ti[:, None]), BF16)
    ones = jnp.asarray(same_chunk, BF16)
    common = lambda w0, wup, a0, aup, tri: [row(w0), wup.astype(BF16), row(a0), aup.astype(BF16), row(k_k), row(k_a),
                                            bd, tri, ones]
    y_b = _rwkv_pass(u_r, common(w0_b, w_up_b, a0_b, a_up_b, tri_b), reverse=True, final=False)
    extra = [row(a0_b), a_up_b.astype(BF16), g_up.astype(BF16), row(r_k), row(ln_w), row(ln_b), y_b]
    return _rwkv_pass(u_r, common(w0_f, w_up_f, a0_f, a_up_f, tri_f), reverse=False, final=True, extra=extra)


def _attn_kernel(sink_ref, bias_ref, q_ref, kp_ref, kc_ref, kn_ref, vp_ref, vc_ref, vn_ref, o_ref, *, seq_len):
    i = pl.program_id(1)
    span = BLOCK + 2 * WINDOW
    kk = jnp.concatenate([kp_ref[0], kc_ref[0], kn_ref[0]], axis=0)
    vv = jnp.concatenate([vp_ref[0], vc_ref[0], vn_ref[0]], axis=0)
    key_pos = i * BLOCK - WINDOW + lax.broadcasted_iota(jnp.int32, (1, span), 1)
    in_seq = (key_pos >= 0) & (key_pos < seq_len)
    group = ATT_HEADS // ATT_KV_HEADS
    qs = q_ref[0] * (ATT_HEAD ** -0.5)
    heads = range(ATT_HEADS)
    hsl = lambda h: slice(h * ATT_HEAD, (h + 1) * ATT_HEAD)
    s = [_dot_nt(qs[:, hsl(h)], kk[:, hsl(h // group)]) for h in heads]
    s = [jnp.where(in_seq, s[h] + bias_ref[h], NEG_INF) for h in heads]
    m = [jnp.maximum(jnp.max(s[h], axis=-1, keepdims=True), sink_ref[h]) for h in heads]
    p = [jnp.exp(s[h] - m[h]) for h in heads]
    denom = [jnp.sum(p[h], axis=-1, keepdims=True) + jnp.exp(sink_ref[h] - m[h]) for h in heads]
    outs = [_dot(p[h].astype(BF16), vv[:, hsl(h // group)]) / denom[h] for h in heads]
    o_ref[0] = jnp.concatenate(outs, axis=-1).astype(o_ref.dtype)


def _attention(qkv, sink):
    b, t, _ = qkv.shape
    nb = t // BLOCK
    span = BLOCK + 2 * WINDOW
    kcol = ATT_DIM // KV_DIM
    prev = lambda i: jnp.maximum(i - 1, 0)
    nxt = lambda i: jnp.minimum(i + 1, nb - 1)
    kv_spec = lambda col, f: pl.BlockSpec((1, BLOCK, KV_DIM), lambda bi, i: (bi, f(i), col))
    same = lambda i: i
    rel = np.abs(np.arange(span)[None, :] - WINDOW - np.arange(BLOCK)[:, None])
    slopes = 2.0 ** (-8.0 * np.arange(1, ATT_HEADS + 1) / ATT_HEADS)
    bias = np.where(rel[None] <= WINDOW, -slopes[:, None, None] * rel[None], NEG_INF).astype(np.float32)
    return pl.pallas_call(
        functools.partial(_attn_kernel, seq_len=t),
        grid=(b, nb),
        in_specs=[pl.BlockSpec(memory_space=pltpu.SMEM),
                  pl.BlockSpec((ATT_HEADS, BLOCK, span), lambda bi, i: (0, 0, 0)),
                  pl.BlockSpec((1, BLOCK, ATT_DIM), lambda bi, i: (bi, i, 0)),
                  kv_spec(kcol, prev), kv_spec(kcol, same), kv_spec(kcol, nxt),
                  kv_spec(kcol + 1, prev), kv_spec(kcol + 1, same), kv_spec(kcol + 1, nxt)],
        out_specs=pl.BlockSpec((1, BLOCK, ATT_DIM), lambda bi, i: (bi, i, 0)),
        out_shape=jax.ShapeDtypeStruct((b, t, ATT_DIM), BF16),
        compiler_params=_params("parallel", "parallel"),
    )(sink.astype(F32), jnp.asarray(bias), qkv, qkv, qkv, qkv, qkv, qkv, qkv)


def _merge_kernel(yr_ref, ya_ref, gate_ref, x_ref, wpr_ref, wpa_ref, wo_ref, g_ref, wrt_ref,
                  x1_ref, h2_ref, aff_ref, afft_ref):
    d = x_ref.shape[-1]
    gl_r = gate_ref[:, 0:d].astype(F32)
    gl_a = gate_ref[:, d:2 * d].astype(F32)
    merged = (_sigmoid(gl_r) * _dot(yr_ref[...], wpr_ref[...])
              + _sigmoid(gl_a) * _dot(ya_ref[...], wpa_ref[...]))
    x1 = x_ref[...] + _dot(merged.astype(BF16), wo_ref[...])
    x1_ref[...] = x1
    h2 = (x1 * lax.rsqrt(jnp.mean(x1 * x1, axis=-1, keepdims=True) + NORM_EPS) * g_ref[...]).astype(BF16)
    h2_ref[...] = h2
    logits_t = _dot_nt(wrt_ref[...], h2)
    mx = jnp.max(logits_t, axis=0, keepdims=True)
    ex = jnp.exp(logits_t - mx)
    aff_t = ex / jnp.sum(ex, axis=0, keepdims=True)
    afft_ref[0] = aff_t
    aff_ref[...] = aff_t.T


def _merge(y_r, y_a, gates, x2, wpr, wpa, wo, g, w_router, tm, seq_len):
    n, d = x2.shape
    e = w_router.shape[1]
    tiles_per_seq = seq_len // tm
    full = lambda shape: pl.BlockSpec(shape, lambda i: (0,) * len(shape))
    rows = lambda w: pl.BlockSpec((tm, w), lambda i: (i, 0))
    return pl.pallas_call(
        _merge_kernel,
        grid=(n // tm,),
        in_specs=[rows(RWKV_DIM), rows(ATT_DIM), rows(2 * d), rows(d),
                  full(wpr.shape), full(wpa.shape), full(wo.shape), full((1, d)), full((e, d))],
        out_specs=[rows(d), rows(d), rows(e),
                   pl.BlockSpec((1, e, tm), lambda i: (i // tiles_per_seq, 0, i % tiles_per_seq))],
        out_shape=[jax.ShapeDtypeStruct((n, d), F32), jax.ShapeDtypeStruct((n, d), BF16),
                   jax.ShapeDtypeStruct((n, e), F32),
                   jax.ShapeDtypeStruct((n // seq_len, e, seq_len), F32)],
        compiler_params=_params("parallel"),
    )(y_r, y_a, gates, x2, wpr.astype(BF16), wpa.astype(BF16), wo.astype(BF16), g.reshape(1, d).astype(F32),
      w_router.T.astype(BF16))


def _select_kernel(a_ref, gsame_ref, gprev_ref, triu_ref, slot_ref, *, cap):
    nb, rows, _ = a_ref.shape
    affs = [a_ref[i] for i in range(nb)]
    gsame = gsame_ref[...]
    gprev = gprev_ref[...]
    triu = triu_ref[...]
    capf = float(cap)

    def group_count(mask):
        per_lane = _dot(gsame, jnp.where(mask, 1.0, 0.0).astype(BF16))
        return jnp.sum(per_lane, axis=-1, keepdims=True)

    def search(i, prefixes):
        out = []
        for aff, prefix in zip(affs, prefixes):
            cand = prefix | lax.shift_left(jnp.int32(1), 30 - i)
            out.append(jnp.where(group_count(aff >= pltpu.bitcast(cand, F32)) >= capf, cand, prefix))
        return tuple(out)

    thr_bits = lax.fori_loop(0, 31, search, tuple(jnp.zeros((rows, 1), jnp.int32) for _ in range(nb)))

    def prefix_count(mask):
        mb = jnp.where(mask, 1.0, 0.0)
        inc = _dot(mb.astype(BF16), triu)
        tot = jnp.broadcast_to(inc[:, LANES - 1:LANES], inc.shape).astype(BF16)
        return inc - mb + _dot(gprev, tot)

    for i, aff in enumerate(affs):
        thr = pltpu.bitcast(thr_bits[i], F32)
        gt = aff > thr
        eq = aff == thr
        need = capf - group_count(gt)
        sel = gt | (eq & (prefix_count(eq) < need))
        slot_ref[i] = jnp.where(sel, prefix_count(sel), -1.0).astype(jnp.int32)


def _select(aff_t, cap):
    b, e, t = aff_t.shape
    tiles = t // LANES
    rows = e * tiles
    nb = _largest_divisor(b, (4, 2))
    rid = np.arange(rows)
    same = (rid[:, None] // tiles) == (rid[None, :] // tiles)
    gsame = jnp.asarray(same, BF16)
    gprev = jnp.asarray(same & (rid[None, :] < rid[:, None]), BF16)
    li = np.arange(LANES)
    triu = jnp.asarray(li[:, None] <= li[None, :], BF16)
    full = lambda shape: pl.BlockSpec(shape, lambda bi: (0,) * len(shape))
    slot = pl.pallas_call(
        functools.partial(_select_kernel, cap=cap),
        grid=(b // nb,),
        in_specs=[pl.BlockSpec((nb, rows, LANES), lambda bi: (bi, 0, 0)),
                  full((rows, rows)), full((rows, rows)), full((LANES, LANES))],
        out_specs=pl.BlockSpec((nb, rows, LANES), lambda bi: (bi, 0, 0)),
        out_shape=jax.ShapeDtypeStruct((b, rows, LANES), jnp.int32),
        compiler_params=_params("parallel"),
    )(aff_t.reshape(b, rows, LANES), gsame, gprev, triu)
    return slot.reshape(b, e, t)


def _gather_kernel(slot_ref, h_ref, xs_ref):
    cap = xs_ref.shape[2]
    slot = slot_ref[0]
    cidx = lax.broadcasted_iota(jnp.int32, (cap, 1), 0)
    onehot = jnp.where(cidx == slot, 1.0, 0.0).astype(BF16)
    xs_ref[0, 0] = _dot(onehot, h_ref[0]).astype(xs_ref.dtype)


def _gather(slot_rows, h2, cap):
    b, t, d = h2.shape
    e = slot_rows.shape[0] // b
    return pl.pallas_call(
        _gather_kernel,
        grid=(b, e),
        in_specs=[pl.BlockSpec((1, 1, t), lambda bi, ei: (bi * e + ei, 0, 0)),
                  pl.BlockSpec((1, t, d), lambda bi, ei: (bi, 0, 0))],
        out_specs=pl.BlockSpec((1, 1, cap, d), lambda bi, ei: (bi, ei, 0, 0)),
        out_shape=jax.ShapeDtypeStruct((b, e, cap, d), BF16),
        compiler_params=_params("parallel", "arbitrary"),
    )(slot_rows, h2)


def _expert_kernel(xs_ref, wg_ref, wu_ref, wd_ref, ys_ref):
    bb, _, cap, d = xs_ref.shape
    xs = xs_ref[...].reshape(bb * cap, d)
    gate = _dot(xs, wg_ref[0])
    up = _dot(xs, wu_ref[0])
    hid = (gate * _sigmoid(gate) * up).astype(BF16)
    ys_ref[...] = _dot(hid, wd_ref[0]).astype(ys_ref.dtype).reshape(bb, 1, cap, d)


def _experts(xs, wg, wu, wd, bb):
    b, e, cap, d = xs.shape
    ff = wg.shape[-1]
    tok = pl.BlockSpec((bb, 1, cap, d), lambda ei, bi: (bi, ei, 0, 0))
    return pl.pallas_call(
        _expert_kernel,
        grid=(e, b // bb),
        in_specs=[tok,
                  pl.BlockSpec((1, d, ff), lambda ei, bi: (ei, 0, 0)),
                  pl.BlockSpec((1, d, ff), lambda ei, bi: (ei, 0, 0)),
                  pl.BlockSpec((1, ff, d), lambda ei, bi: (ei, 0, 0))],
        out_specs=tok,
        out_shape=jax.ShapeDtypeStruct((b, e, cap, d), BF16),
        compiler_params=_params("parallel", "arbitrary"),
    )(xs, wg, wu, wd)


def _scatter_kernel(slot_ref, aff_ref, ys_ref, x1_ref, g_ref, o_ref, *, final_norm):
    n_exp, cap = ys_ref.shape[1], ys_ref.shape[2]
    slot = slot_ref[0]
    aff = aff_ref[0]
    cidx = lax.broadcasted_iota(jnp.int32, (1, cap), 1)
    acc = x1_ref[0]
    for e in range(n_exp):
        onehot = jnp.where(slot[:, e:e + 1] == cidx, 1.0, 0.0).astype(BF16)
        acc = acc + aff[:, e:e + 1] * _dot(onehot, ys_ref[0, e])
    if final_norm:
        acc = acc * lax.rsqrt(jnp.mean(acc * acc, axis=-1, keepdims=True) + NORM_EPS) * g_ref[...]
    o_ref[0] = acc


def _scatter(slot_cols, aff, ys, x1, g, tt, final_norm):
    b, t, d = x1.shape
    e, cap = ys.shape[1], ys.shape[2]
    return pl.pallas_call(
        functools.partial(_scatter_kernel, final_norm=final_norm),
        grid=(b, t // tt),
        in_specs=[pl.BlockSpec((1, tt, e), lambda bi, ti: (bi, ti, 0)),
                  pl.BlockSpec((1, tt, e), lambda bi, ti: (bi, ti, 0)),
                  pl.BlockSpec((1, e, cap, d), lambda bi, ti: (bi, 0, 0, 0)),
                  pl.BlockSpec((1, tt, d), lambda bi, ti: (bi, ti, 0)),
                  pl.BlockSpec((1, d), lambda bi, ti: (0, 0))],
        out_specs=pl.BlockSpec((1, tt, d), lambda bi, ti: (bi, ti, 0)),
        out_shape=jax.ShapeDtypeStruct((b, t, d), F32),
        compiler_params=_params("parallel", "arbitrary"),
    )(slot_cols, aff, ys, x1, g.reshape(1, d).astype(F32))


def kernel(x, norm_mix_g, w_in, mu_prev, mu_next, w0_f, w_up_f, w0_b, w_up_b, a0_f, a_up_f, a0_b, a_up_b, g_up, k_k, k_a, r_k, ln_x_w, ln_x_b, attn_sink, w_proj_rwkv, w_proj_attn, w_out, norm_ffn_g, w_router, exp_w_gate, exp_w_up, exp_w_down, norm_final_g):
    b, t, d = x.shape
    depth = w_in.shape[0]
    n = b * t
    tm = _largest_divisor(t, (512, 256, 128))
    cap = CAPACITY_FACTOR * t // N_EXPERTS
    bb = _largest_divisor(b, (4, 2))
    xc = x
    for l in range(depth):
        x2 = xc.reshape(n, d)
        u_r, qkv, gates = _in_projection(x2, norm_mix_g[l].astype(F32), w_in[l].astype(BF16), mu_prev[l], mu_next[l],
                                         tm, t)
        y_r = _rwkv_branch(u_r.reshape(b, t, RWKV_COLS), w0_f[l], w_up_f[l], w0_b[l],
                           w_up_b[l], a0_f[l], a_up_f[l], a0_b[l], a_up_b[l], g_up[l], k_k[l], k_a[l], r_k[l],
                           ln_x_w[l], ln_x_b[l])
        y_a = _attention(qkv.reshape(b, t, QKV_COLS), attn_sink[l])
        last = l == depth - 1
        x1, h2, aff, aff_t = _merge(y_r.reshape(n, RWKV_DIM), y_a.reshape(n, ATT_DIM), gates, x2, w_proj_rwkv[l],
                                    w_proj_attn[l], w_out[l], norm_ffn_g[l], w_router[l], tm, t)
        slot = _select(aff_t, cap)
        xs = _gather(slot.reshape(b * N_EXPERTS, 1, t), h2.reshape(b, t, d), cap)
        ys = _experts(xs, exp_w_gate[l].astype(BF16), exp_w_up[l].astype(BF16), exp_w_down[l].astype(BF16), bb)
        xc = _scatter(jnp.swapaxes(slot, 1, 2), aff.reshape(b, t, N_EXPERTS), ys, x1.reshape(b, t, d),
                      norm_final_g, tm, final_norm=last)
    return xc
```

```python
import functools
import math

import numpy as np
import jax
import jax.numpy as jnp
from jax import lax
from jax.experimental import pallas as pl
from jax.experimental.pallas import tpu as pltpu

F32 = jnp.float32
BF16 = jnp.bfloat16

RWKV_HEAD = 64
RWKV_HEADS = 8
RWKV_DIM = RWKV_HEADS * RWKV_HEAD
W_LORA = 64
A_LORA = 64
G_LORA = 128
DECAY_SCALE = math.exp(-0.5)
LNX_EPS = 64e-5
ATT_HEADS = 8
ATT_KV_HEADS = 2
ATT_HEAD = 64
ATT_DIM = ATT_HEADS * ATT_HEAD
KV_DIM = ATT_KV_HEADS * ATT_HEAD
WINDOW = 128
BLOCK = 128
NEG_INF = -1e30
N_EXPERTS = 16
CAPACITY_FACTOR = 2
NORM_EPS = 1e-6
RWKV_COLS = 3 * RWKV_DIM + G_LORA + 2 * W_LORA + 2 * A_LORA
QKV_COLS = ATT_DIM + 2 * KV_DIM
OFF_R, OFF_K, OFF_V = 0, RWKV_DIM, 2 * RWKV_DIM
OFF_G = 3 * RWKV_DIM
OFF_WF = OFF_G + G_LORA
OFF_WB = OFF_WF + W_LORA
OFF_AF = OFF_WB + W_LORA
OFF_AB = OFF_AF + A_LORA

LANES = 128
VMEM_LIMIT = 48 * 1024 * 1024

CHUNK = 64


def _dot(a, b):
    return jnp.dot(a, b, preferred_element_type=F32)


def _dot_nt(a, b):
    return lax.dot_general(a, b, (((1,), (1,)), ((), ())), preferred_element_type=F32)


def _split3(x):
    hi = x.astype(BF16)
    r1 = x - hi.astype(F32)
    mid = r1.astype(BF16)
    lo = (r1 - mid.astype(F32)).astype(BF16)
    return hi, mid, lo


def _seg_sum(x, bd):
    width = bd.shape[0]
    return jnp.concatenate([_dot(x[:, j:j + width].astype(BF16), bd) for j in range(0, x.shape[1], width)], axis=1)


def _sigmoid(x):
    return 1.0 / (1.0 + jnp.exp(-x))


def _largest_divisor(n, candidates):
    for c in candidates:
        if n % c == 0:
            return c
    return 1


def _params(*sem):
    return pltpu.CompilerParams(dimension_semantics=sem, vmem_limit_bytes=VMEM_LIMIT)


def _inproj_kernel(x_ref, xp_ref, xn_ref, g_ref, w_ref, mup_ref, mun_ref, ur_ref, qkv_ref, gate_ref, *,
                   col_chunk, tiles_per_seq):
    i = pl.program_id(0)
    tm = x_ref.shape[0]
    halo = xp_ref.shape[0]
    x = jnp.concatenate([xp_ref[...], x_ref[...], xn_ref[...]], axis=0)
    h = x * lax.rsqrt(jnp.mean(x * x, axis=-1, keepdims=True) + NORM_EPS) * g_ref[...]
    row = lax.broadcasted_iota(jnp.int32, (tm + 2 * halo, 1), 0)
    pos = i % tiles_per_seq
    lo = jnp.where(pos == 0, halo, 0)
    hi = jnp.where(pos == tiles_per_seq - 1, tm + halo, tm + 2 * halo)
    h = jnp.where((row < lo) | (row >= hi), 0.0, h)
    hb_all = h.astype(BF16)
    hb = h[halo:halo + tm].astype(BF16)
    for j in range(0, RWKV_COLS, col_chunk):
        w = min(col_chunk, RWKV_COLS - j)
        u = _dot(hb_all, w_ref[:, j:j + w])
        prev = pltpu.roll(u, 1, axis=0)
        nxt = pltpu.roll(u, tm + 2 * halo - 1, axis=0)
        us = u + mup_ref[:, j:j + w] * (prev - u) + mun_ref[:, j:j + w] * (nxt - u)
        ur_ref[:, j:j + w] = us[halo:halo + tm].astype(BF16)
    c0 = RWKV_COLS
    for ref in (qkv_ref, gate_ref):
        width = ref.shape[-1]
        for j in range(0, width, col_chunk):
            w = min(col_chunk, width - j)
            ref[:, j:j + w] = _dot(hb, w_ref[:, c0 + j:c0 + j + w]).astype(BF16)
        c0 += width


def _in_projection(x2, g, w_in_bf, mu_prev, mu_next, tm, seq_len):
    n, d = x2.shape
    cols = w_in_bf.shape[1]
    gate_cols = cols - RWKV_COLS - QKV_COLS
    halo = 8
    hb = tm // halo
    n_halo = n // halo
    return pl.pallas_call(
        functools.partial(_inproj_kernel, col_chunk=512, tiles_per_seq=seq_len // tm),
        grid=(n // tm,),
        in_specs=[pl.BlockSpec((tm, d), lambda i: (i, 0)),
                  pl.BlockSpec((halo, d), lambda i: (jnp.maximum(i * hb - 1, 0), 0)),
                  pl.BlockSpec((halo, d), lambda i: (jnp.minimum((i + 1) * hb, n_halo - 1), 0)),
                  pl.BlockSpec((1, d), lambda i: (0, 0)),
                  pl.BlockSpec((d, cols), lambda i: (0, 0)),
                  pl.BlockSpec((1, RWKV_COLS), lambda i: (0, 0)),
                  pl.BlockSpec((1, RWKV_COLS), lambda i: (0, 0))],
        out_specs=[pl.BlockSpec((tm, RWKV_COLS), lambda i: (i, 0)),
                   pl.BlockSpec((tm, QKV_COLS), lambda i: (i, 0)),
                   pl.BlockSpec((tm, gate_cols), lambda i: (i, 0))],
        out_shape=[jax.ShapeDtypeStruct((n, RWKV_COLS), BF16),
                   jax.ShapeDtypeStruct((n, QKV_COLS), BF16),
                   jax.ShapeDtypeStruct((n, gate_cols), BF16)],
        compiler_params=_params("parallel"),
    )(x2, x2, x2, g.reshape(1, d), w_in_bf, mu_prev.reshape(1, -1).astype(F32), mu_next.reshape(1, -1).astype(F32))


GROUP_HEADS = 4
GROUP = GROUP_HEADS * RWKV_HEAD
N_GROUPS = RWKV_HEADS // GROUP_HEADS
ROWS_PER_STEP = 256
SEQS_PER_STEP = 2


def _rwkv_kernel(*refs, reverse, final):
    if final:
        (u_ref, w0_ref, wup_ref, a0_ref, aup_ref, kk_ref, ka_ref,
         bd_ref, tri_ref, ones_ref, a0o_ref, aupo_ref, gup_ref, rk_ref, lnw_ref, lnb_ref, yb_ref,
         o_ref, st_ref) = refs
    else:
        (u_ref, w0_ref, wup_ref, a0_ref, aup_ref, kk_ref, ka_ref,
         bd_ref, tri_ref, ones_ref, o_ref, st_ref) = refs
    NB, TB = u_ref.shape[0], u_ref.shape[1]
    L = CHUNK
    N = RWKV_HEAD
    n_ck = TB // L
    step = pl.program_id(1)

    @pl.when(step == 0)
    def _():
        st_ref[...] = jnp.zeros_like(st_ref)

    bd = bd_ref[...]
    tri = tri_ref[...]
    ones = ones_ref[...]

    def prep(bi):
        col = lambda off, width: u_ref[bi, :, off:off + width].astype(F32)
        r = col(OFF_R, RWKV_DIM)
        k = col(OFF_K, RWKV_DIM)
        v = col(OFF_V, RWKV_DIM)
        w_lo = col(OFF_WB if reverse else OFF_WF, W_LORA)
        a_lo = col(OFF_AB if reverse else OFF_AF, A_LORA)
        lw = -DECAY_SCALE * _sigmoid(w0_ref[...] + _dot(jnp.tanh(w_lo).astype(BF16), wup_ref[...]))
        a = _sigmoid(a0_ref[...] + _dot(a_lo.astype(BF16), aup_ref[...]))
        kkr = k * kk_ref[...]
        kk = kkr / jnp.maximum(jnp.sqrt(_seg_sum(kkr * kkr, bd)), 1e-12)
        kd = k * (1.0 + (a - 1.0) * ka_ref[...])
        b = a * kk
        l_hi, l_mid, l_lo = _split3(lw)
        cs = _dot(tri, l_hi) + _dot(tri, l_mid) + _dot(tri, l_lo)
        tot = _dot(ones, l_hi) + _dot(ones, l_mid) + _dot(ones, l_lo)
        e_neg = jnp.exp(-cs)
        e_rem = jnp.exp(tot - cs)
        return dict(r=r, k=k, v=v, kd=kd, col=col, e_tot=jnp.exp(tot),
                    a_t=(kk * jnp.exp(cs - lw)).astype(BF16), r_t=(r * jnp.exp(cs)).astype(BF16),
                    k_t=(kd * e_neg).astype(BF16), b_t=(b * e_neg).astype(BF16),
                    k_h=(kd * e_rem).astype(BF16), b_hn=(-b * e_rem).astype(BF16))

    pre = [prep(bi) for bi in range(NB)]

    ii = lax.broadcasted_iota(jnp.int32, (GROUP, GROUP), 0)
    jj = lax.broadcasted_iota(jnp.int32, (GROUP, GROUP), 1)
    same_head = (ii // N) == (jj // N)
    ti = lax.broadcasted_iota(jnp.int32, (L, GROUP), 0)
    si = lax.broadcasted_iota(jnp.int32, (L, GROUP), 1) % N
    if reverse:
        strict = si > ti
        incl = si >= ti
    else:
        strict = si < ti
        incl = si <= ti
    eye = (si == ti).astype(F32)
    level_masks = []
    s = 1
    while s < L:
        same = (ti // (2 * s)) == (si // (2 * s))
        t_hi = (ti & s) != 0
        s_hi = (si & s) != 0
        if reverse:
            level_masks.append(same & jnp.logical_not(t_hi) & s_hi)
        else:
            level_masks.append(same & t_hi & jnp.logical_not(s_hi))
        s *= 2

    def bdiag(xp):
        return jnp.where(same_head, jnp.concatenate([xp.astype(BF16)] * GROUP_HEADS, axis=0), 0.0)

    def head_transpose(xp):
        return jnp.concatenate([xp[:, h * N:(h + 1) * N].T for h in range(GROUP_HEADS)], axis=1)

    order = list(range(n_ck))[::-1] if reverse else list(range(n_ck))
    cgs = [(bi, c, g) for bi in range(NB) for c in order for g in range(N_GROUPS)]

    def tile(name, cg):
        bi, c, g = cg
        return pre[bi][name][c * L:(c + 1) * L, g * GROUP:(g + 1) * GROUP]

    ar = {cg: jnp.concatenate([tile('a_t', cg), tile('r_t', cg)], axis=0) for cg in cgs}
    arb = {cg: _dot_nt(ar[cg], bdiag(tile('b_t', cg))) for cg in cgs}
    ark = {cg: _dot_nt(ar[cg], bdiag(tile('k_t', cg))) for cg in cgs}
    ab = {cg: arb[cg][0:L] for cg in cgs}
    rb = {cg: arb[cg][L:2 * L] for cg in cgs}
    ak = {cg: ark[cg][0:L] for cg in cgs}
    rk_ = {cg: ark[cg][L:2 * L] for cg in cgs}
    x = {cg: eye - jnp.where(level_masks[0], ab[cg], 0.0) for cg in cgs}
    for m in level_masks[1:]:
        xn = {cg: _dot(x[cg].astype(BF16), bdiag(jnp.where(m, ab[cg], 0.0))).astype(BF16) for cg in cgs}
        x = {cg: x[cg] - _dot(xn[cg], bdiag(x[cg])) for cg in cgs}
    xb = {cg: x[cg].astype(BF16) for cg in cgs}
    xa = {cg: _dot(xb[cg], bdiag(tile('a_t', cg))) for cg in cgs}
    xa_w = {cg: bdiag(xa[cg]) for cg in cgs}
    mrb_n = {cg: jnp.where(incl, -rb[cg], 0.0).astype(BF16) for cg in cgs}
    v_w = {cg: bdiag(tile('v', cg)) for cg in cgs}
    mv = {cg: _dot(jnp.concatenate([jnp.where(strict, ak[cg], 0.0).astype(BF16),
                                    jnp.where(incl, rk_[cg], 0.0).astype(BF16)], axis=0), v_w[cg]) for cg in cgs}
    xmv = {cg: _dot(xb[cg], bdiag(mv[cg][0:L])) for cg in cgs}
    rq = {cg: (tile('r_t', cg).astype(F32) + _dot(mrb_n[cg], xa_w[cg])).astype(BF16) for cg in cgs}
    xmv_w = {cg: bdiag(xmv[cg]) for cg in cgs}
    dd = {cg: mv[cg][L:2 * L] + _dot(mrb_n[cg], xmv_w[cg]) for cg in cgs}
    eye_b = eye.astype(BF16)
    xt = {cg: jnp.concatenate([_dot_nt(eye_b, xa_w[cg]), _dot_nt(eye_b, xmv_w[cg])], axis=0).astype(BF16) for cg in cgs}
    wb = {cg: _dot(xt[cg], bdiag(tile('b_hn', cg))) for cg in cgs}
    cc = {cg: _dot(head_transpose(tile('v', cg)).astype(BF16), bdiag(tile('k_h', cg))) + wb[cg][L:2 * L] for cg in cgs}
    w_w = {cg: bdiag(wb[cg][0:L]) for cg in cgs}

    state = {(bi, g): st_ref[bi, g] for bi in range(NB) for g in range(N_GROUPS)}
    y_tiles = {}
    for c in order:
        for bi in range(NB):
            for g in range(N_GROUPS):
                cg = (bi, c, g)
                sb = state[bi, g].astype(BF16)
                state[bi, g] = state[bi, g] * tile('e_tot', cg)[0:1] + _dot(sb, w_w[cg]) + cc[cg]
                y_tiles[cg] = _dot_nt(rq[cg], bdiag(sb)) + dd[cg]
    for key, val in state.items():
        st_ref[key] = val

    for bi in range(NB):
        wkv = jnp.concatenate(
            [jnp.concatenate([y_tiles[(bi, c, g)] for g in range(N_GROUPS)], axis=1) for c in range(n_ck)], axis=0)
        if not final:
            o_ref[bi] = wkv
            continue
        p = pre[bi]
        r, k, v, kd, col = p['r'], p['k'], p['v'], p['kd'], p['col']
        wkv = wkv + yb_ref[bi]
        inv_n = 1.0 / N
        mean = _seg_sum(wkv, bd) * inv_n
        xc = wkv - mean
        var = _seg_sum(xc * xc, bd) * inv_n
        normed = xc * lax.rsqrt(var + LNX_EPS) * lnw_ref[...] + lnb_ref[...]
        a_o = _sigmoid(a0o_ref[...] + _dot(col(OFF_AB, A_LORA).astype(BF16), aupo_ref[...]))
        kd_o = k * (1.0 + (a_o - 1.0) * ka_ref[...])
        bonus = _seg_sum(r * rk_ref[...] * (kd + kd_o), bd) * v
        gg = _dot(_sigmoid(col(OFF_G, G_LORA)).astype(BF16), gup_ref[...])
        o_ref[bi] = ((normed + bonus) * gg).astype(o_ref.dtype)


def _rwkv_pass(u_r, consts, *, reverse, final, extra=()):
    b, t, _ = u_r.shape
    tb = _largest_divisor(t, (ROWS_PER_STEP, CHUNK))
    nb = _largest_divisor(b, (SEQS_PER_STEP,))
    n_steps = t // tb

    def sidx(s):
        return (n_steps - 1 - s) if reverse else s

    def full(shape):
        return pl.BlockSpec(shape, lambda bi, s: (0,) * len(shape))

    in_specs = [pl.BlockSpec((nb, tb, RWKV_COLS), lambda bi, s: (bi, sidx(s), 0))]
    args = [u_r]
    for arr in consts:
        in_specs.append(full(arr.shape))
        args.append(arr)
    if final:
        for arr in extra[:-1]:
            in_specs.append(full(arr.shape))
            args.append(arr)
        in_specs.append(pl.BlockSpec((nb, tb, RWKV_DIM), lambda bi, s: (bi, sidx(s), 0)))
        args.append(extra[-1])
    out_dtype = BF16 if final else F32
    return pl.pallas_call(
        functools.partial(_rwkv_kernel, reverse=reverse, final=final),
        grid=(b // nb, n_steps),
        in_specs=in_specs,
        out_specs=pl.BlockSpec((nb, tb, RWKV_DIM), lambda bi, s: (bi, sidx(s), 0)),
        out_shape=jax.ShapeDtypeStruct((b, t, RWKV_DIM), out_dtype),
        scratch_shapes=[pltpu.VMEM((nb, N_GROUPS, RWKV_HEAD, GROUP), F32)],
        compiler_params=_params("parallel", "arbitrary"),
    )(*args)


def _rwkv_branch(u_r, w0_f, w_up_f, w0_b, w_up_b, a0_f, a_up_f, a0_b, a_up_b,
                 g_up, k_k, k_a, r_k, ln_w, ln_b):
    t = u_r.shape[1]
    tb = _largest_divisor(t, (ROWS_PER_STEP, CHUNK))
    row = lambda a: a.reshape(1, -1).astype(F32)
    seg = np.arange(GROUP) // RWKV_HEAD
    bd = jnp.asarray(seg[:, None] == seg[None, :], BF16)
    ti = np.arange(tb)
    same_chunk = (ti[:, None] // CHUNK) == (ti[None, :] // CHUNK)
    tri_f = jnp.asarray(same_chunk & (ti[None, :] <= ti[:, None]), BF16)
    tri_b = jnp.asarray(same_chunk & (ti[None, :] >= ti[:, None]), BF16)
    ones = jnp.asarray(same_chunk, BF16)
    common = lambda w0, wup, a0, aup, tri: [row(w0), wup.astype(BF16), row(a0), aup.astype(BF16), row(k_k), row(k_a),
                                            bd, tri, ones]
    y_b = _rwkv_pass(u_r, common(w0_b, w_up_b, a0_b, a_up_b, tri_b), reverse=True, final=False)
    extra = [row(a0_b), a_up_b.astype(BF16), g_up.astype(BF16), row(r_k), row(ln_w), row(ln_b), y_b]
    return _rwkv_pass(u_r, common(w0_f, w_up_f, a0_f, a_up_f, tri_f), reverse=False, final=True, extra=extra)


def _attn_kernel(sink_ref, bias_ref, q_ref, kp_ref, kc_ref, kn_ref, vp_ref, vc_ref, vn_ref, o_ref, *, seq_len):
    i = pl.program_id(1)
    span = BLOCK + 2 * WINDOW
    kk = jnp.concatenate([kp_ref[0], kc_ref[0], kn_ref[0]], axis=0)
    vv = jnp.concatenate([vp_ref[0], vc_ref[0], vn_ref[0]], axis=0)
    key_pos = i * BLOCK - WINDOW + lax.broadcasted_iota(jnp.int32, (1, span), 1)
    in_seq = (key_pos >= 0) & (key_pos < seq_len)
    group = ATT_HEADS // ATT_KV_HEADS
    qs = q_ref[0] * (ATT_HEAD ** -0.5)
    heads = range(ATT_HEADS)
    hsl = lambda h: slice(h * ATT_HEAD, (h + 1) * ATT_HEAD)
    s = [_dot_nt(qs[:, hsl(h)], kk[:, hsl(h // group)]) for h in heads]
    s = [jnp.where(in_seq, s[h] + bias_ref[h], NEG_INF) for h in heads]
    m = [jnp.maximum(jnp.max(s[h], axis=-1, keepdims=True), sink_ref[h]) for h in heads]
    p = [jnp.exp(s[h] - m[h]) for h in heads]
    denom = [jnp.sum(p[h], axis=-1, keepdims=True) + jnp.exp(sink_ref[h] - m[h]) for h in heads]
    outs = [_dot(p[h].astype(BF16), vv[:, hsl(h // group)]) / denom[h] for h in heads]
    o_ref[0] = jnp.concatenate(outs, axis=-1).astype(o_ref.dtype)


def _attention(qkv, sink):
    b, t, _ = qkv.shape
    nb = t // BLOCK
    span = BLOCK + 2 * WINDOW
    kcol = ATT_DIM // KV_DIM
    prev = lambda i: jnp.maximum(i - 1, 0)
    nxt = lambda i: jnp.minimum(i + 1, nb - 1)
    kv_spec = lambda col, f: pl.BlockSpec((1, BLOCK, KV_DIM), lambda bi, i: (bi, f(i), col))
    same = lambda i: i
    rel = np.abs(np.arange(span)[None, :] - WINDOW - np.arange(BLOCK)[:, None])
    slopes = 2.0 ** (-8.0 * np.arange(1, ATT_HEADS + 1) / ATT_HEADS)
    bias = np.where(rel[None] <= WINDOW, -slopes[:, None, None] * rel[None], NEG_INF).astype(np.float32)
    return pl.pallas_call(
        functools.partial(_attn_kernel, seq_len=t),
        grid=(b, nb),
        in_specs=[pl.BlockSpec(memory_space=pltpu.SMEM),
                  pl.BlockSpec((ATT_HEADS, BLOCK, span), lambda bi, i: (0, 0, 0)),
                  pl.BlockSpec((1, BLOCK, ATT_DIM), lambda bi, i: (bi, i, 0)),
                  kv_spec(kcol, prev), kv_spec(kcol, same), kv_spec(kcol, nxt),
                  kv_spec(kcol + 1, prev), kv_spec(kcol + 1, same), kv_spec(kcol + 1, nxt)],
        out_specs=pl.BlockSpec((1, BLOCK, ATT_DIM), lambda bi, i: (bi, i, 0)),
        out_shape=jax.ShapeDtypeStruct((b, t, ATT_DIM), BF16),
        compiler_params=_params("parallel", "parallel"),
    )(sink.astype(F32), jnp.asarray(bias), qkv, qkv, qkv, qkv, qkv, qkv, qkv)


def _merge_kernel(yr_ref, ya_ref, gate_ref, x_ref, wpr_ref, wpa_ref, wo_ref, g_ref, wrt_ref,
                  x1_ref, h2_ref, aff_ref, afft_ref):
    d = x_ref.shape[-1]
    gl_r = gate_ref[:, 0:d].astype(F32)
    gl_a = gate_ref[:, d:2 * d].astype(F32)
    merged = (_sigmoid(gl_r) * _dot(yr_ref[...], wpr_ref[...])
              + _sigmoid(gl_a) * _dot(ya_ref[...], wpa_ref[...]))
    x1 = x_ref[...] + _dot(merged.astype(BF16), wo_ref[...])
    x1_ref[...] = x1
    h2 = (x1 * lax.rsqrt(jnp.mean(x1 * x1, axis=-1, keepdims=True) + NORM_EPS) * g_ref[...]).astype(BF16)
    h2_ref[...] = h2
    logits_t = _dot_nt(wrt_ref[...], h2)
    mx = jnp.max(logits_t, axis=0, keepdims=True)
    ex = jnp.exp(logits_t - mx)
    aff_t = ex / jnp.sum(ex, axis=0, keepdims=True)
    afft_ref[0] = aff_t
    aff_ref[...] = aff_t.T


def _merge(y_r, y_a, gates, x2, wpr, wpa, wo, g, w_router, tm, seq_len):
    n, d = x2.shape
    e = w_router.shape[1]
    tiles_per_seq = seq_len // tm
    full = lambda shape: pl.BlockSpec(shape, lambda i: (0,) * len(shape))
    rows = lambda w: pl.BlockSpec((tm, w), lambda i: (i, 0))
    return pl.pallas_call(
        _merge_kernel,
        grid=(n // tm,),
        in_specs=[rows(RWKV_DIM), rows(ATT_DIM), rows(2 * d), rows(d),
                  full(wpr.shape), full(wpa.shape), full(wo.shape), full((1, d)), full((e, d))],
        out_specs=[rows(d), rows(d), rows(e),
                   pl.BlockSpec((1, e, tm), lambda i: (i // tiles_per_seq, 0, i % tiles_per_seq))],
        out_shape=[jax.ShapeDtypeStruct((n, d), F32), jax.ShapeDtypeStruct((n, d), BF16),
                   jax.ShapeDtypeStruct((n, e), F32),
                   jax.ShapeDtypeStruct((n // seq_len, e, seq_len), F32)],
        compiler_params=_params("parallel"),
    )(y_r, y_a, gates, x2, wpr.astype(BF16), wpa.astype(BF16), wo.astype(BF16), g.reshape(1, d).astype(F32),
      w_router.T.astype(BF16))


def _select_kernel(a_ref, gsame_ref, gprev_ref, triu_ref, slot_ref, *, cap):
    nb, rows, _ = a_ref.shape
    affs = [a_ref[i] for i in range(nb)]
    gsame = gsame_ref[...]
    gprev = gprev_ref[...]
    triu = triu_ref[...]
    capf = float(cap)

    def group_count(mask):
        per_lane = _dot(gsame, jnp.where(mask, 1.0, 0.0).astype(BF16))
        return jnp.sum(per_lane, axis=-1, keepdims=True)

    def search(i, prefixes):
        out = []
        for aff, prefix in zip(affs, prefixes):
            cand = prefix | lax.shift_left(jnp.int32(1), 30 - i)
            out.append(jnp.where(group_count(aff >= pltpu.bitcast(cand, F32)) >= capf, cand, prefix))
        return tuple(out)

    thr_bits = lax.fori_loop(0, 31, search, tuple(jnp.zeros((rows, 1), jnp.int32) for _ in range(nb)))

    def prefix_count(mask):
        mb = jnp.where(mask, 1.0, 0.0)
        inc = _dot(mb.astype(BF16), triu)
        tot = jnp.broadcast_to(inc[:, LANES - 1:LANES], inc.shape).astype(BF16)
        return inc - mb + _dot(gprev, tot)

    for i, aff in enumerate(affs):
        thr = pltpu.bitcast(thr_bits[i], F32)
        gt = aff > thr
        eq = aff == thr
        need = capf - group_count(gt)
        sel = gt | (eq & (prefix_count(eq) < need))
        slot_ref[i] = jnp.where(sel, prefix_count(sel), -1.0).astype(jnp.int32)


def _select(aff_t, cap):
    b, e, t = aff_t.shape
    tiles = t // LANES
    rows = e * tiles
    nb = _largest_divisor(b, (4, 2))
    rid = np.arange(rows)
    same = (rid[:, None] // tiles) == (rid[None, :] // tiles)
    gsame = jnp.asarray(same, BF16)
    gprev = jnp.asarray(same & (rid[None, :] < rid[:, None]), BF16)
    li = np.arange(LANES)
    triu = jnp.asarray(li[:, None] <= li[None, :], BF16)
    full = lambda shape: pl.BlockSpec(shape, lambda bi: (0,) * len(shape))
    slot = pl.pallas_call(
        functools.partial(_select_kernel, cap=cap),
        grid=(b // nb,),
        in_specs=[pl.BlockSpec((nb, rows, LANES), lambda bi: (bi, 0, 0)),
                  full((rows, rows)), full((rows, rows)), full((LANES, LANES))],
        out_specs=pl.BlockSpec((nb, rows, LANES), lambda bi: (bi, 0, 0)),
        out_shape=jax.ShapeDtypeStruct((b, rows, LANES), jnp.int32),
        compiler_params=_params("parallel"),
    )(aff_t.reshape(b, rows, LANES), gsame, gprev, triu)
    return slot.reshape(b, e, t)


def _gather_kernel(slot_ref, h_ref, xs_ref):
    cap = xs_ref.shape[2]
    slot = slot_ref[0]
    cidx = lax.broadcasted_iota(jnp.int32, (cap, 1), 0)
    onehot = jnp.where(cidx == slot, 1.0, 0.0).astype(BF16)
    xs_ref[0, 0] = _dot(onehot, h_ref[0]).astype(xs_ref.dtype)


def _gather(slot_rows, h2, cap):
    b, t, d = h2.shape
    e = slot_rows.shape[0] // b
    return pl.pallas_call(
        _gather_kernel,
        grid=(b, e),
        in_specs=[pl.BlockSpec((1, 1, t), lambda bi, ei: (bi * e + ei, 0, 0)),
                  pl.BlockSpec((1, t, d), lambda bi, ei: (bi, 0, 0))],
        out_specs=pl.BlockSpec((1, 1, cap, d), lambda bi, ei: (bi, ei, 0, 0)),
        out_shape=jax.ShapeDtypeStruct((b, e, cap, d), BF16),
        compiler_params=_params("parallel", "arbitrary"),
    )(slot_rows, h2)


def _expert_kernel(xs_ref, wg_ref, wu_ref, wd_ref, ys_ref):
    bb, _, cap, d = xs_ref.shape
    xs = xs_ref[...].reshape(bb * cap, d)
    gate = _dot(xs, wg_ref[0])
    up = _dot(xs, wu_ref[0])
    hid = (gate * _sigmoid(gate) * up).astype(BF16)
    ys_ref[...] = _dot(hid, wd_ref[0]).astype(ys_ref.dtype).reshape(bb, 1, cap, d)


def _experts(xs, wg, wu, wd, bb):
    b, e, cap, d = xs.shape
    ff = wg.shape[-1]
    tok = pl.BlockSpec((bb, 1, cap, d), lambda ei, bi: (bi, ei, 0, 0))
    return pl.pallas_call(
        _expert_kernel,
        grid=(e, b // bb),
        in_specs=[tok,
                  pl.BlockSpec((1, d, ff), lambda ei, bi: (ei, 0, 0)),
                  pl.BlockSpec((1, d, ff), lambda ei, bi: (ei, 0, 0)),
                  pl.BlockSpec((1, ff, d), lambda ei, bi: (ei, 0, 0))],
        out_specs=tok,
        out_shape=jax.ShapeDtypeStruct((b, e, cap, d), BF16),
        compiler_params=_params("parallel", "arbitrary"),
    )(xs, wg, wu, wd)


def _scatter_kernel(slot_ref, aff_ref, ys_ref, x1_ref, g_ref, o_ref, *, final_norm):
    n_exp, cap = ys_ref.shape[1], ys_ref.shape[2]
    slot = slot_ref[0]
    aff = aff_ref[0]
    cidx = lax.broadcasted_iota(jnp.int32, (1, cap), 1)
    acc = x1_ref[0]
    for e in range(n_exp):
        onehot = jnp.where(slot[:, e:e + 1] == cidx, 1.0, 0.0).astype(BF16)
        acc = acc + aff[:, e:e + 1] * _dot(onehot, ys_ref[0, e])
    if final_norm:
        acc = acc * lax.rsqrt(jnp.mean(acc * acc, axis=-1, keepdims=True) + NORM_EPS) * g_ref[...]
    o_ref[0] = acc


def _scatter(slot_cols, aff, ys, x1, g, tt, final_norm):
    b, t, d = x1.shape
    e, cap = ys.shape[1], ys.shape[2]
    return pl.pallas_call(
        functools.partial(_scatter_kernel, final_norm=final_norm),
        grid=(b, t // tt),
        in_specs=[pl.BlockSpec((1, tt, e), lambda bi, ti: (bi, ti, 0)),
                  pl.BlockSpec((1, tt, e), lambda bi, ti: (bi, ti, 0)),
                  pl.BlockSpec((1, e, cap, d), lambda bi, ti: (bi, 0, 0, 0)),
                  pl.BlockSpec((1, tt, d), lambda bi, ti: (bi, ti, 0)),
                  pl.BlockSpec((1, d), lambda bi, ti: (0, 0))],
        out_specs=pl.BlockSpec((1, tt, d), lambda bi, ti: (bi, ti, 0)),
        out_shape=jax.ShapeDtypeStruct((b, t, d), F32),
        compiler_params=_params("parallel", "arbitrary"),
    )(slot_cols, aff, ys, x1, g.reshape(1, d).astype(F32))


def kernel(x, norm_mix_g, w_in, mu_prev, mu_next, w0_f, w_up_f, w0_b, w_up_b, a0_f, a_up_f, a0_b, a_up_b, g_up, k_k, k_a, r_k, ln_x_w, ln_x_b, attn_sink, w_proj_rwkv, w_proj_attn, w_out, norm_ffn_g, w_router, exp_w_gate, exp_w_up, exp_w_down, norm_final_g):
    b, t, d = x.shape
    depth = w_in.shape[0]
    n = b * t
    tm = _largest_divisor(t, (512, 256, 128))
    cap = CAPACITY_FACTOR * t // N_EXPERTS
    bb = _largest_divisor(b, (4, 2))
    xc = x
    for l in range(depth):
        x2 = xc.reshape(n, d)
        u_r, qkv, gates = _in_projection(x2, norm_mix_g[l].astype(F32), w_in[l].astype(BF16), mu_prev[l], mu_next[l],
                                         tm, t)
        y_r = _rwkv_branch(u_r.reshape(b, t, RWKV_COLS), w0_f[l], w_up_f[l], w0_b[l],
                           w_up_b[l], a0_f[l], a_up_f[l], a0_b[l], a_up_b[l], g_up[l], k_k[l], k_a[l], r_k[l],
                           ln_x_w[l], ln_x_b[l])
        y_a = _attention(qkv.reshape(b, t, QKV_COLS), attn_sink[l])
        last = l == depth - 1
        x1, h2, aff, aff_t = _merge(y_r.reshape(n, RWKV_DIM), y_a.reshape(n, ATT_DIM), gates, x2, w_proj_rwkv[l],
                                    w_proj_attn[l], w_out[l], norm_ffn_g[l], w_router[l], tm, t)
        slot = _select(aff_t, cap)
        xs = _gather(slot.reshape(b * N_EXPERTS, 1, t), h2.reshape(b, t, d), cap)
        ys = _experts(xs, exp_w_gate[l].astype(BF16), exp_w_up[l].astype(BF16), exp_w_down[l].astype(BF16), bb)
        xc = _scatter(jnp.swapaxes(slot, 1, 2), aff.reshape(b, t, N_EXPERTS), ys, x1.reshape(b, t, d),
                      norm_final_g, tm, final_norm=last)
    return xc
```

```python
import functools
import math

import numpy as np
import jax
import jax.numpy as jnp
from jax import lax
from jax.experimental import pallas as pl
from jax.experimental.pallas import tpu as pltpu

F32 = jnp.float32
BF16 = jnp.bfloat16

RWKV_HEAD = 64
RWKV_HEADS = 8
RWKV_DIM = RWKV_HEADS * RWKV_HEAD
W_LORA = 64
A_LORA = 64
G_LORA = 128
DECAY_SCALE = math.exp(-0.5)
LNX_EPS = 64e-5
ATT_HEADS = 8
ATT_KV_HEADS = 2
ATT_HEAD = 64
ATT_DIM = ATT_HEADS * ATT_HEAD
KV_DIM = ATT_KV_HEADS * ATT_HEAD
WINDOW = 128
BLOCK = 128
NEG_INF = -1e30
N_EXPERTS = 16
CAPACITY_FACTOR = 2
NORM_EPS = 1e-6
RWKV_COLS = 3 * RWKV_DIM + G_LORA + 2 * W_LORA + 2 * A_LORA
QKV_COLS = ATT_DIM + 2 * KV_DIM
OFF_R, OFF_K, OFF_V = 0, RWKV_DIM, 2 * RWKV_DIM
OFF_G = 3 * RWKV_DIM
OFF_WF = OFF_G + G_LORA
OFF_WB = OFF_WF + W_LORA
OFF_AF = OFF_WB + W_LORA
OFF_AB = OFF_AF + A_LORA

LANES = 128
VMEM_LIMIT = 48 * 1024 * 1024

CHUNK = 64


def _dot(a, b):
    return jnp.dot(a, b, preferred_element_type=F32)


def _dot_nt(a, b):
    return lax.dot_general(a, b, (((1,), (1,)), ((), ())), preferred_element_type=F32)


def _split3(x):
    hi = x.astype(BF16)
    r1 = x - hi.astype(F32)
    mid = r1.astype(BF16)
    lo = (r1 - mid.astype(F32)).astype(BF16)
    return hi, mid, lo


def _seg_sum(x, bd):
    width = bd.shape[0]
    return jnp.concatenate([_dot(x[:, j:j + width].astype(BF16), bd) for j in range(0, x.shape[1], width)], axis=1)


def _sigmoid(x):
    return 1.0 / (1.0 + jnp.exp(-x))


def _largest_divisor(n, candidates):
    for c in candidates:
        if n % c == 0:
            return c
    return 1


def _params(*sem):
    return pltpu.CompilerParams(dimension_semantics=sem, vmem_limit_bytes=VMEM_LIMIT)


def _inproj_kernel(x_ref, xp_ref, xn_ref, g_ref, w_ref, mup_ref, mun_ref, ur_ref, qkv_ref, gate_ref, *,
                   col_chunk, tiles_per_seq):
    i = pl.program_id(0)
    tm = x_ref.shape[0]
    halo = xp_ref.shape[0]
    x = jnp.concatenate([xp_ref[...], x_ref[...], xn_ref[...]], axis=0)
    h = x * lax.rsqrt(jnp.mean(x * x, axis=-1, keepdims=True) + NORM_EPS) * g_ref[...]
    row = lax.broadcasted_iota(jnp.int32, (tm + 2 * halo, 1), 0)
    pos = i % tiles_per_seq
    lo = jnp.where(pos == 0, halo, 0)
    hi = jnp.where(pos == tiles_per_seq - 1, tm + halo, tm + 2 * halo)
    h = jnp.where((row < lo) | (row >= hi), 0.0, h)
    hb_all = h.astype(BF16)
    hb = h[halo:halo + tm].astype(BF16)
    for j in range(0, RWKV_COLS, col_chunk):
        w = min(col_chunk, RWKV_COLS - j)
        u = _dot(hb_all, w_ref[:, j:j + w])
        prev = pltpu.roll(u, 1, axis=0)
        nxt = pltpu.roll(u, tm + 2 * halo - 1, axis=0)
        us = u + mup_ref[:, j:j + w] * (prev - u) + mun_ref[:, j:j + w] * (nxt - u)
        ur_ref[:, j:j + w] = us[halo:halo + tm].astype(BF16)
    c0 = RWKV_COLS
    for ref in (qkv_ref, gate_ref):
        width = ref.shape[-1]
        for j in range(0, width, col_chunk):
            w = min(col_chunk, width - j)
            ref[:, j:j + w] = _dot(hb, w_ref[:, c0 + j:c0 + j + w]).astype(BF16)
        c0 += width


def _in_projection(x2, g, w_in_bf, mu_prev, mu_next, tm, seq_len):
    n, d = x2.shape
    cols = w_in_bf.shape[1]
    gate_cols = cols - RWKV_COLS - QKV_COLS
    halo = 8
    hb = tm // halo
    n_halo = n // halo
    return pl.pallas_call(
        functools.partial(_inproj_kernel, col_chunk=512, tiles_per_seq=seq_len // tm),
        grid=(n // tm,),
        in_specs=[pl.BlockSpec((tm, d), lambda i: (i, 0)),
                  pl.BlockSpec((halo, d), lambda i: (jnp.maximum(i * hb - 1, 0), 0)),
                  pl.BlockSpec((halo, d), lambda i: (jnp.minimum((i + 1) * hb, n_halo - 1), 0)),
                  pl.BlockSpec((1, d), lambda i: (0, 0)),
                  pl.BlockSpec((d, cols), lambda i: (0, 0)),
                  pl.BlockSpec((1, RWKV_COLS), lambda i: (0, 0)),
                  pl.BlockSpec((1, RWKV_COLS), lambda i: (0, 0))],
        out_specs=[pl.BlockSpec((tm, RWKV_COLS), lambda i: (i, 0)),
                   pl.BlockSpec((tm, QKV_COLS), lambda i: (i, 0)),
                   pl.BlockSpec((tm, gate_cols), lambda i: (i, 0))],
        out_shape=[jax.ShapeDtypeStruct((n, RWKV_COLS), BF16),
                   jax.ShapeDtypeStruct((n, QKV_COLS), BF16),
                   jax.ShapeDtypeStruct((n, gate_cols), BF16)],
        compiler_params=_params("parallel"),
    )(x2, x2, x2, g.reshape(1, d), w_in_bf, mu_prev.reshape(1, -1).astype(F32), mu_next.reshape(1, -1).astype(F32))


GROUP_HEADS = 4
GROUP = GROUP_HEADS * RWKV_HEAD
N_GROUPS = RWKV_HEADS // GROUP_HEADS
ROWS_PER_STEP = 256
SEQS_PER_STEP = 2


def _rwkv_kernel(*refs, reverse, final):
    if final:
        (u_ref, w0_ref, wup_ref, a0_ref, aup_ref, kk_ref, ka_ref,
         bd_ref, tri_ref, ones_ref, a0o_ref, aupo_ref, gup_ref, rk_ref, lnw_ref, lnb_ref, yb_ref,
         o_ref, st_ref) = refs
    else:
        (u_ref, w0_ref, wup_ref, a0_ref, aup_ref, kk_ref, ka_ref,
         bd_ref, tri_ref, ones_ref, o_ref, st_ref) = refs
    NB, TB = u_ref.shape[0], u_ref.shape[1]
    L = CHUNK
    N = RWKV_HEAD
    n_ck = TB // L
    step = pl.program_id(1)

    @pl.when(step == 0)
    def _():
        st_ref[...] = jnp.zeros_like(st_ref)

    bd = bd_ref[...]
    tri = tri_ref[...]
    ones = ones_ref[...]

    def prep(bi):
        col = lambda off, width: u_ref[bi, :, off:off + width].astype(F32)
        r = col(OFF_R, RWKV_DIM)
        k = col(OFF_K, RWKV_DIM)
        v = col(OFF_V, RWKV_DIM)
        w_lo = col(OFF_WB if reverse else OFF_WF, W_LORA)
        a_lo = col(OFF_AB if reverse else OFF_AF, A_LORA)
        lw = -DECAY_SCALE * _sigmoid(w0_ref[...] + _dot(jnp.tanh(w_lo).astype(BF16), wup_ref[...]))
        a = _sigmoid(a0_ref[...] + _dot(a_lo.astype(BF16), aup_ref[...]))
        kkr = k * kk_ref[...]
        kk = kkr / jnp.maximum(jnp.sqrt(_seg_sum(kkr * kkr, bd)), 1e-12)
        kd = k * (1.0 + (a - 1.0) * ka_ref[...])
        b = a * kk
        l_hi, l_mid, l_lo = _split3(lw)
        cs = _dot(tri, l_hi) + _dot(tri, l_mid) + _dot(tri, l_lo)
        tot = _dot(ones, l_hi) + _dot(ones, l_mid) + _dot(ones, l_lo)
        e_neg = jnp.exp(-cs)
        e_rem = jnp.exp(tot - cs)
        return dict(r=r, k=k, v=v, kd=kd, col=col, e_tot=jnp.exp(tot),
                    a_t=(kk * jnp.exp(cs - lw)).astype(BF16), r_t=(r * jnp.exp(cs)).astype(BF16),
                    k_t=(kd * e_neg).astype(BF16), b_t=(b * e_neg).astype(BF16),
                    k_h=(kd * e_rem).astype(BF16), b_hn=(-b * e_rem).astype(BF16))

    pre = [prep(bi) for bi in range(NB)]

    ii = lax.broadcasted_iota(jnp.int32, (GROUP, GROUP), 0)
    jj = lax.broadcasted_iota(jnp.int32, (GROUP, GROUP), 1)
    same_head = (ii // N) == (jj // N)
    ti = lax.broadcasted_iota(jnp.int32, (L, GROUP), 0)
    si = lax.broadcasted_iota(jnp.int32, (L, GROUP), 1) % N
    if reverse:
        strict = si > ti
        incl = si >= ti
    else:
        strict = si < ti
        incl = si <= ti
    eye = (si == ti).astype(F32)
    level_masks = []
    s = 1
    while s < L:
        same = (ti // (2 * s)) == (si // (2 * s))
        t_hi = (ti & s) != 0
        s_hi = (si & s) != 0
        if reverse:
            level_masks.append(same & jnp.logical_not(t_hi) & s_hi)
        else:
            level_masks.append(same & t_hi & jnp.logical_not(s_hi))
        s *= 2

    def bdiag(xp):
        return jnp.where(same_head, jnp.concatenate([xp.astype(BF16)] * GROUP_HEADS, axis=0), 0.0)

    def head_transpose(xp):
        return jnp.concatenate([xp[:, h * N:(h + 1) * N].T for h in range(GROUP_HEADS)], axis=1)

    order = list(range(n_ck))[::-1] if reverse else list(range(n_ck))
    cgs = [(bi, c, g) for bi in range(NB) for c in order for g in range(N_GROUPS)]

    def tile(name, cg):
        bi, c, g = cg
        return pre[bi][name][c * L:(c + 1) * L, g * GROUP:(g + 1) * GROUP]

    ar = {cg: jnp.concatenate([tile('a_t', cg), tile('r_t', cg)], axis=0) for cg in cgs}
    arb = {cg: _dot_nt(ar[cg], bdiag(tile('b_t', cg))) for cg in cgs}
    ark = {cg: _dot_nt(ar[cg], bdiag(tile('k_t', cg))) for cg in cgs}
    ab = {cg: arb[cg][0:L] for cg in cgs}
    rb = {cg: arb[cg][L:2 * L] for cg in cgs}
    ak = {cg: ark[cg][0:L] for cg in cgs}
    rk_ = {cg: ark[cg][L:2 * L] for cg in cgs}
    x = {cg: eye - jnp.where(level_masks[0], ab[cg], 0.0) for cg in cgs}
    for m in level_masks[1:]:
        xn = {cg: _dot(x[cg].astype(BF16), bdiag(jnp.where(m, ab[cg], 0.0))).astype(BF16) for cg in cgs}
        x = {cg: x[cg] - _dot(xn[cg], bdiag(x[cg])) for cg in cgs}
    xb = {cg: x[cg].astype(BF16) for cg in cgs}
    xa = {cg: _dot(xb[cg], bdiag(tile('a_t', cg))) for cg in cgs}
    xa_w = {cg: bdiag(xa[cg]) for cg in cgs}
    mrb_n = {cg: jnp.where(incl, -rb[cg], 0.0).astype(BF16) for cg in cgs}
    v_w = {cg: bdiag(tile('v', cg)) for cg in cgs}
    mv = {cg: _dot(jnp.concatenate([jnp.where(strict, ak[cg], 0.0).astype(BF16),
                                    jnp.where(incl, rk_[cg], 0.0).astype(BF16)], axis=0), v_w[cg]) for cg in cgs}
    xmv = {cg: _dot(xb[cg], bdiag(mv[cg][0:L])) for cg in cgs}
    rq = {cg: (tile('r_t', cg).astype(F32) + _dot(mrb_n[cg], xa_w[cg])).astype(BF16) for cg in cgs}
    xmv_w = {cg: bdiag(xmv[cg]) for cg in cgs}
    dd = {cg: mv[cg][L:2 * L] + _dot(mrb_n[cg], xmv_w[cg]) for cg in cgs}
    eye_b = eye.astype(BF16)
    xt = {cg: jnp.concatenate([_dot_nt(eye_b, xa_w[cg]), _dot_nt(eye_b, xmv_w[cg])], axis=0).astype(BF16) for cg in cgs}
    wb = {cg: _dot(xt[cg], bdiag(tile('b_hn', cg))) for cg in cgs}
    cc = {cg: _dot(head_transpose(tile('v', cg)).astype(BF16), bdiag(tile('k_h', cg))) + wb[cg][L:2 * L] for cg in cgs}
    w_w = {cg: bdiag(wb[cg][0:L]) for cg in cgs}

    state = {(bi, g): st_ref[bi, g] for bi in range(NB) for g in range(N_GROUPS)}
    y_tiles = {}
    for c in order:
        for bi in range(NB):
            for g in range(N_GROUPS):
                cg = (bi, c, g)
                sb = state[bi, g].astype(BF16)
                state[bi, g] = state[bi, g] * tile('e_tot', cg)[0:1] + _dot(sb, w_w[cg]) + cc[cg]
                y_tiles[cg] = _dot_nt(rq[cg], bdiag(sb)) + dd[cg]
    for key, val in state.items():
        st_ref[key] = val

    for bi in range(NB):
        wkv = jnp.concatenate(
            [jnp.concatenate([y_tiles[(bi, c, g)] for g in range(N_GROUPS)], axis=1) for c in range(n_ck)], axis=0)
        if not final:
            o_ref[bi] = wkv
            continue
        p = pre[bi]
        r, k, v, kd, col = p['r'], p['k'], p['v'], p['kd'], p['col']
        wkv = wkv + yb_ref[bi]
        inv_n = 1.0 / N
        mean = _seg_sum(wkv, bd) * inv_n
        xc = wkv - mean
        var = _seg_sum(xc * xc, bd) * inv_n
        normed = xc * lax.rsqrt(var + LNX_EPS) * lnw_ref[...] + lnb_ref[...]
        a_o = _sigmoid(a0o_ref[...] + _dot(col(OFF_AB, A_LORA).astype(BF16), aupo_ref[...]))
        kd_o = k * (1.0 + (a_o - 1.0) * ka_ref[...])
        bonus = _seg_sum(r * rk_ref[...] * (kd + kd_o), bd) * v
        gg = _dot(_sigmoid(col(OFF_G, G_LORA)).astype(BF16), gup_ref[...])
        o_ref[bi] = ((normed + bonus) * gg).astype(o_ref.dtype)


def _rwkv_pass(u_r, consts, *, reverse, final, extra=()):
    b, t, _ = u_r.shape
    tb = _largest_divisor(t, (ROWS_PER_STEP, CHUNK))
    nb = _largest_divisor(b, (SEQS_PER_STEP,))
    n_steps = t // tb

    def sidx(s):
        return (n_steps - 1 - s) if reverse else s

    def full(shape):
        return pl.BlockSpec(shape, lambda bi, s: (0,) * len(shape))

    in_specs = [pl.BlockSpec((nb, tb, RWKV_COLS), lambda bi, s: (bi, sidx(s), 0))]
    args = [u_r]
    for arr in consts:
        in_specs.append(full(arr.shape))
        args.append(arr)
    if final:
        for arr in extra[:-1]:
            in_specs.append(full(arr.shape))
            args.append(arr)
        in_specs.append(pl.BlockSpec((nb, tb, RWKV_DIM), lambda bi, s: (bi, sidx(s), 0)))
        args.append(extra[-1])
    out_dtype = BF16 if final else F32
    return pl.pallas_call(
        functools.partial(_rwkv_kernel, reverse=reverse, final=final),
        grid=(b // nb, n_steps),
        in_specs=in_specs,
        out_specs=pl.BlockSpec((nb, tb, RWKV_DIM), lambda bi, s: (bi, sidx(s), 0)),
        out_shape=jax.ShapeDtypeStruct((b, t, RWKV_DIM), out_dtype),
        scratch_shapes=[pltpu.VMEM((nb, N_GROUPS, RWKV_HEAD, GROUP), F32)],
        compiler_params=_params("parallel", "arbitrary"),
    )(*args)


def _rwkv_branch(u_r, w0_f, w_up_f, w0_b, w_up_b, a0_f, a_up_f, a0_b, a_up_b,
                 g_up, k_k, k_a, r_k, ln_w, ln_b):
    t = u_r.shape[1]
    tb = _largest_divisor(t, (ROWS_PER_STEP, CHUNK))
    row = lambda a: a.reshape(1, -1).astype(F32)
    seg = np.arange(GROUP) // RWKV_HEAD
    bd = jnp.asarray(seg[:, None] == seg[None, :], BF16)
    ti = np.arange(tb)
    same_chunk = (ti[:, None] // CHUNK) == (ti[None, :] // CHUNK)
    tri_f = jnp.asarray(same_chunk & (ti[None, :] <= ti[:, None]), BF16)
    tri_b = jnp.asarray(same_chunk & (ti[None, :] >= ti[:, None]), BF16)
    ones = jnp.asarray(same_chunk, BF16)
    common = lambda w0, wup, a0, aup, tri: [row(w0), wup.astype(BF16), row(a0), aup.astype(BF16), row(k_k), row(k_a),
                                            bd, tri, ones]
    y_b = _rwkv_pass(u_r, common(w0_b, w_up_b, a0_b, a_up_b, tri_b), reverse=True, final=False)
    extra = [row(a0_b), a_up_b.astype(BF16), g_up.astype(BF16), row(r_k), row(ln_w), row(ln_b), y_b]
    return _rwkv_pass(u_r, common(w0_f, w_up_f, a0_f, a_up_f, tri_f), reverse=False, final=True, extra=extra)


def _attn_kernel(sink_ref, bias_ref, q_ref, kp_ref, kc_ref, kn_ref, vp_ref, vc_ref, vn_ref, o_ref, *, seq_len):
    i = pl.program_id(1)
    span = BLOCK + 2 * WINDOW
    kk = jnp.concatenate([kp_ref[0], kc_ref[0], kn_ref[0]], axis=0)
    vv = jnp.concatenate([vp_ref[0], vc_ref[0], vn_ref[0]], axis=0)
    key_pos = i * BLOCK - WINDOW + lax.broadcasted_iota(jnp.int32, (1, span), 1)
    in_seq = (key_pos >= 0) & (key_pos < seq_len)
    group = ATT_HEADS // ATT_KV_HEADS
    qs = q_ref[0] * (ATT_HEAD ** -0.5)
    heads = range(ATT_HEADS)
    hsl = lambda h: slice(h * ATT_HEAD, (h + 1) * ATT_HEAD)
    s = [_dot_nt(qs[:, hsl(h)], kk[:, hsl(h // group)]) for h in heads]
    s = [jnp.where(in_seq, s[h] + bias_ref[h], NEG_INF) for h in heads]
    m = [jnp.maximum(jnp.max(s[h], axis=-1, keepdims=True), sink_ref[h]) for h in heads]
    p = [jnp.exp(s[h] - m[h]) for h in heads]
    denom = [jnp.sum(p[h], axis=-1, keepdims=True) + jnp.exp(sink_ref[h] - m[h]) for h in heads]
    outs = [_dot(p[h].astype(BF16), vv[:, hsl(h // group)]) / denom[h] for h in heads]
    o_ref[0] = jnp.concatenate(outs, axis=-1).astype(o_ref.dtype)


def _attention(qkv, sink):
    b, t, _ = qkv.shape
    nb = t // BLOCK
    span = BLOCK + 2 * WINDOW
    kcol = ATT_DIM // KV_DIM
    prev = lambda i: jnp.maximum(i - 1, 0)
    nxt = lambda i: jnp.minimum(i + 1, nb - 1)
    kv_spec = lambda col, f: pl.BlockSpec((1, BLOCK, KV_DIM), lambda bi, i: (bi, f(i), col))
    same = lambda i: i
    rel = np.abs(np.arange(span)[None, :] - WINDOW - np.arange(BLOCK)[:, None])
    slopes = 2.0 ** (-8.0 * np.arange(1, ATT_HEADS + 1) / ATT_HEADS)
    bias = np.where(rel[None] <= WINDOW, -slopes[:, None, None] * rel[None], NEG_INF).astype(np.float32)
    return pl.pallas_call(
        functools.partial(_attn_kernel, seq_len=t),
        grid=(b, nb),
        in_specs=[pl.BlockSpec(memory_space=pltpu.SMEM),
                  pl.BlockSpec((ATT_HEADS, BLOCK, span), lambda bi, i: (0, 0, 0)),
                  pl.BlockSpec((1, BLOCK, ATT_DIM), lambda bi, i: (bi, i, 0)),
                  kv_spec(kcol, prev), kv_spec(kcol, same), kv_spec(kcol, nxt),
                  kv_spec(kcol + 1, prev), kv_spec(kcol + 1, same), kv_spec(kcol + 1, nxt)],
        out_specs=pl.BlockSpec((1, BLOCK, ATT_DIM), lambda bi, i: (bi, i, 0)),
        out_shape=jax.ShapeDtypeStruct((b, t, ATT_DIM), BF16),
        compiler_params=_params("parallel", "parallel"),
    )(sink.astype(F32), jnp.asarray(bias), qkv, qkv, qkv, qkv, qkv, qkv, qkv)


def _merge_kernel(yr_ref, ya_ref, gate_ref, x_ref, wpr_ref, wpa_ref, wo_ref, g_ref, wrt_ref,
                  x1_ref, h2_ref, aff_ref, afft_ref):
    d = x_ref.shape[-1]
    gl_r = gate_ref[:, 0:d].astype(F32)
    gl_a = gate_ref[:, d:2 * d].astype(F32)
    merged = (_sigmoid(gl_r) * _dot(yr_ref[...], wpr_ref[...])
              + _sigmoid(gl_a) * _dot(ya_ref[...], wpa_ref[...]))
    x1 = x_ref[...] + _dot(merged.astype(BF16), wo_ref[...])
    x1_ref[...] = x1
    h2 = (x1 * lax.rsqrt(jnp.mean(x1 * x1, axis=-1, keepdims=True) + NORM_EPS) * g_ref[...]).astype(BF16)
    h2_ref[...] = h2
    logits_t = _dot_nt(wrt_ref[...], h2)
    mx = jnp.max(logits_t, axis=0, keepdims=True)
    ex = jnp.exp(logits_t - mx)
    aff_t = ex / jnp.sum(ex, axis=0, keepdims=True)
    afft_ref[0] = aff_t
    aff_ref[...] = aff_t.T


def _merge(y_r, y_a, gates, x2, wpr, wpa, wo, g, w_router, tm, seq_len):
    n, d = x2.shape
    e = w_router.shape[1]
    tiles_per_seq = seq_len // tm
    full = lambda shape: pl.BlockSpec(shape, lambda i: (0,) * len(shape))
    rows = lambda w: pl.BlockSpec((tm, w), lambda i: (i, 0))
    return pl.pallas_call(
        _merge_kernel,
        grid=(n // tm,),
        in_specs=[rows(RWKV_DIM), rows(ATT_DIM), rows(2 * d), rows(d),
                  full(wpr.shape), full(wpa.shape), full(wo.shape), full((1, d)), full((e, d))],
        out_specs=[rows(d), rows(d), rows(e),
                   pl.BlockSpec((1, e, tm), lambda i: (i // tiles_per_seq, 0, i % tiles_per_seq))],
        out_shape=[jax.ShapeDtypeStruct((n, d), F32), jax.ShapeDtypeStruct((n, d), BF16),
                   jax.ShapeDtypeStruct((n, e), F32),
                   jax.ShapeDtypeStruct((n // seq_len, e, seq_len), F32)],
        compiler_params=_params("parallel"),
    )(y_r, y_a, gates, x2, wpr.astype(BF16), wpa.astype(BF16), wo.astype(BF16), g.reshape(1, d).astype(F32),
      w_router.T.astype(BF16))


def _select_kernel(a_ref, gsame_ref, gprev_ref, triu_ref, slot_ref, start_ref, *, cap):
    nb, rows, _ = a_ref.shape
    affs = [a_ref[i] for i in range(nb)]
    gsame = gsame_ref[...]
    gprev = gprev_ref[...]
    triu = triu_ref[...]
    capf = float(cap)

    def group_count(mask):
        per_lane = _dot(gsame, jnp.where(mask, 1.0, 0.0).astype(BF16))
        return jnp.sum(per_lane, axis=-1, keepdims=True)

    def search(i, prefixes):
        out = []
        for aff, prefix in zip(affs, prefixes):
            cand = prefix | lax.shift_left(jnp.int32(1), 30 - i)
            out.append(jnp.where(group_count(aff >= pltpu.bitcast(cand, F32)) >= capf, cand, prefix))
        return tuple(out)

    thr_bits = lax.fori_loop(0, 31, search, tuple(jnp.zeros((rows, 1), jnp.int32) for _ in range(nb)))

    def prefix_count(mask):
        mb = jnp.where(mask, 1.0, 0.0)
        inc = _dot(mb.astype(BF16), triu)
        tot = jnp.broadcast_to(inc[:, LANES - 1:LANES], inc.shape).astype(BF16)
        before = _dot(gprev, tot)
        return inc - mb + before, before

    for i, aff in enumerate(affs):
        thr = pltpu.bitcast(thr_bits[i], F32)
        gt = aff > thr
        eq = aff == thr
        need = capf - group_count(gt)
        sel = gt | (eq & (prefix_count(eq)[0] < need))
        pos, before = prefix_count(sel)
        slot_ref[i] = jnp.where(sel, pos, -1.0).astype(jnp.int32)
        start_ref[i] = before.astype(jnp.int32)


def _select(aff_t, cap):
    b, e, t = aff_t.shape
    tiles = t // LANES
    rows = e * tiles
    nb = _largest_divisor(b, (4, 2))
    rid = np.arange(rows)
    same = (rid[:, None] // tiles) == (rid[None, :] // tiles)
    gsame = jnp.asarray(same, BF16)
    gprev = jnp.asarray(same & (rid[None, :] < rid[:, None]), BF16)
    li = np.arange(LANES)
    triu = jnp.asarray(li[:, None] <= li[None, :], BF16)
    full = lambda shape: pl.BlockSpec(shape, lambda bi: (0,) * len(shape))
    slot, start = pl.pallas_call(
        functools.partial(_select_kernel, cap=cap),
        grid=(b // nb,),
        in_specs=[pl.BlockSpec((nb, rows, LANES), lambda bi: (bi, 0, 0)),
                  full((rows, rows)), full((rows, rows)), full((LANES, LANES))],
        out_specs=[pl.BlockSpec((nb, rows, LANES), lambda bi: (bi, 0, 0))] * 2,
        out_shape=[jax.ShapeDtypeStruct((b, rows, LANES), jnp.int32)] * 2,
        compiler_params=_params("parallel"),
    )(aff_t.reshape(b, rows, LANES), gsame, gprev, triu)
    return slot.reshape(b, e, t), start[:, :, 0].reshape(b, e, tiles)


ROUTE_TILE = 256
ROUTE_WIN = 64
WIN_ALIGN = 16


def _route_tables(start128, cap):
    per = ROUTE_TILE // LANES
    start = start128[:, :, ::per]
    nxt = jnp.concatenate([start[:, :, 1:], jnp.full_like(start[:, :, :1], cap)], axis=-1)
    w0 = jnp.minimum((start // WIN_ALIGN) * WIN_ALIGN, cap - ROUTE_WIN)
    fits = jnp.all(nxt - w0 <= ROUTE_WIN, axis=1)
    return w0.reshape(-1).astype(jnp.int32), fits.reshape(-1).astype(jnp.int32)


def _gather_kernel(w0_ref, fits_ref, slot_ref, h_ref, xs_ref, *, col_chunk):
    n_exp, cap, d = xs_ref.shape[1], xs_ref.shape[2], xs_ref.shape[3]
    nt = h_ref.shape[1] // ROUTE_TILE
    bi = pl.program_id(0)
    xs_ref[...] = jnp.zeros_like(xs_ref)
    ridx = lax.broadcasted_iota(jnp.int32, (ROUTE_WIN, 1), 0)
    cidx = lax.broadcasted_iota(jnp.int32, (cap, 1), 0)
    for j in range(nt):
        tok = slice(j * ROUTE_TILE, (j + 1) * ROUTE_TILE)
        fit = fits_ref[bi * nt + j]

        @pl.when(fit != 0)
        def _():
            w0 = [pl.multiple_of(w0_ref[(bi * n_exp + e) * nt + j], WIN_ALIGN) for e in range(n_exp)]
            onehot = jnp.concatenate(
                [jnp.where(slot_ref[0, e:e + 1, tok] - w0[e] == ridx, 1.0, 0.0).astype(BF16) for e in range(n_exp)],
                axis=0)
            for c0 in range(0, d, col_chunk):
                rows = _dot(onehot, h_ref[0, tok, c0:c0 + col_chunk]).astype(BF16)
                for e in range(n_exp):
                    win = (0, e, pl.ds(w0[e], ROUTE_WIN), slice(c0, c0 + col_chunk))
                    xs_ref[win] = xs_ref[win] + rows[e * ROUTE_WIN:(e + 1) * ROUTE_WIN]

        @pl.when(fit == 0)
        def _():
            for e in range(n_exp):
                onehot = jnp.where(slot_ref[0, e:e + 1, tok] == cidx, 1.0, 0.0).astype(BF16)
                xs_ref[0, e] = xs_ref[0, e] + _dot(onehot, h_ref[0, tok, :]).astype(BF16)


def _gather(w0, fits, slot, h2, cap):
    b, t, d = h2.shape
    e = slot.shape[1]
    return pl.pallas_call(
        functools.partial(_gather_kernel, col_chunk=512),
        grid_spec=pltpu.PrefetchScalarGridSpec(
            num_scalar_prefetch=2,
            grid=(b,),
            in_specs=[pl.BlockSpec((1, e, t), lambda bi, *_: (bi, 0, 0)),
                      pl.BlockSpec((1, t, d), lambda bi, *_: (bi, 0, 0))],
            out_specs=pl.BlockSpec((1, e, cap, d), lambda bi, *_: (bi, 0, 0, 0))),
        out_shape=jax.ShapeDtypeStruct((b, e, cap, d), BF16),
        compiler_params=_params("parallel"),
    )(w0, fits, slot, h2)


def _expert_kernel(xs_ref, wg_ref, wu_ref, wd_ref, ys_ref):
    bb, _, cap, d = xs_ref.shape
    xs = xs_ref[...].reshape(bb * cap, d)
    gate = _dot(xs, wg_ref[0])
    up = _dot(xs, wu_ref[0])
    hid = (gate * _sigmoid(gate) * up).astype(BF16)
    ys_ref[...] = _dot(hid, wd_ref[0]).astype(ys_ref.dtype).reshape(bb, 1, cap, d)


def _experts(xs, wg, wu, wd, bb):
    b, e, cap, d = xs.shape
    ff = wg.shape[-1]
    tok = pl.BlockSpec((bb, 1, cap, d), lambda ei, bi: (bi, ei, 0, 0))
    return pl.pallas_call(
        _expert_kernel,
        grid=(e, b // bb),
        in_specs=[tok,
                  pl.BlockSpec((1, d, ff), lambda ei, bi: (ei, 0, 0)),
                  pl.BlockSpec((1, d, ff), lambda ei, bi: (ei, 0, 0)),
                  pl.BlockSpec((1, ff, d), lambda ei, bi: (ei, 0, 0))],
        out_specs=tok,
        out_shape=jax.ShapeDtypeStruct((b, e, cap, d), BF16),
        compiler_params=_params("parallel", "arbitrary"),
    )(xs, wg, wu, wd)


def _scatter_kernel(w0_ref, fits_ref, slot_ref, aff_ref, expand_ref, ys_ref, x1_ref, g_ref, o_ref, *, final_norm):
    n_exp, cap = ys_ref.shape[1], ys_ref.shape[2]
    bi, j = pl.program_id(0), pl.program_id(1)
    nt = pl.num_programs(1)
    slot = slot_ref[0]
    aff = aff_ref[0]
    fit = fits_ref[bi * nt + j]

    def finish(acc):
        if final_norm:
            acc = acc * lax.rsqrt(jnp.mean(acc * acc, axis=-1, keepdims=True) + NORM_EPS) * g_ref[...]
        o_ref[0] = acc

    @pl.when(fit != 0)
    def _():
        w0 = [pl.multiple_of(w0_ref[(bi * n_exp + e) * nt + j], WIN_ALIGN) for e in range(n_exp)]
        lane = lax.broadcasted_iota(jnp.int32, (1, n_exp * ROUTE_WIN), 1)
        w0_row = jnp.zeros((1, n_exp * ROUTE_WIN), jnp.int32)
        for e in range(n_exp):
            w0_row = jnp.where(lane // ROUTE_WIN == e, w0[e], w0_row)
        target = (w0_row + lane % ROUTE_WIN).astype(F32)
        slot_l = _dot(slot.astype(F32).astype(BF16), expand_ref[...])
        aff_l = _dot(aff.astype(BF16), expand_ref[...])
        onehot = jnp.where(slot_l == target, aff_l, 0.0).astype(BF16)
        rows = jnp.concatenate([ys_ref[0, e, pl.ds(w0[e], ROUTE_WIN), :] for e in range(n_exp)], axis=0)
        finish(x1_ref[0] + _dot(onehot, rows))

    @pl.when(fit == 0)
    def _():
        cidx = lax.broadcasted_iota(jnp.int32, (1, cap), 1)
        acc = x1_ref[0]
        for e in range(n_exp):
            onehot = jnp.where(slot[:, e:e + 1] == cidx, aff[:, e:e + 1], 0.0).astype(BF16)
            acc = acc + _dot(onehot, ys_ref[0, e])
        finish(acc)


def _scatter(w0, fits, slot_cols, aff, ys, x1, g, final_norm):
    b, t, d = x1.shape
    e, cap = ys.shape[1], ys.shape[2]
    assert cap <= 256, "slot ids must stay exact in bf16"
    lane_exp = np.arange(e * ROUTE_WIN) // ROUTE_WIN
    expand = jnp.asarray(np.arange(e)[:, None] == lane_exp[None, :], BF16)
    return pl.pallas_call(
        functools.partial(_scatter_kernel, final_norm=final_norm),
        grid_spec=pltpu.PrefetchScalarGridSpec(
            num_scalar_prefetch=2,
            grid=(b, t // ROUTE_TILE),
            in_specs=[pl.BlockSpec((1, ROUTE_TILE, e), lambda bi, ti, *_: (bi, ti, 0)),
                      pl.BlockSpec((1, ROUTE_TILE, e), lambda bi, ti, *_: (bi, ti, 0)),
                      pl.BlockSpec((e, e * ROUTE_WIN), lambda bi, ti, *_: (0, 0)),
                      pl.BlockSpec((1, e, cap, d), lambda bi, ti, *_: (bi, 0, 0, 0)),
                      pl.BlockSpec((1, ROUTE_TILE, d), lambda bi, ti, *_: (bi, ti, 0)),
                      pl.BlockSpec((1, d), lambda bi, ti, *_: (0, 0))],
            out_specs=pl.BlockSpec((1, ROUTE_TILE, d), lambda bi, ti, *_: (bi, ti, 0))),
        out_shape=jax.ShapeDtypeStruct((b, t, d), F32),
        compiler_params=_params("parallel", "arbitrary"),
    )(w0, fits, slot_cols, aff, expand, ys, x1, g.reshape(1, d).astype(F32))


def kernel(x, norm_mix_g, w_in, mu_prev, mu_next, w0_f, w_up_f, w0_b, w_up_b, a0_f, a_up_f, a0_b, a_up_b, g_up, k_k, k_a, r_k, ln_x_w, ln_x_b, attn_sink, w_proj_rwkv, w_proj_attn, w_out, norm_ffn_g, w_router, exp_w_gate, exp_w_up, exp_w_down, norm_final_g):
    b, t, d = x.shape
    depth = w_in.shape[0]
    n = b * t
    tm = _largest_divisor(t, (512, 256, 128))
    cap = CAPACITY_FACTOR * t // N_EXPERTS
    bb = _largest_divisor(b, (4, 2))
    xc = x
    for l in range(depth):
        x2 = xc.reshape(n, d)
        u_r, qkv, gates = _in_projection(x2, norm_mix_g[l].astype(F32), w_in[l].astype(BF16), mu_prev[l], mu_next[l],
                                         tm, t)
        y_r = _rwkv_branch(u_r.reshape(b, t, RWKV_COLS), w0_f[l], w_up_f[l], w0_b[l],
                           w_up_b[l], a0_f[l], a_up_f[l], a0_b[l], a_up_b[l], g_up[l], k_k[l], k_a[l], r_k[l],
                           ln_x_w[l], ln_x_b[l])
        y_a = _attention(qkv.reshape(b, t, QKV_COLS), attn_sink[l])
        last = l == depth - 1
        x1, h2, aff, aff_t = _merge(y_r.reshape(n, RWKV_DIM), y_a.reshape(n, ATT_DIM), gates, x2, w_proj_rwkv[l],
                                    w_proj_attn[l], w_out[l], norm_ffn_g[l], w_router[l], tm, t)
        slot, start128 = _select(aff_t, cap)
        w0, fits = _route_tables(start128, cap)
        xs = _gather(w0, fits, slot, h2.reshape(b, t, d), cap)
        ys = _experts(xs, exp_w_gate[l].astype(BF16), exp_w_up[l].astype(BF16), exp_w_down[l].astype(BF16), bb)
        xc = _scatter(w0, fits, jnp.swapaxes(slot, 1, 2), aff.reshape(b, t, N_EXPERTS), ys, x1.reshape(b, t, d),
                      norm_final_g, final_norm=last)
    return xc
```

```python
import functools
import math

import numpy as np
import jax
import jax.numpy as jnp
from jax import lax
from jax.experimental import pallas as pl
from jax.experimental.pallas import tpu as pltpu

F32 = jnp.float32
BF16 = jnp.bfloat16

RWKV_HEAD = 64
RWKV_HEADS = 8
RWKV_DIM = RWKV_HEADS * RWKV_HEAD
W_LORA = 64
A_LORA = 64
G_LORA = 128
DECAY_SCALE = math.exp(-0.5)
LNX_EPS = 64e-5
ATT_HEADS = 8
ATT_KV_HEADS = 2
ATT_HEAD = 64
ATT_DIM = ATT_HEADS * ATT_HEAD
KV_DIM = ATT_KV_HEADS * ATT_HEAD
WINDOW = 128
BLOCK = 128
NEG_INF = -1e30
N_EXPERTS = 16
CAPACITY_FACTOR = 2
NORM_EPS = 1e-6
RWKV_COLS = 3 * RWKV_DIM + G_LORA + 2 * W_LORA + 2 * A_LORA
QKV_COLS = ATT_DIM + 2 * KV_DIM
OFF_R, OFF_K, OFF_V = 0, RWKV_DIM, 2 * RWKV_DIM
OFF_G = 3 * RWKV_DIM
OFF_WF = OFF_G + G_LORA
OFF_WB = OFF_WF + W_LORA
OFF_AF = OFF_WB + W_LORA
OFF_AB = OFF_AF + A_LORA

LANES = 128
VMEM_LIMIT = 48 * 1024 * 1024

CHUNK = 64


def _dot(a, b):
    return jnp.dot(a, b, preferred_element_type=F32)


def _dot_nt(a, b):
    return lax.dot_general(a, b, (((1,), (1,)), ((), ())), preferred_element_type=F32)


def _split2(x):
    hi = x.astype(BF16)
    lo = (x - hi.astype(F32)).astype(BF16)
    return hi, lo


def _seg_sum(x, bd):
    width = bd.shape[0]
    return jnp.concatenate([_dot(x[:, j:j + width].astype(BF16), bd) for j in range(0, x.shape[1], width)], axis=1)


def _sigmoid(x):
    return 0.5 * jnp.tanh(0.5 * x) + 0.5


def _largest_divisor(n, candidates):
    for c in candidates:
        if n % c == 0:
            return c
    return 1


def _params(*sem):
    return pltpu.CompilerParams(dimension_semantics=sem, vmem_limit_bytes=VMEM_LIMIT)


def _inproj_kernel(x_ref, xp_ref, xn_ref, g_ref, w_ref, mup_ref, mun_ref, ur_ref, qkv_ref, gate_ref, *,
                   col_chunk, tiles_per_seq):
    i = pl.program_id(0)
    tm = x_ref.shape[0]
    halo = xp_ref.shape[0]
    x = jnp.concatenate([xp_ref[...], x_ref[...], xn_ref[...]], axis=0)
    h = x * lax.rsqrt(jnp.mean(x * x, axis=-1, keepdims=True) + NORM_EPS) * g_ref[...]
    row = lax.broadcasted_iota(jnp.int32, (tm + 2 * halo, 1), 0)
    pos = i % tiles_per_seq
    lo = jnp.where(pos == 0, halo, 0)
    hi = jnp.where(pos == tiles_per_seq - 1, tm + halo, tm + 2 * halo)
    h = jnp.where((row < lo) | (row >= hi), 0.0, h)
    hb_all = h.astype(BF16)
    hb = h[halo:halo + tm].astype(BF16)
    for j in range(0, RWKV_COLS, col_chunk):
        w = min(col_chunk, RWKV_COLS - j)
        u = _dot(hb_all, w_ref[:, j:j + w])
        prev = pltpu.roll(u, 1, axis=0)
        nxt = pltpu.roll(u, tm + 2 * halo - 1, axis=0)
        us = u + mup_ref[:, j:j + w] * (prev - u) + mun_ref[:, j:j + w] * (nxt - u)
        ur_ref[:, j:j + w] = us[halo:halo + tm].astype(BF16)
    c0 = RWKV_COLS
    for ref in (qkv_ref, gate_ref):
        width = ref.shape[-1]
        for j in range(0, width, col_chunk):
            w = min(col_chunk, width - j)
            ref[:, j:j + w] = _dot(hb, w_ref[:, c0 + j:c0 + j + w]).astype(BF16)
        c0 += width


def _in_projection(x2, g, w_in_bf, mu_prev, mu_next, tm, seq_len):
    n, d = x2.shape
    cols = w_in_bf.shape[1]
    gate_cols = cols - RWKV_COLS - QKV_COLS
    halo = 8
    hb = tm // halo
    n_halo = n // halo
    return pl.pallas_call(
        functools.partial(_inproj_kernel, col_chunk=512, tiles_per_seq=seq_len // tm),
        grid=(n // tm,),
        in_specs=[pl.BlockSpec((tm, d), lambda i: (i, 0)),
                  pl.BlockSpec((halo, d), lambda i: (jnp.maximum(i * hb - 1, 0), 0)),
                  pl.BlockSpec((halo, d), lambda i: (jnp.minimum((i + 1) * hb, n_halo - 1), 0)),
                  pl.BlockSpec((1, d), lambda i: (0, 0)),
                  pl.BlockSpec((d, cols), lambda i: (0, 0)),
                  pl.BlockSpec((1, RWKV_COLS), lambda i: (0, 0)),
                  pl.BlockSpec((1, RWKV_COLS), lambda i: (0, 0))],
        out_specs=[pl.BlockSpec((tm, RWKV_COLS), lambda i: (i, 0)),
                   pl.BlockSpec((tm, QKV_COLS), lambda i: (i, 0)),
                   pl.BlockSpec((tm, gate_cols), lambda i: (i, 0))],
        out_shape=[jax.ShapeDtypeStruct((n, RWKV_COLS), BF16),
                   jax.ShapeDtypeStruct((n, QKV_COLS), BF16),
                   jax.ShapeDtypeStruct((n, gate_cols), BF16)],
        compiler_params=_params("parallel"),
    )(x2, x2, x2, g.reshape(1, d), w_in_bf, mu_prev.reshape(1, -1).astype(F32), mu_next.reshape(1, -1).astype(F32))


GROUP_HEADS = 4
GROUP = GROUP_HEADS * RWKV_HEAD
N_GROUPS = RWKV_HEADS // GROUP_HEADS
ROWS_PER_STEP = 256
SEQS_PER_STEP = 2


def _rwkv_kernel(*refs, reverse, final):
    if final:
        (u_ref, w0_ref, wup_ref, a0_ref, aup_ref, kk_ref, ka_ref,
         bd_ref, tri_ref, a0o_ref, aupo_ref, gup_ref, rk_ref, lnw_ref, lnb_ref, yb_ref,
         o_ref, st_ref) = refs
    else:
        (u_ref, w0_ref, wup_ref, a0_ref, aup_ref, kk_ref, ka_ref,
         bd_ref, tri_ref, o_ref, st_ref) = refs
    NB, TB = u_ref.shape[0], u_ref.shape[1]
    L = CHUNK
    N = RWKV_HEAD
    n_ck = TB // L
    step = pl.program_id(1)

    @pl.when(step == 0)
    def _():
        st_ref[...] = jnp.zeros_like(st_ref)

    bd = bd_ref[...]
    tri = tri_ref[...]

    def prep(bi):
        col = lambda off, width: u_ref[bi, :, off:off + width].astype(F32)
        r = col(OFF_R, RWKV_DIM)
        k = col(OFF_K, RWKV_DIM)
        v = col(OFF_V, RWKV_DIM)
        w_lo = col(OFF_WB if reverse else OFF_WF, W_LORA)
        a_lo = col(OFF_AB if reverse else OFF_AF, A_LORA)
        lw = -DECAY_SCALE * _sigmoid(w0_ref[...] + _dot(jnp.tanh(w_lo).astype(BF16), wup_ref[...]))
        a = _sigmoid(a0_ref[...] + _dot(a_lo.astype(BF16), aup_ref[...]))
        kkr = k * kk_ref[...]
        kk = kkr * lax.rsqrt(jnp.maximum(_seg_sum(kkr * kkr, bd), 1e-24))
        kd = k * (1.0 + (a - 1.0) * ka_ref[...])
        b = a * kk
        l_hi, l_lo = _split2(lw)
        cs = _dot(tri, l_hi) + _dot(tri, l_lo)
        edge = 0 if reverse else L - 1
        e_tot = [jnp.exp(cs[c * L + edge:c * L + edge + 1]) for c in range(n_ck)]
        e_neg = jnp.exp(-cs)
        e_rem = jnp.concatenate([e_neg[c * L:(c + 1) * L] * e_tot[c] for c in range(n_ck)], axis=0)
        return dict(r=r, k=k, v=v, kd=kd, col=col, e_tot=e_tot,
                    a_t=(kk * jnp.exp(cs - lw)).astype(BF16), r_t=(r * jnp.exp(cs)).astype(BF16),
                    k_t=(kd * e_neg).astype(BF16), b_t=(b * e_neg).astype(BF16),
                    k_h=(kd * e_rem).astype(BF16), b_hn=(-b * e_rem).astype(BF16))

    pre = [prep(bi) for bi in range(NB)]

    ii = lax.broadcasted_iota(jnp.int32, (GROUP, GROUP), 0)
    jj = lax.broadcasted_iota(jnp.int32, (GROUP, GROUP), 1)
    same_head = (ii // N) == (jj // N)
    ti = lax.broadcasted_iota(jnp.int32, (L, GROUP), 0)
    si = lax.broadcasted_iota(jnp.int32, (L, GROUP), 1) % N
    if reverse:
        strict = si > ti
        incl = si >= ti
    else:
        strict = si < ti
        incl = si <= ti
    eye = (si == ti).astype(F32)
    level_masks = []
    s = 1
    while s < L:
        same = (ti // (2 * s)) == (si // (2 * s))
        t_hi = (ti & s) != 0
        s_hi = (si & s) != 0
        if reverse:
            level_masks.append(same & jnp.logical_not(t_hi) & s_hi)
        else:
            level_masks.append(same & t_hi & jnp.logical_not(s_hi))
        s *= 2

    def bdiag(xp):
        return jnp.where(same_head, jnp.concatenate([xp.astype(BF16)] * GROUP_HEADS, axis=0), 0.0)

    def head_transpose(xp):
        return jnp.concatenate([xp[:, h * N:(h + 1) * N].T for h in range(GROUP_HEADS)], axis=1)

    order = list(range(n_ck))[::-1] if reverse else list(range(n_ck))
    cgs = [(bi, c, g) for bi in range(NB) for c in order for g in range(N_GROUPS)]

    def tile(name, cg):
        bi, c, g = cg
        return pre[bi][name][c * L:(c + 1) * L, g * GROUP:(g + 1) * GROUP]

    ar = {cg: jnp.concatenate([tile('a_t', cg), tile('r_t', cg)], axis=0) for cg in cgs}
    arb = {cg: _dot_nt(ar[cg], bdiag(tile('b_t', cg))) for cg in cgs}
    ark = {cg: _dot_nt(ar[cg], bdiag(tile('k_t', cg))) for cg in cgs}
    ab = {cg: arb[cg][0:L] for cg in cgs}
    rb = {cg: arb[cg][L:2 * L] for cg in cgs}
    ak = {cg: ark[cg][0:L] for cg in cgs}
    rk_ = {cg: ark[cg][L:2 * L] for cg in cgs}
    x = {cg: eye - jnp.where(level_masks[0], ab[cg], 0.0) for cg in cgs}
    for m in level_masks[1:]:
        xn = {cg: _dot(x[cg].astype(BF16), bdiag(jnp.where(m, ab[cg], 0.0))).astype(BF16) for cg in cgs}
        x = {cg: x[cg] - _dot(xn[cg], bdiag(x[cg])) for cg in cgs}
    xb = {cg: x[cg].astype(BF16) for cg in cgs}
    xa = {cg: _dot(xb[cg], bdiag(tile('a_t', cg))) for cg in cgs}
    xa_w = {cg: bdiag(xa[cg]) for cg in cgs}
    mrb_n = {cg: jnp.where(incl, -rb[cg], 0.0).astype(BF16) for cg in cgs}
    v_w = {cg: bdiag(tile('v', cg)) for cg in cgs}
    mv = {cg: _dot(jnp.concatenate([jnp.where(strict, ak[cg], 0.0).astype(BF16),
                                    jnp.where(incl, rk_[cg], 0.0).astype(BF16)], axis=0), v_w[cg]) for cg in cgs}
    xmv = {cg: _dot(xb[cg], bdiag(mv[cg][0:L])) for cg in cgs}
    rq = {cg: (tile('r_t', cg).astype(F32) + _dot(mrb_n[cg], xa_w[cg])).astype(BF16) for cg in cgs}
    xmv_w = {cg: bdiag(xmv[cg]) for cg in cgs}
    dd = {cg: mv[cg][L:2 * L] + _dot(mrb_n[cg], xmv_w[cg]) for cg in cgs}
    eye_b = eye.astype(BF16)
    xt = {cg: jnp.concatenate([_dot_nt(eye_b, xa_w[cg]), _dot_nt(eye_b, xmv_w[cg])], axis=0).astype(BF16) for cg in cgs}
    wb = {cg: _dot(xt[cg], bdiag(tile('b_hn', cg))) for cg in cgs}
    cc = {cg: _dot(head_transpose(tile('v', cg)).astype(BF16), bdiag(tile('k_h', cg))) + wb[cg][L:2 * L] for cg in cgs}
    w_w = {cg: bdiag(wb[cg][0:L]) for cg in cgs}

    state = {(bi, g): st_ref[bi, g] for bi in range(NB) for g in range(N_GROUPS)}
    y_tiles = {}
    for c in order:
        for bi in range(NB):
            for g in range(N_GROUPS):
                cg = (bi, c, g)
                sb = state[bi, g].astype(BF16)
                decay = pre[bi]['e_tot'][c][:, g * GROUP:(g + 1) * GROUP]
                state[bi, g] = state[bi, g] * decay + _dot(sb, w_w[cg]) + cc[cg]
                y_tiles[cg] = _dot_nt(rq[cg], bdiag(sb)) + dd[cg]
    for key, val in state.items():
        st_ref[key] = val

    for bi in range(NB):
        wkv = jnp.concatenate(
            [jnp.concatenate([y_tiles[(bi, c, g)] for g in range(N_GROUPS)], axis=1) for c in range(n_ck)], axis=0)
        if not final:
            o_ref[bi] = wkv
            continue
        p = pre[bi]
        r, k, v, kd, col = p['r'], p['k'], p['v'], p['kd'], p['col']
        wkv = wkv + yb_ref[bi]
        inv_n = 1.0 / N
        mean = _seg_sum(wkv, bd) * inv_n
        xc = wkv - mean
        var = _seg_sum(xc * xc, bd) * inv_n
        normed = xc * lax.rsqrt(var + LNX_EPS) * lnw_ref[...] + lnb_ref[...]
        a_o = _sigmoid(a0o_ref[...] + _dot(col(OFF_AB, A_LORA).astype(BF16), aupo_ref[...]))
        kd_o = k * (1.0 + (a_o - 1.0) * ka_ref[...])
        bonus = _seg_sum(r * rk_ref[...] * (kd + kd_o), bd) * v
        gg = _dot(_sigmoid(col(OFF_G, G_LORA)).astype(BF16), gup_ref[...])
        o_ref[bi] = ((normed + bonus) * gg).astype(o_ref.dtype)


def _rwkv_pass(u_r, consts, *, reverse, final, extra=()):
    b, t, _ = u_r.shape
    tb = _largest_divisor(t, (ROWS_PER_STEP, CHUNK))
    nb = _largest_divisor(b, (SEQS_PER_STEP,))
    n_steps = t // tb

    def sidx(s):
        return (n_steps - 1 - s) if reverse else s

    def full(shape):
        return pl.BlockSpec(shape, lambda bi, s: (0,) * len(shape))

    in_specs = [pl.BlockSpec((nb, tb, RWKV_COLS), lambda bi, s: (bi, sidx(s), 0))]
    args = [u_r]
    for arr in consts:
        in_specs.append(full(arr.shape))
        args.append(arr)
    if final:
        for arr in extra[:-1]:
            in_specs.append(full(arr.shape))
            args.append(arr)
        in_specs.append(pl.BlockSpec((nb, tb, RWKV_DIM), lambda bi, s: (bi, sidx(s), 0)))
        args.append(extra[-1])
    out_dtype = BF16 if final else F32
    return pl.pallas_call(
        functools.partial(_rwkv_kernel, reverse=reverse, final=final),
        grid=(b // nb, n_steps),
        in_specs=in_specs,
        out_specs=pl.BlockSpec((nb, tb, RWKV_DIM), lambda bi, s: (bi, sidx(s), 0)),
        out_shape=jax.ShapeDtypeStruct((b, t, RWKV_DIM), out_dtype),
        scratch_shapes=[pltpu.VMEM((nb, N_GROUPS, RWKV_HEAD, GROUP), F32)],
        compiler_params=_params("parallel", "arbitrary"),
    )(*args)


def _rwkv_branch(u_r, w0_f, w_up_f, w0_b, w_up_b, a0_f, a_up_f, a0_b, a_up_b,
                 g_up, k_k, k_a, r_k, ln_w, ln_b):
    t = u_r.shape[1]
    tb = _largest_divisor(t, (ROWS_PER_STEP, CHUNK))
    row = lambda a: a.reshape(1, -1).astype(F32)
    seg = np.arange(GROUP) // RWKV_HEAD
    bd = jnp.asarray(seg[:, None] == seg[None, :], BF16)
    ti = np.arange(tb)
    same_chunk = (ti[:, None] // CHUNK) == (ti[None, :] // CHUNK)
    tri_f = jnp.asarray(same_chunk & (ti[None, :] <= ti[:, None]), BF16)
    tri_b = jnp.asarray(same_chunk & (ti[None, :] >= ti[:, None]), BF16)
    common = lambda w0, wup, a0, aup, tri: [row(w0), wup.astype(BF16), row(a0), aup.astype(BF16), row(k_k), row(k_a),
                                            bd, tri]
    y_b = _rwkv_pass(u_r, common(w0_b, w_up_b, a0_b, a_up_b, tri_b), reverse=True, final=False)
    extra = [row(a0_b), a_up_b.astype(BF16), g_up.astype(BF16), row(r_k), row(ln_w), row(ln_b), y_b]
    return _rwkv_pass(u_r, common(w0_f, w_up_f, a0_f, a_up_f, tri_f), reverse=False, final=True, extra=extra)


def _attn_kernel(sink_ref, bias_ref, q_ref, kp_ref, kc_ref, kn_ref, vp_ref, vc_ref, vn_ref, o_ref):
    kk = jnp.concatenate([kp_ref[0], kc_ref[0], kn_ref[0]], axis=0)
    vv = jnp.concatenate([vp_ref[0], vc_ref[0], vn_ref[0]], axis=0)
    group = ATT_HEADS // ATT_KV_HEADS
    kv_lane = lax.broadcasted_iota(jnp.int32, (1, KV_DIM), 1) // ATT_HEAD
    v_ext = [jnp.where(kv_lane == kv, vv, 1.0) for kv in range(ATT_KV_HEADS)]
    qs = q_ref[0] * (ATT_HEAD ** -0.5)
    heads = range(ATT_HEADS)
    hsl = lambda h: slice(h * ATT_HEAD, (h + 1) * ATT_HEAD)
    s = [_dot_nt(qs[:, hsl(h)], kk[:, hsl(h // group)]) + bias_ref[0, h] for h in heads]
    m = [jnp.maximum(jnp.max(s[h], axis=-1, keepdims=True), sink_ref[h]) for h in heads]
    p = [jnp.exp(s[h] - m[h]).astype(BF16) for h in heads]
    pv = [_dot(p[h], v_ext[h // group]) for h in heads]
    outs = []
    for h in heads:
        kv = h // group
        other = (kv + 1) % ATT_KV_HEADS
        denom = pv[h][:, other * ATT_HEAD:other * ATT_HEAD + 1] + jnp.exp(sink_ref[h] - m[h])
        outs.append(pv[h][:, hsl(kv)] / denom)
    o_ref[0] = jnp.concatenate(outs, axis=-1).astype(o_ref.dtype)


def _attention(qkv, sink):
    b, t, _ = qkv.shape
    nb = t // BLOCK
    span = BLOCK + 2 * WINDOW
    kcol = ATT_DIM // KV_DIM
    prev = lambda i: jnp.maximum(i - 1, 0)
    nxt = lambda i: jnp.minimum(i + 1, nb - 1)
    kv_spec = lambda col, f: pl.BlockSpec((1, BLOCK, KV_DIM), lambda bi, i: (bi, f(i), col))
    same = lambda i: i
    col = np.arange(span)[None, :]
    rel = np.abs(col - WINDOW - np.arange(BLOCK)[:, None])
    slopes = 2.0 ** (-8.0 * np.arange(1, ATT_HEADS + 1) / ATT_HEADS)
    base = np.where(rel[None] <= WINDOW, -slopes[:, None, None] * rel[None], NEG_INF)
    tables = []
    for variant in range(4):
        dead = ((col < WINDOW) & bool(variant & 1)) | ((col >= WINDOW + BLOCK) & bool(variant & 2))
        tables.append(np.where(dead[None], NEG_INF, base))
    bias = jnp.asarray(np.stack(tables).astype(np.float32))
    variant_of = lambda i: jnp.where(i == 0, 1, 0) + jnp.where(i == nb - 1, 2, 0)
    return pl.pallas_call(
        _attn_kernel,
        grid=(b, nb),
        in_specs=[pl.BlockSpec(memory_space=pltpu.SMEM),
                  pl.BlockSpec((1, ATT_HEADS, BLOCK, span), lambda bi, i: (variant_of(i), 0, 0, 0)),
                  pl.BlockSpec((1, BLOCK, ATT_DIM), lambda bi, i: (bi, i, 0)),
                  kv_spec(kcol, prev), kv_spec(kcol, same), kv_spec(kcol, nxt),
                  kv_spec(kcol + 1, prev), kv_spec(kcol + 1, same), kv_spec(kcol + 1, nxt)],
        out_specs=pl.BlockSpec((1, BLOCK, ATT_DIM), lambda bi, i: (bi, i, 0)),
        out_shape=jax.ShapeDtypeStruct((b, t, ATT_DIM), BF16),
        compiler_params=_params("parallel", "parallel"),
    )(sink.astype(F32), bias, qkv, qkv, qkv, qkv, qkv, qkv, qkv)


def _merge_kernel(yr_ref, ya_ref, gate_ref, x_ref, wpr_ref, wpa_ref, wo_ref, g_ref, wrt_ref,
                  x1_ref, h2_ref, aff_ref, afft_ref):
    d = x_ref.shape[-1]
    gl_r = gate_ref[:, 0:d].astype(F32)
    gl_a = gate_ref[:, d:2 * d].astype(F32)
    merged = (_sigmoid(gl_r) * _dot(yr_ref[...], wpr_ref[...])
              + _sigmoid(gl_a) * _dot(ya_ref[...], wpa_ref[...]))
    x1 = x_ref[...] + _dot(merged.astype(BF16), wo_ref[...])
    x1_ref[...] = x1
    h2 = (x1 * lax.rsqrt(jnp.mean(x1 * x1, axis=-1, keepdims=True) + NORM_EPS) * g_ref[...]).astype(BF16)
    h2_ref[...] = h2
    logits_t = _dot_nt(wrt_ref[...], h2)
    mx = jnp.max(logits_t, axis=0, keepdims=True)
    ex = jnp.exp(logits_t - mx)
    aff_t = ex / jnp.sum(ex, axis=0, keepdims=True)
    afft_ref[0] = aff_t
    aff_ref[...] = aff_t.T


def _merge(y_r, y_a, gates, x2, wpr, wpa, wo, g, w_router, tm, seq_len):
    n, d = x2.shape
    e = w_router.shape[1]
    tiles_per_seq = seq_len // tm
    full = lambda shape: pl.BlockSpec(shape, lambda i: (0,) * len(shape))
    rows = lambda w: pl.BlockSpec((tm, w), lambda i: (i, 0))
    return pl.pallas_call(
        _merge_kernel,
        grid=(n // tm,),
        in_specs=[rows(RWKV_DIM), rows(ATT_DIM), rows(2 * d), rows(d),
                  full(wpr.shape), full(wpa.shape), full(wo.shape), full((1, d)), full((e, d))],
        out_specs=[rows(d), rows(d), rows(e),
                   pl.BlockSpec((1, e, tm), lambda i: (i // tiles_per_seq, 0, i % tiles_per_seq))],
        out_shape=[jax.ShapeDtypeStruct((n, d), F32), jax.ShapeDtypeStruct((n, d), BF16),
                   jax.ShapeDtypeStruct((n, e), F32),
                   jax.ShapeDtypeStruct((n // seq_len, e, seq_len), F32)],
        compiler_params=_params("parallel"),
    )(y_r, y_a, gates, x2, wpr.astype(BF16), wpa.astype(BF16), wo.astype(BF16), g.reshape(1, d).astype(F32),
      w_router.T.astype(BF16))


def _select_kernel(a_ref, gsame_ref, gprev_ref, triu_ref, slot_ref, start_ref, *, cap):
    nb, rows, _ = a_ref.shape
    affs = [a_ref[i] for i in range(nb)]
    gsame = gsame_ref[...]
    gprev = gprev_ref[...]
    triu = triu_ref[...]
    capf = float(cap)

    def group_count(mask):
        per_lane = _dot(gsame, jnp.where(mask, 1.0, 0.0).astype(BF16))
        return jnp.sum(per_lane, axis=-1, keepdims=True)

    def search(i, prefixes):
        out = []
        for aff, prefix in zip(affs, prefixes):
            cand = prefix | lax.shift_left(jnp.int32(1), 30 - i)
            out.append(jnp.where(group_count(aff >= pltpu.bitcast(cand, F32)) >= capf, cand, prefix))
        return tuple(out)

    thr_bits = lax.fori_loop(0, 31, search, tuple(jnp.zeros((rows, 1), jnp.int32) for _ in range(nb)))

    def prefix_count(mask):
        mb = jnp.where(mask, 1.0, 0.0)
        inc = _dot(mb.astype(BF16), triu)
        tot = jnp.broadcast_to(inc[:, LANES - 1:LANES], inc.shape).astype(BF16)
        before = _dot(gprev, tot)
        return inc - mb + before, before

    for i, aff in enumerate(affs):
        thr = pltpu.bitcast(thr_bits[i], F32)
        gt = aff > thr
        eq = aff == thr
        need = capf - group_count(gt)
        sel = gt | (eq & (prefix_count(eq)[0] < need))
        pos, before = prefix_count(sel)
        slot_ref[i] = jnp.where(sel, pos, -1.0).astype(jnp.int32)
        start_ref[i] = before.astype(jnp.int32)


def _select(aff_t, cap):
    b, e, t = aff_t.shape
    tiles = t // LANES
    rows = e * tiles
    nb = _largest_divisor(b, (4, 2))
    rid = np.arange(rows)
    same = (rid[:, None] // tiles) == (rid[None, :] // tiles)
    gsame = jnp.asarray(same, BF16)
    gprev = jnp.asarray(same & (rid[None, :] < rid[:, None]), BF16)
    li = np.arange(LANES)
    triu = jnp.asarray(li[:, None] <= li[None, :], BF16)
    full = lambda shape: pl.BlockSpec(shape, lambda bi: (0,) * len(shape))
    slot, start = pl.pallas_call(
        functools.partial(_select_kernel, cap=cap),
        grid=(b // nb,),
        in_specs=[pl.BlockSpec((nb, rows, LANES), lambda bi: (bi, 0, 0)),
                  full((rows, rows)), full((rows, rows)), full((LANES, LANES))],
        out_specs=[pl.BlockSpec((nb, rows, LANES), lambda bi: (bi, 0, 0))] * 2,
        out_shape=[jax.ShapeDtypeStruct((b, rows, LANES), jnp.int32)] * 2,
        compiler_params=_params("parallel"),
    )(aff_t.reshape(b, rows, LANES), gsame, gprev, triu)
    return slot.reshape(b, e, t), start[:, :, 0].reshape(b, e, tiles)


ROUTE_TILE = 256
ROUTE_WIN = 64
WIN_ALIGN = 16


def _route_tables(start128, cap):
    per = ROUTE_TILE // LANES
    start = start128[:, :, ::per]
    nxt = jnp.concatenate([start[:, :, 1:], jnp.full_like(start[:, :, :1], cap)], axis=-1)
    w0 = jnp.minimum((start // WIN_ALIGN) * WIN_ALIGN, cap - ROUTE_WIN)
    fits = jnp.all(nxt - w0 <= ROUTE_WIN, axis=1)
    return w0.reshape(-1).astype(jnp.int32), fits.reshape(-1).astype(jnp.int32)


def _gather_kernel(w0_ref, fits_ref, slot_ref, h_ref, xs_ref, *, col_chunk):
    n_exp, cap, d = xs_ref.shape[1], xs_ref.shape[2], xs_ref.shape[3]
    nt = h_ref.shape[1] // ROUTE_TILE
    bi = pl.program_id(0)
    xs_ref[...] = jnp.zeros_like(xs_ref)
    ridx = lax.broadcasted_iota(jnp.int32, (ROUTE_WIN, 1), 0)
    cidx = lax.broadcasted_iota(jnp.int32, (cap, 1), 0)
    for j in range(nt):
        tok = slice(j * ROUTE_TILE, (j + 1) * ROUTE_TILE)
        fit = fits_ref[bi * nt + j]

        @pl.when(fit != 0)
        def _():
            w0 = [pl.multiple_of(w0_ref[(bi * n_exp + e) * nt + j], WIN_ALIGN) for e in range(n_exp)]
            onehot = jnp.concatenate(
                [jnp.where(slot_ref[0, e:e + 1, tok] - w0[e] == ridx, 1.0, 0.0).astype(BF16) for e in range(n_exp)],
                axis=0)
            for c0 in range(0, d, col_chunk):
                rows = _dot(onehot, h_ref[0, tok, c0:c0 + col_chunk]).astype(BF16)
                for e in range(n_exp):
                    win = (0, e, pl.ds(w0[e], ROUTE_WIN), slice(c0, c0 + col_chunk))
                    xs_ref[win] = xs_ref[win] + rows[e * ROUTE_WIN:(e + 1) * ROUTE_WIN]

        @pl.when(fit == 0)
        def _():
            for e in range(n_exp):
                onehot = jnp.where(slot_ref[0, e:e + 1, tok] == cidx, 1.0, 0.0).astype(BF16)
                xs_ref[0, e] = xs_ref[0, e] + _dot(onehot, h_ref[0, tok, :]).astype(BF16)


def _gather(w0, fits, slot, h2, cap):
    b, t, d = h2.shape
    e = slot.shape[1]
    return pl.pallas_call(
        functools.partial(_gather_kernel, col_chunk=512),
        grid_spec=pltpu.PrefetchScalarGridSpec(
            num_scalar_prefetch=2,
            grid=(b,),
            in_specs=[pl.BlockSpec((1, e, t), lambda bi, *_: (bi, 0, 0)),
                      pl.BlockSpec((1, t, d), lambda bi, *_: (bi, 0, 0))],
            out_specs=pl.BlockSpec((1, e, cap, d), lambda bi, *_: (bi, 0, 0, 0))),
        out_shape=jax.ShapeDtypeStruct((b, e, cap, d), BF16),
        compiler_params=_params("parallel"),
    )(w0, fits, slot, h2)


def _expert_kernel(xs_ref, wg_ref, wu_ref, wd_ref, ys_ref):
    bb, _, cap, d = xs_ref.shape
    xs = xs_ref[...].reshape(bb * cap, d)
    gate = _dot(xs, wg_ref[0])
    up = _dot(xs, wu_ref[0])
    hid = (gate * _sigmoid(gate) * up).astype(BF16)
    ys_ref[...] = _dot(hid, wd_ref[0]).astype(ys_ref.dtype).reshape(bb, 1, cap, d)


def _experts(xs, wg, wu, wd, bb):
    b, e, cap, d = xs.shape
    ff = wg.shape[-1]
    tok = pl.BlockSpec((bb, 1, cap, d), lambda ei, bi: (bi, ei, 0, 0))
    return pl.pallas_call(
        _expert_kernel,
        grid=(e, b // bb),
        in_specs=[tok,
                  pl.BlockSpec((1, d, ff), lambda ei, bi: (ei, 0, 0)),
                  pl.BlockSpec((1, d, ff), lambda ei, bi: (ei, 0, 0)),
                  pl.BlockSpec((1, ff, d), lambda ei, bi: (ei, 0, 0))],
        out_specs=tok,
        out_shape=jax.ShapeDtypeStruct((b, e, cap, d), BF16),
        compiler_params=_params("parallel", "arbitrary"),
    )(xs, wg, wu, wd)


def _scatter_kernel(w0_ref, fits_ref, slot_ref, aff_ref, expand_ref, ys_ref, x1_ref, g_ref, o_ref, *, final_norm):
    n_exp, cap = ys_ref.shape[1], ys_ref.shape[2]
    bi, j = pl.program_id(0), pl.program_id(1)
    nt = pl.num_programs(1)
    slot = slot_ref[0]
    aff = aff_ref[0]
    fit = fits_ref[bi * nt + j]

    def finish(acc):
        if final_norm:
            acc = acc * lax.rsqrt(jnp.mean(acc * acc, axis=-1, keepdims=True) + NORM_EPS) * g_ref[...]
        o_ref[0] = acc

    @pl.when(fit != 0)
    def _():
        w0 = [pl.multiple_of(w0_ref[(bi * n_exp + e) * nt + j], WIN_ALIGN) for e in range(n_exp)]
        lane = lax.broadcasted_iota(jnp.int32, (1, n_exp * ROUTE_WIN), 1)
        w0_row = jnp.zeros((1, n_exp * ROUTE_WIN), jnp.int32)
        for e in range(n_exp):
            w0_row = jnp.where(lane // ROUTE_WIN == e, w0[e], w0_row)
        target = (w0_row + lane % ROUTE_WIN).astype(F32)
        slot_l = _dot(slot.astype(F32).astype(BF16), expand_ref[...])
        aff_l = _dot(aff.astype(BF16), expand_ref[...])
        onehot = jnp.where(slot_l == target, aff_l, 0.0).astype(BF16)
        rows = jnp.concatenate([ys_ref[0, e, pl.ds(w0[e], ROUTE_WIN), :] for e in range(n_exp)], axis=0)
        finish(x1_ref[0] + _dot(onehot, rows))

    @pl.when(fit == 0)
    def _():
        cidx = lax.broadcasted_iota(jnp.int32, (1, cap), 1)
        acc = x1_ref[0]
        for e in range(n_exp):
            onehot = jnp.where(slot[:, e:e + 1] == cidx, aff[:, e:e + 1], 0.0).astype(BF16)
            acc = acc + _dot(onehot, ys_ref[0, e])
        finish(acc)


def _scatter(w0, fits, slot_cols, aff, ys, x1, g, final_norm):
    b, t, d = x1.shape
    e, cap = ys.shape[1], ys.shape[2]
    assert cap <= 256, "slot ids must stay exact in bf16"
    lane_exp = np.arange(e * ROUTE_WIN) // ROUTE_WIN
    expand = jnp.asarray(np.arange(e)[:, None] == lane_exp[None, :], BF16)
    return pl.pallas_call(
        functools.partial(_scatter_kernel, final_norm=final_norm),
        grid_spec=pltpu.PrefetchScalarGridSpec(
            num_scalar_prefetch=2,
            grid=(b, t // ROUTE_TILE),
            in_specs=[pl.BlockSpec((1, ROUTE_TILE, e), lambda bi, ti, *_: (bi, ti, 0)),
                      pl.BlockSpec((1, ROUTE_TILE, e), lambda bi, ti, *_: (bi, ti, 0)),
                      pl.BlockSpec((e, e * ROUTE_WIN), lambda bi, ti, *_: (0, 0)),
                      pl.BlockSpec((1, e, cap, d), lambda bi, ti, *_: (bi, 0, 0, 0)),
                      pl.BlockSpec((1, ROUTE_TILE, d), lambda bi, ti, *_: (bi, ti, 0)),
                      pl.BlockSpec((1, d), lambda bi, ti, *_: (0, 0))],
            out_specs=pl.BlockSpec((1, ROUTE_TILE, d), lambda bi, ti, *_: (bi, ti, 0))),
        out_shape=jax.ShapeDtypeStruct((b, t, d), F32),
        compiler_params=_params("parallel", "arbitrary"),
    )(w0, fits, slot_cols, aff, expand, ys, x1, g.reshape(1, d).astype(F32))


def kernel(x, norm_mix_g, w_in, mu_prev, mu_next, w0_f, w_up_f, w0_b, w_up_b, a0_f, a_up_f, a0_b, a_up_b, g_up, k_k, k_a, r_k, ln_x_w, ln_x_b, attn_sink, w_proj_rwkv, w_proj_attn, w_out, norm_ffn_g, w_router, exp_w_gate, exp_w_up, exp_w_down, norm_final_g):
    b, t, d = x.shape
    depth = w_in.shape[0]
    n = b * t
    tm = _largest_divisor(t, (512, 256, 128))
    cap = CAPACITY_FACTOR * t // N_EXPERTS
    bb = _largest_divisor(b, (4, 2))
    xc = x
    for l in range(depth):
        x2 = xc.reshape(n, d)
        u_r, qkv, gates = _in_projection(x2, norm_mix_g[l].astype(F32), w_in[l].astype(BF16), mu_prev[l], mu_next[l],
                                         tm, t)
        y_r = _rwkv_branch(u_r.reshape(b, t, RWKV_COLS), w0_f[l], w_up_f[l], w0_b[l],
                           w_up_b[l], a0_f[l], a_up_f[l], a0_b[l], a_up_b[l], g_up[l], k_k[l], k_a[l], r_k[l],
                           ln_x_w[l], ln_x_b[l])
        y_a = _attention(qkv.reshape(b, t, QKV_COLS), attn_sink[l])
        last = l == depth - 1
        x1, h2, aff, aff_t = _merge(y_r.reshape(n, RWKV_DIM), y_a.reshape(n, ATT_DIM), gates, x2, w_proj_rwkv[l],
                                    w_proj_attn[l], w_out[l], norm_ffn_g[l], w_router[l], tm, t)
        slot, start128 = _select(aff_t, cap)
        w0, fits = _route_tables(start128, cap)
        xs = _gather(w0, fits, slot, h2.reshape(b, t, d), cap)
        ys = _experts(xs, exp_w_gate[l].astype(BF16), exp_w_up[l].astype(BF16), exp_w_down[l].astype(BF16), bb)
        xc = _scatter(w0, fits, jnp.swapaxes(slot, 1, 2), aff.reshape(b, t, N_EXPERTS), ys, x1.reshape(b, t, d),
                      norm_final_g, final_norm=last)
    return xc
```

```python
import functools
import math

import numpy as np
import jax
import jax.numpy as jnp
from jax import lax
from jax.experimental import pallas as pl
from jax.experimental.pallas import tpu as pltpu

F32 = jnp.float32
BF16 = jnp.bfloat16

RWKV_HEAD = 64
RWKV_HEADS = 8
RWKV_DIM = RWKV_HEADS * RWKV_HEAD
W_LORA = 64
A_LORA = 64
G_LORA = 128
DECAY_SCALE = math.exp(-0.5)
LNX_EPS = 64e-5
ATT_HEADS = 8
ATT_KV_HEADS = 2
ATT_HEAD = 64
ATT_DIM = ATT_HEADS * ATT_HEAD
KV_DIM = ATT_KV_HEADS * ATT_HEAD
WINDOW = 128
BLOCK = 128
NEG_INF = -1e30
N_EXPERTS = 16
CAPACITY_FACTOR = 2
NORM_EPS = 1e-6
RWKV_COLS = 3 * RWKV_DIM + G_LORA + 2 * W_LORA + 2 * A_LORA
QKV_COLS = ATT_DIM + 2 * KV_DIM
OFF_R, OFF_K, OFF_V = 0, RWKV_DIM, 2 * RWKV_DIM
OFF_G = 3 * RWKV_DIM
OFF_WF = OFF_G + G_LORA
OFF_WB = OFF_WF + W_LORA
OFF_AF = OFF_WB + W_LORA
OFF_AB = OFF_AF + A_LORA

LANES = 128
VMEM_LIMIT = 48 * 1024 * 1024

CHUNK = 64


def _dot(a, b):
    return jnp.dot(a, b, preferred_element_type=F32)


def _dot_nt(a, b):
    return lax.dot_general(a, b, (((1,), (1,)), ((), ())), preferred_element_type=F32)


def _split2(x):
    hi = x.astype(BF16)
    lo = (x - hi.astype(F32)).astype(BF16)
    return hi, lo


def _seg_sum(x, bd):
    width = bd.shape[0]
    return jnp.concatenate([_dot(x[:, j:j + width].astype(BF16), bd) for j in range(0, x.shape[1], width)], axis=1)


def _sigmoid(x):
    return 0.5 * jnp.tanh(0.5 * x) + 0.5


def _largest_divisor(n, candidates):
    for c in candidates:
        if n % c == 0:
            return c
    return 1


def _params(*sem):
    return pltpu.CompilerParams(dimension_semantics=sem, vmem_limit_bytes=VMEM_LIMIT)


def _inproj_kernel(x_ref, xp_ref, xn_ref, g_ref, w_ref, mup_ref, mun_ref, ur_ref, qkv_ref, gate_ref, *,
                   col_chunk, tiles_per_seq):
    i = pl.program_id(0)
    tm = x_ref.shape[0]
    halo = xp_ref.shape[0]
    x = jnp.concatenate([xp_ref[...], x_ref[...], xn_ref[...]], axis=0)
    h = x * lax.rsqrt(jnp.mean(x * x, axis=-1, keepdims=True) + NORM_EPS) * g_ref[...]
    row = lax.broadcasted_iota(jnp.int32, (tm + 2 * halo, 1), 0)
    pos = i % tiles_per_seq
    lo = jnp.where(pos == 0, halo, 0)
    hi = jnp.where(pos == tiles_per_seq - 1, tm + halo, tm + 2 * halo)
    h = jnp.where((row < lo) | (row >= hi), 0.0, h)
    hb_all = h.astype(BF16)
    hb = h[halo:halo + tm].astype(BF16)
    for j in range(0, RWKV_COLS, col_chunk):
        w = min(col_chunk, RWKV_COLS - j)
        u = _dot(hb_all, w_ref[:, j:j + w])
        prev = pltpu.roll(u, 1, axis=0)
        nxt = pltpu.roll(u, tm + 2 * halo - 1, axis=0)
        us = u + mup_ref[:, j:j + w] * (prev - u) + mun_ref[:, j:j + w] * (nxt - u)
        ur_ref[:, j:j + w] = us[halo:halo + tm].astype(BF16)
    c0 = RWKV_COLS
    for ref in (qkv_ref, gate_ref):
        width = ref.shape[-1]
        for j in range(0, width, col_chunk):
            w = min(col_chunk, width - j)
            ref[:, j:j + w] = _dot(hb, w_ref[:, c0 + j:c0 + j + w]).astype(BF16)
        c0 += width


def _in_projection(x2, g, w_in_bf, mu_prev, mu_next, tm, seq_len):
    n, d = x2.shape
    cols = w_in_bf.shape[1]
    gate_cols = cols - RWKV_COLS - QKV_COLS
    halo = 8
    hb = tm // halo
    n_halo = n // halo
    return pl.pallas_call(
        functools.partial(_inproj_kernel, col_chunk=512, tiles_per_seq=seq_len // tm),
        grid=(n // tm,),
        in_specs=[pl.BlockSpec((tm, d), lambda i: (i, 0)),
                  pl.BlockSpec((halo, d), lambda i: (jnp.maximum(i * hb - 1, 0), 0)),
                  pl.BlockSpec((halo, d), lambda i: (jnp.minimum((i + 1) * hb, n_halo - 1), 0)),
                  pl.BlockSpec((1, d), lambda i: (0, 0)),
                  pl.BlockSpec((d, cols), lambda i: (0, 0)),
                  pl.BlockSpec((1, RWKV_COLS), lambda i: (0, 0)),
                  pl.BlockSpec((1, RWKV_COLS), lambda i: (0, 0))],
        out_specs=[pl.BlockSpec((tm, RWKV_COLS), lambda i: (i, 0)),
                   pl.BlockSpec((tm, QKV_COLS), lambda i: (i, 0)),
                   pl.BlockSpec((tm, gate_cols), lambda i: (i, 0))],
        out_shape=[jax.ShapeDtypeStruct((n, RWKV_COLS), BF16),
                   jax.ShapeDtypeStruct((n, QKV_COLS), BF16),
                   jax.ShapeDtypeStruct((n, gate_cols), BF16)],
        compiler_params=_params("parallel"),
    )(x2, x2, x2, g.reshape(1, d), w_in_bf, mu_prev.reshape(1, -1).astype(F32), mu_next.reshape(1, -1).astype(F32))


GROUP_HEADS = 4
GROUP = GROUP_HEADS * RWKV_HEAD
N_GROUPS = RWKV_HEADS // GROUP_HEADS
ROWS_PER_STEP = 256
SEQS_PER_STEP = 2


def _rwkv_kernel(*refs, reverse, final):
    if final:
        (u_ref, w0_ref, wup_ref, a0_ref, aup_ref, kk_ref, ka_ref,
         bd_ref, tri_ref, a0o_ref, aupo_ref, gup_ref, rk_ref, lnw_ref, lnb_ref, yb_ref,
         o_ref, st_ref) = refs
    else:
        (u_ref, w0_ref, wup_ref, a0_ref, aup_ref, kk_ref, ka_ref,
         bd_ref, tri_ref, o_ref, st_ref) = refs
    NB, TB = u_ref.shape[0], u_ref.shape[1]
    L = CHUNK
    N = RWKV_HEAD
    n_ck = TB // L
    step = pl.program_id(1)

    @pl.when(step == 0)
    def _():
        st_ref[...] = jnp.zeros_like(st_ref)

    bd = bd_ref[...]
    tri = tri_ref[...]

    def prep(bi):
        col = lambda off, width: u_ref[bi, :, off:off + width].astype(F32)
        r = col(OFF_R, RWKV_DIM)
        k = col(OFF_K, RWKV_DIM)
        v = col(OFF_V, RWKV_DIM)
        w_lo = col(OFF_WB if reverse else OFF_WF, W_LORA)
        a_lo = col(OFF_AB if reverse else OFF_AF, A_LORA)
        lw = -DECAY_SCALE * _sigmoid(w0_ref[...] + _dot(jnp.tanh(w_lo).astype(BF16), wup_ref[...]))
        a = _sigmoid(a0_ref[...] + _dot(a_lo.astype(BF16), aup_ref[...]))
        kkr = k * kk_ref[...]
        kk = kkr * lax.rsqrt(jnp.maximum(_seg_sum(kkr * kkr, bd), 1e-24))
        kd = k * (1.0 + (a - 1.0) * ka_ref[...])
        b = a * kk
        l_hi, l_lo = _split2(lw)
        cs = _dot(tri, l_hi) + _dot(tri, l_lo)
        edge = 0 if reverse else L - 1
        e_tot = [jnp.exp(cs[c * L + edge:c * L + edge + 1]) for c in range(n_ck)]
        e_neg = jnp.exp(-cs)
        e_rem = jnp.concatenate([e_neg[c * L:(c + 1) * L] * e_tot[c] for c in range(n_ck)], axis=0)
        return dict(r=r, k=k, v=v, kd=kd, col=col, e_tot=e_tot,
                    a_t=(kk * jnp.exp(cs - lw)).astype(BF16), r_t=(r * jnp.exp(cs)).astype(BF16),
                    k_t=(kd * e_neg).astype(BF16), b_t=(b * e_neg).astype(BF16),
                    k_h=(kd * e_rem).astype(BF16), b_hn=(-b * e_rem).astype(BF16))

    pre = [prep(bi) for bi in range(NB)]

    ii = lax.broadcasted_iota(jnp.int32, (GROUP, GROUP), 0)
    jj = lax.broadcasted_iota(jnp.int32, (GROUP, GROUP), 1)
    same_head = (ii // N) == (jj // N)
    ti = lax.broadcasted_iota(jnp.int32, (L, GROUP), 0)
    si = lax.broadcasted_iota(jnp.int32, (L, GROUP), 1) % N
    if reverse:
        strict = si > ti
        incl = si >= ti
    else:
        strict = si < ti
        incl = si <= ti
    eye = (si == ti).astype(F32)
    level_masks = []
    s = 1
    while s < L:
        same = (ti // (2 * s)) == (si // (2 * s))
        t_hi = (ti & s) != 0
        s_hi = (si & s) != 0
        if reverse:
            level_masks.append(same & jnp.logical_not(t_hi) & s_hi)
        else:
            level_masks.append(same & t_hi & jnp.logical_not(s_hi))
        s *= 2

    def bdiag(xp):
        return jnp.where(same_head, jnp.concatenate([xp.astype(BF16)] * GROUP_HEADS, axis=0), 0.0)

    def head_transpose(xp):
        return jnp.concatenate([xp[:, h * N:(h + 1) * N].T for h in range(GROUP_HEADS)], axis=1)

    order = list(range(n_ck))[::-1] if reverse else list(range(n_ck))
    cgs = [(bi, c, g) for bi in range(NB) for c in order for g in range(N_GROUPS)]

    def tile(name, cg):
        bi, c, g = cg
        return pre[bi][name][c * L:(c + 1) * L, g * GROUP:(g + 1) * GROUP]

    ar = {cg: jnp.concatenate([tile('a_t', cg), tile('r_t', cg)], axis=0) for cg in cgs}
    arb = {cg: _dot_nt(ar[cg], bdiag(tile('b_t', cg))) for cg in cgs}
    ark = {cg: _dot_nt(ar[cg], bdiag(tile('k_t', cg))) for cg in cgs}
    ab = {cg: arb[cg][0:L] for cg in cgs}
    rb = {cg: arb[cg][L:2 * L] for cg in cgs}
    ak = {cg: ark[cg][0:L] for cg in cgs}
    rk_ = {cg: ark[cg][L:2 * L] for cg in cgs}
    x = {cg: eye - jnp.where(level_masks[0], ab[cg], 0.0) for cg in cgs}
    for m in level_masks[1:]:
        xn = {cg: _dot(x[cg].astype(BF16), bdiag(jnp.where(m, ab[cg], 0.0))).astype(BF16) for cg in cgs}
        x = {cg: x[cg] - _dot(xn[cg], bdiag(x[cg])) for cg in cgs}
    xb = {cg: x[cg].astype(BF16) for cg in cgs}
    xa = {cg: _dot(xb[cg], bdiag(tile('a_t', cg))) for cg in cgs}
    xa_w = {cg: bdiag(xa[cg]) for cg in cgs}
    mrb_n = {cg: jnp.where(incl, -rb[cg], 0.0).astype(BF16) for cg in cgs}
    v_w = {cg: bdiag(tile('v', cg)) for cg in cgs}
    mv = {cg: _dot(jnp.concatenate([jnp.where(strict, ak[cg], 0.0).astype(BF16),
                                    jnp.where(incl, rk_[cg], 0.0).astype(BF16)], axis=0), v_w[cg]) for cg in cgs}
    xmv = {cg: _dot(xb[cg], bdiag(mv[cg][0:L])) for cg in cgs}
    rq = {cg: (tile('r_t', cg).astype(F32) + _dot(mrb_n[cg], xa_w[cg])).astype(BF16) for cg in cgs}
    xmv_w = {cg: bdiag(xmv[cg]) for cg in cgs}
    dd = {cg: mv[cg][L:2 * L] + _dot(mrb_n[cg], xmv_w[cg]) for cg in cgs}
    eye_b = eye.astype(BF16)
    xt = {cg: jnp.concatenate([_dot_nt(eye_b, xa_w[cg]), _dot_nt(eye_b, xmv_w[cg])], axis=0).astype(BF16) for cg in cgs}
    wb = {cg: _dot(xt[cg], bdiag(tile('b_hn', cg))) for cg in cgs}
    cc = {cg: _dot(head_transpose(tile('v', cg)).astype(BF16), bdiag(tile('k_h', cg))) + wb[cg][L:2 * L] for cg in cgs}
    w_w = {cg: bdiag(wb[cg][0:L]) for cg in cgs}

    state = {(bi, g): st_ref[bi, g] for bi in range(NB) for g in range(N_GROUPS)}
    y_tiles = {}
    for c in order:
        for bi in range(NB):
            for g in range(N_GROUPS):
                cg = (bi, c, g)
                sb = state[bi, g].astype(BF16)
                decay = pre[bi]['e_tot'][c][:, g * GROUP:(g + 1) * GROUP]
                state[bi, g] = state[bi, g] * decay + _dot(sb, w_w[cg]) + cc[cg]
                y_tiles[cg] = _dot_nt(rq[cg], bdiag(sb)) + dd[cg]
    for key, val in state.items():
        st_ref[key] = val

    for bi in range(NB):
        wkv = jnp.concatenate(
            [jnp.concatenate([y_tiles[(bi, c, g)] for g in range(N_GROUPS)], axis=1) for c in range(n_ck)], axis=0)
        if not final:
            o_ref[bi] = wkv
            continue
        p = pre[bi]
        r, k, v, kd, col = p['r'], p['k'], p['v'], p['kd'], p['col']
        wkv = wkv + yb_ref[bi]
        inv_n = 1.0 / N
        mean = _seg_sum(wkv, bd) * inv_n
        xc = wkv - mean
        var = _seg_sum(xc * xc, bd) * inv_n
        normed = xc * lax.rsqrt(var + LNX_EPS) * lnw_ref[...] + lnb_ref[...]
        a_o = _sigmoid(a0o_ref[...] + _dot(col(OFF_AB, A_LORA).astype(BF16), aupo_ref[...]))
        kd_o = k * (1.0 + (a_o - 1.0) * ka_ref[...])
        bonus = _seg_sum(r * rk_ref[...] * (kd + kd_o), bd) * v
        gg = _dot(_sigmoid(col(OFF_G, G_LORA)).astype(BF16), gup_ref[...])
        o_ref[bi] = ((normed + bonus) * gg).astype(o_ref.dtype)


def _rwkv_pass(u_r, consts, *, reverse, final, extra=()):
    b, t, _ = u_r.shape
    tb = _largest_divisor(t, (ROWS_PER_STEP, CHUNK))
    nb = _largest_divisor(b, (SEQS_PER_STEP,))
    n_steps = t // tb

    def sidx(s):
        return (n_steps - 1 - s) if reverse else s

    def full(shape):
        return pl.BlockSpec(shape, lambda bi, s: (0,) * len(shape))

    in_specs = [pl.BlockSpec((nb, tb, RWKV_COLS), lambda bi, s: (bi, sidx(s), 0))]
    args = [u_r]
    for arr in consts:
        in_specs.append(full(arr.shape))
        args.append(arr)
    if final:
        for arr in extra[:-1]:
            in_specs.append(full(arr.shape))
            args.append(arr)
        in_specs.append(pl.BlockSpec((nb, tb, RWKV_DIM), lambda bi, s: (bi, sidx(s), 0)))
        args.append(extra[-1])
    out_dtype = BF16 if final else F32
    return pl.pallas_call(
        functools.partial(_rwkv_kernel, reverse=reverse, final=final),
        grid=(b // nb, n_steps),
        in_specs=in_specs,
        out_specs=pl.BlockSpec((nb, tb, RWKV_DIM), lambda bi, s: (bi, sidx(s), 0)),
        out_shape=jax.ShapeDtypeStruct((b, t, RWKV_DIM), out_dtype),
        scratch_shapes=[pltpu.VMEM((nb, N_GROUPS, RWKV_HEAD, GROUP), F32)],
        compiler_params=_params("parallel", "arbitrary"),
    )(*args)


def _rwkv_branch(u_r, w0_f, w_up_f, w0_b, w_up_b, a0_f, a_up_f, a0_b, a_up_b,
                 g_up, k_k, k_a, r_k, ln_w, ln_b):
    t = u_r.shape[1]
    tb = _largest_divisor(t, (ROWS_PER_STEP, CHUNK))
    row = lambda a: a.reshape(1, -1).astype(F32)
    seg = np.arange(GROUP) // RWKV_HEAD
    bd = jnp.asarray(seg[:, None] == seg[None, :], BF16)
    ti = np.arange(tb)
    same_chunk = (ti[:, None] // CHUNK) == (ti[None, :] // CHUNK)
    tri_f = jnp.asarray(same_chunk & (ti[None, :] <= ti[:, None]), BF16)
    tri_b = jnp.asarray(same_chunk & (ti[None, :] >= ti[:, None]), BF16)
    common = lambda w0, wup, a0, aup, tri: [row(w0), wup.astype(BF16), row(a0), aup.astype(BF16), row(k_k), row(k_a),
                                            bd, tri]
    y_b = _rwkv_pass(u_r, common(w0_b, w_up_b, a0_b, a_up_b, tri_b), reverse=True, final=False)
    extra = [row(a0_b), a_up_b.astype(BF16), g_up.astype(BF16), row(r_k), row(ln_w), row(ln_b), y_b]
    return _rwkv_pass(u_r, common(w0_f, w_up_f, a0_f, a_up_f, tri_f), reverse=False, final=True, extra=extra)


Q_BLOCKS = 2


def _attn_kernel(sink_ref, *refs):
    bias_refs = refs[:Q_BLOCKS]
    q_ref, kp_ref, kc_ref, kn_ref, vp_ref, vc_ref, vn_ref, o_ref = refs[Q_BLOCKS:]
    span = BLOCK + 2 * WINDOW
    kk = jnp.concatenate([kp_ref[0], kc_ref[0], kn_ref[0]], axis=0)
    vv = jnp.concatenate([vp_ref[0], vc_ref[0], vn_ref[0]], axis=0)
    group = ATT_HEADS // ATT_KV_HEADS
    kv_lane = lax.broadcasted_iota(jnp.int32, (1, KV_DIM), 1) // ATT_HEAD
    v_ext = [jnp.where(kv_lane == kv, vv, 1.0) for kv in range(ATT_KV_HEADS)]
    qs = q_ref[0] * (ATT_HEAD ** -0.5)
    hsl = lambda h: slice(h * ATT_HEAD, (h + 1) * ATT_HEAD)
    qrows = lambda j: slice(j * BLOCK, (j + 1) * BLOCK)
    krows = lambda j: slice(j * BLOCK, j * BLOCK + span)
    jh = [(j, h) for j in range(Q_BLOCKS) for h in range(ATT_HEADS)]
    s = {c: _dot_nt(qs[qrows(c[0]), hsl(c[1])], kk[krows(c[0]), hsl(c[1] // group)]) + bias_refs[c[0]][0, c[1]]
         for c in jh}
    m = {c: jnp.maximum(jnp.max(s[c], axis=-1, keepdims=True), sink_ref[c[1]]) for c in jh}
    p = {c: jnp.exp(s[c] - m[c]).astype(BF16) for c in jh}
    pv = {c: _dot(p[c], v_ext[c[1] // group][krows(c[0])]) for c in jh}
    for j in range(Q_BLOCKS):
        outs = []
        for h in range(ATT_HEADS):
            kv = h // group
            other = (kv + 1) % ATT_KV_HEADS
            denom = pv[j, h][:, other * ATT_HEAD:other * ATT_HEAD + 1] + jnp.exp(sink_ref[h] - m[j, h])
            outs.append(pv[j, h][:, hsl(kv)] / denom)
        o_ref[0, qrows(j)] = jnp.concatenate(outs, axis=-1).astype(o_ref.dtype)


def _attention(qkv, sink):
    b, t, _ = qkv.shape
    nb = t // BLOCK
    assert nb % Q_BLOCKS == 0
    n_steps = nb // Q_BLOCKS
    rows = Q_BLOCKS * BLOCK
    span = BLOCK + 2 * WINDOW
    kcol = ATT_DIM // KV_DIM
    halo = lambda col, f: pl.BlockSpec((1, BLOCK, KV_DIM), lambda bi, i: (bi, f(i), col))
    prev = lambda i: jnp.maximum(i * Q_BLOCKS - 1, 0)
    nxt = lambda i: jnp.minimum((i + 1) * Q_BLOCKS, nb - 1)
    cur = lambda col: pl.BlockSpec((1, rows, KV_DIM), lambda bi, i: (bi, i, col))
    col = np.arange(span)[None, :]
    rel = np.abs(col - WINDOW - np.arange(BLOCK)[:, None])
    slopes = 2.0 ** (-8.0 * np.arange(1, ATT_HEADS + 1) / ATT_HEADS)
    base = np.where(rel[None] <= WINDOW, -slopes[:, None, None] * rel[None], NEG_INF)
    tables = []
    for variant in range(4):
        dead = ((col < WINDOW) & bool(variant & 1)) | ((col >= WINDOW + BLOCK) & bool(variant & 2))
        tables.append(np.where(dead[None], NEG_INF, base))
    bias = jnp.asarray(np.stack(tables).astype(np.float32))

    def bias_spec(j):
        variant = lambda i: (jnp.where(i * Q_BLOCKS + j == 0, 1, 0) + jnp.where(i * Q_BLOCKS + j == nb - 1, 2, 0))
        return pl.BlockSpec((1, ATT_HEADS, BLOCK, span), lambda bi, i: (variant(i), 0, 0, 0))

    return pl.pallas_call(
        _attn_kernel,
        grid=(b, n_steps),
        in_specs=[pl.BlockSpec(memory_space=pltpu.SMEM)] + [bias_spec(j) for j in range(Q_BLOCKS)]
        + [pl.BlockSpec((1, rows, ATT_DIM), lambda bi, i: (bi, i, 0)),
           halo(kcol, prev), cur(kcol), halo(kcol, nxt),
           halo(kcol + 1, prev), cur(kcol + 1), halo(kcol + 1, nxt)],
        out_specs=pl.BlockSpec((1, rows, ATT_DIM), lambda bi, i: (bi, i, 0)),
        out_shape=jax.ShapeDtypeStruct((b, t, ATT_DIM), BF16),
        compiler_params=_params("parallel", "parallel"),
    )(sink.astype(F32), *([bias] * Q_BLOCKS), qkv, qkv, qkv, qkv, qkv, qkv, qkv)


MERGE_SPLIT = 2


def _merge_kernel(yr_ref, ya_ref, gate_ref, x_ref, wpr_ref, wpa_ref, wo_ref, g_ref, wrt_ref,
                  x1_ref, h2_ref, aff_ref, afft_ref):
    tm, d = x_ref.shape
    part = tm // MERGE_SPLIT
    for j in range(MERGE_SPLIT):
        rs = slice(j * part, (j + 1) * part)
        gl_r = gate_ref[rs, 0:d].astype(F32)
        gl_a = gate_ref[rs, d:2 * d].astype(F32)
        merged = (_sigmoid(gl_r) * _dot(yr_ref[rs, :], wpr_ref[...])
                  + _sigmoid(gl_a) * _dot(ya_ref[rs, :], wpa_ref[...]))
        x1 = x_ref[rs, :] + _dot(merged.astype(BF16), wo_ref[...])
        x1_ref[rs, :] = x1
        h2 = (x1 * lax.rsqrt(jnp.mean(x1 * x1, axis=-1, keepdims=True) + NORM_EPS) * g_ref[...]).astype(BF16)
        h2_ref[rs, :] = h2
        logits_t = _dot_nt(wrt_ref[...], h2)
        mx = jnp.max(logits_t, axis=0, keepdims=True)
        ex = jnp.exp(logits_t - mx)
        aff_t = ex / jnp.sum(ex, axis=0, keepdims=True)
        afft_ref[0, :, rs] = aff_t
        aff_ref[rs, :] = aff_t.T


def _merge(y_r, y_a, gates, x2, wpr, wpa, wo, g, w_router, tm, seq_len):
    n, d = x2.shape
    e = w_router.shape[1]
    tiles_per_seq = seq_len // tm
    full = lambda shape: pl.BlockSpec(shape, lambda i: (0,) * len(shape))
    rows = lambda w: pl.BlockSpec((tm, w), lambda i: (i, 0))
    return pl.pallas_call(
        _merge_kernel,
        grid=(n // tm,),
        in_specs=[rows(RWKV_DIM), rows(ATT_DIM), rows(2 * d), rows(d),
                  full(wpr.shape), full(wpa.shape), full(wo.shape), full((1, d)), full((e, d))],
        out_specs=[rows(d), rows(d), rows(e),
                   pl.BlockSpec((1, e, tm), lambda i: (i // tiles_per_seq, 0, i % tiles_per_seq))],
        out_shape=[jax.ShapeDtypeStruct((n, d), F32), jax.ShapeDtypeStruct((n, d), BF16),
                   jax.ShapeDtypeStruct((n, e), F32),
                   jax.ShapeDtypeStruct((n // seq_len, e, seq_len), F32)],
        compiler_params=_params("parallel"),
    )(y_r, y_a, gates, x2, wpr.astype(BF16), wpa.astype(BF16), wo.astype(BF16), g.reshape(1, d).astype(F32),
      w_router.T.astype(BF16))


def _select_kernel(a_ref, gsame_ref, gprev_ref, triu_ref, slot_ref, start_ref, *, cap):
    nb, rows, _ = a_ref.shape
    affs = [a_ref[i] for i in range(nb)]
    gsame = gsame_ref[...]
    gprev = gprev_ref[...]
    triu = triu_ref[...]
    capf = float(cap)

    def group_count(mask):
        per_lane = _dot(gsame, jnp.where(mask, 1.0, 0.0).astype(BF16))
        return jnp.sum(per_lane, axis=-1, keepdims=True)

    def search(i, prefixes):
        out = []
        for aff, prefix in zip(affs, prefixes):
            cand = prefix | lax.shift_left(jnp.int32(1), 30 - i)
            out.append(jnp.where(group_count(aff >= pltpu.bitcast(cand, F32)) >= capf, cand, prefix))
        return tuple(out)

    thr_bits = lax.fori_loop(0, 31, search, tuple(jnp.zeros((rows, 1), jnp.int32) for _ in range(nb)))

    def prefix_count(mask):
        mb = jnp.where(mask, 1.0, 0.0)
        inc = _dot(mb.astype(BF16), triu)
        tot = jnp.broadcast_to(inc[:, LANES - 1:LANES], inc.shape).astype(BF16)
        before = _dot(gprev, tot)
        return inc - mb + before, before

    for i, aff in enumerate(affs):
        thr = pltpu.bitcast(thr_bits[i], F32)
        gt = aff > thr
        eq = aff == thr
        need = capf - group_count(gt)
        sel = gt | (eq & (prefix_count(eq)[0] < need))
        pos, before = prefix_count(sel)
        slot_ref[i] = jnp.where(sel, pos, -1.0).astype(jnp.int32)
        start_ref[i] = before.astype(jnp.int32)


def _select(aff_t, cap):
    b, e, t = aff_t.shape
    tiles = t // LANES
    rows = e * tiles
    nb = _largest_divisor(b, (4, 2))
    rid = np.arange(rows)
    same = (rid[:, None] // tiles) == (rid[None, :] // tiles)
    gsame = jnp.asarray(same, BF16)
    gprev = jnp.asarray(same & (rid[None, :] < rid[:, None]), BF16)
    li = np.arange(LANES)
    triu = jnp.asarray(li[:, None] <= li[None, :], BF16)
    full = lambda shape: pl.BlockSpec(shape, lambda bi: (0,) * len(shape))
    slot, start = pl.pallas_call(
        functools.partial(_select_kernel, cap=cap),
        grid=(b // nb,),
        in_specs=[pl.BlockSpec((nb, rows, LANES), lambda bi: (bi, 0, 0)),
                  full((rows, rows)), full((rows, rows)), full((LANES, LANES))],
        out_specs=[pl.BlockSpec((nb, rows, LANES), lambda bi: (bi, 0, 0))] * 2,
        out_shape=[jax.ShapeDtypeStruct((b, rows, LANES), jnp.int32)] * 2,
        compiler_params=_params("parallel"),
    )(aff_t.reshape(b, rows, LANES), gsame, gprev, triu)
    return slot.reshape(b, e, t), start[:, :, 0].reshape(b, e, tiles)


ROUTE_TILE = 256
ROUTE_WIN = 64
WIN_ALIGN = 16


def _route_tables(start128, cap):
    per = ROUTE_TILE // LANES
    start = start128[:, :, ::per]
    nxt = jnp.concatenate([start[:, :, 1:], jnp.full_like(start[:, :, :1], cap)], axis=-1)
    w0 = jnp.minimum((start // WIN_ALIGN) * WIN_ALIGN, cap - ROUTE_WIN)
    fits = jnp.all(nxt - w0 <= ROUTE_WIN, axis=1)
    return w0.reshape(-1).astype(jnp.int32), fits.reshape(-1).astype(jnp.int32)


def _gather_kernel(w0_ref, fits_ref, slot_ref, h_ref, xs_ref, *, col_chunk):
    n_exp, cap, d = xs_ref.shape[1], xs_ref.shape[2], xs_ref.shape[3]
    nt = h_ref.shape[1] // ROUTE_TILE
    bi = pl.program_id(0)
    xs_ref[...] = jnp.zeros_like(xs_ref)
    ridx = lax.broadcasted_iota(jnp.int32, (ROUTE_WIN, 1), 0)
    cidx = lax.broadcasted_iota(jnp.int32, (cap, 1), 0)
    for j in range(nt):
        tok = slice(j * ROUTE_TILE, (j + 1) * ROUTE_TILE)
        fit = fits_ref[bi * nt + j]

        @pl.when(fit != 0)
        def _():
            w0 = [pl.multiple_of(w0_ref[(bi * n_exp + e) * nt + j], WIN_ALIGN) for e in range(n_exp)]
            onehot = jnp.concatenate(
                [jnp.where(slot_ref[0, e:e + 1, tok] - w0[e] == ridx, 1.0, 0.0).astype(BF16) for e in range(n_exp)],
                axis=0)
            for c0 in range(0, d, col_chunk):
                rows = _dot(onehot, h_ref[0, tok, c0:c0 + col_chunk]).astype(BF16)
                for e in range(n_exp):
                    win = (0, e, pl.ds(w0[e], ROUTE_WIN), slice(c0, c0 + col_chunk))
                    xs_ref[win] = xs_ref[win] + rows[e * ROUTE_WIN:(e + 1) * ROUTE_WIN]

        @pl.when(fit == 0)
        def _():
            for e in range(n_exp):
                onehot = jnp.where(slot_ref[0, e:e + 1, tok] == cidx, 1.0, 0.0).astype(BF16)
                xs_ref[0, e] = xs_ref[0, e] + _dot(onehot, h_ref[0, tok, :]).astype(BF16)


def _gather(w0, fits, slot, h2, cap):
    b, t, d = h2.shape
    e = slot.shape[1]
    return pl.pallas_call(
        functools.partial(_gather_kernel, col_chunk=512),
        grid_spec=pltpu.PrefetchScalarGridSpec(
            num_scalar_prefetch=2,
            grid=(b,),
            in_specs=[pl.BlockSpec((1, e, t), lambda bi, *_: (bi, 0, 0)),
                      pl.BlockSpec((1, t, d), lambda bi, *_: (bi, 0, 0))],
            out_specs=pl.BlockSpec((1, e, cap, d), lambda bi, *_: (bi, 0, 0, 0))),
        out_shape=jax.ShapeDtypeStruct((b, e, cap, d), BF16),
        compiler_params=_params("parallel"),
    )(w0, fits, slot, h2)


def _expert_kernel(xs_ref, wg_ref, wu_ref, wd_ref, ys_ref):
    bb, _, cap, d = xs_ref.shape
    xs = xs_ref[...].reshape(bb * cap, d)
    gate = _dot(xs, wg_ref[0])
    up = _dot(xs, wu_ref[0])
    hid = (gate * _sigmoid(gate) * up).astype(BF16)
    ys_ref[...] = _dot(hid, wd_ref[0]).astype(ys_ref.dtype).reshape(bb, 1, cap, d)


def _experts(xs, wg, wu, wd, bb):
    b, e, cap, d = xs.shape
    ff = wg.shape[-1]
    tok = pl.BlockSpec((bb, 1, cap, d), lambda ei, bi: (bi, ei, 0, 0))
    return pl.pallas_call(
        _expert_kernel,
        grid=(e, b // bb),
        in_specs=[tok,
                  pl.BlockSpec((1, d, ff), lambda ei, bi: (ei, 0, 0)),
                  pl.BlockSpec((1, d, ff), lambda ei, bi: (ei, 0, 0)),
                  pl.BlockSpec((1, ff, d), lambda ei, bi: (ei, 0, 0))],
        out_specs=tok,
        out_shape=jax.ShapeDtypeStruct((b, e, cap, d), BF16),
        compiler_params=_params("parallel", "arbitrary"),
    )(xs, wg, wu, wd)


def _scatter_kernel(w0_ref, fits_ref, slot_ref, aff_ref, expand_ref, ys_ref, x1_ref, g_ref, o_ref, *, final_norm):
    n_exp, cap = ys_ref.shape[1], ys_ref.shape[2]
    bi, j = pl.program_id(0), pl.program_id(1)
    nt = pl.num_programs(1)
    slot = slot_ref[0]
    aff = aff_ref[0]
    fit = fits_ref[bi * nt + j]

    def finish(acc):
        if final_norm:
            acc = acc * lax.rsqrt(jnp.mean(acc * acc, axis=-1, keepdims=True) + NORM_EPS) * g_ref[...]
        o_ref[0] = acc

    @pl.when(fit != 0)
    def _():
        w0 = [pl.multiple_of(w0_ref[(bi * n_exp + e) * nt + j], WIN_ALIGN) for e in range(n_exp)]
        lane = lax.broadcasted_iota(jnp.int32, (1, n_exp * ROUTE_WIN), 1)
        w0_row = jnp.zeros((1, n_exp * ROUTE_WIN), jnp.int32)
        for e in range(n_exp):
            w0_row = jnp.where(lane // ROUTE_WIN == e, w0[e], w0_row)
        target = (w0_row + lane % ROUTE_WIN).astype(F32)
        slot_l = _dot(slot.astype(F32).astype(BF16), expand_ref[...])
        aff_l = _dot(aff.astype(BF16), expand_ref[...])
        onehot = jnp.where(slot_l == target, aff_l, 0.0).astype(BF16)
        rows = jnp.concatenate([ys_ref[0, e, pl.ds(w0[e], ROUTE_WIN), :] for e in range(n_exp)], axis=0)
        finish(x1_ref[0] + _dot(onehot, rows))

    @pl.when(fit == 0)
    def _():
        cidx = lax.broadcasted_iota(jnp.int32, (1, cap), 1)
        acc = x1_ref[0]
        for e in range(n_exp):
            onehot = jnp.where(slot[:, e:e + 1] == cidx, aff[:, e:e + 1], 0.0).astype(BF16)
            acc = acc + _dot(onehot, ys_ref[0, e])
        finish(acc)


def _scatter(w0, fits, slot_cols, aff, ys, x1, g, final_norm):
    b, t, d = x1.shape
    e, cap = ys.shape[1], ys.shape[2]
    assert cap <= 256, "slot ids must stay exact in bf16"
    lane_exp = np.arange(e * ROUTE_WIN) // ROUTE_WIN
    expand = jnp.asarray(np.arange(e)[:, None] == lane_exp[None, :], BF16)
    return pl.pallas_call(
        functools.partial(_scatter_kernel, final_norm=final_norm),
        grid_spec=pltpu.PrefetchScalarGridSpec(
            num_scalar_prefetch=2,
            grid=(b, t // ROUTE_TILE),
            in_specs=[pl.BlockSpec((1, ROUTE_TILE, e), lambda bi, ti, *_: (bi, ti, 0)),
                      pl.BlockSpec((1, ROUTE_TILE, e), lambda bi, ti, *_: (bi, ti, 0)),
                      pl.BlockSpec((e, e * ROUTE_WIN), lambda bi, ti, *_: (0, 0)),
                      pl.BlockSpec((1, e, cap, d), lambda bi, ti, *_: (bi, 0, 0, 0)),
                      pl.BlockSpec((1, ROUTE_TILE, d), lambda bi, ti, *_: (bi, ti, 0)),
                      pl.BlockSpec((1, d), lambda bi, ti, *_: (0, 0))],
            out_specs=pl.BlockSpec((1, ROUTE_TILE, d), lambda bi, ti, *_: (bi, ti, 0))),
        out_shape=jax.ShapeDtypeStruct((b, t, d), F32),
        compiler_params=_params("parallel", "arbitrary"),
    )(w0, fits, slot_cols, aff, expand, ys, x1, g.reshape(1, d).astype(F32))


def kernel(x, norm_mix_g, w_in, mu_prev, mu_next, w0_f, w_up_f, w0_b, w_up_b, a0_f, a_up_f, a0_b, a_up_b, g_up, k_k, k_a, r_k, ln_x_w, ln_x_b, attn_sink, w_proj_rwkv, w_proj_attn, w_out, norm_ffn_g, w_router, exp_w_gate, exp_w_up, exp_w_down, norm_final_g):
    b, t, d = x.shape
    depth = w_in.shape[0]
    n = b * t
    tm = _largest_divisor(t, (512, 256, 128))
    cap = CAPACITY_FACTOR * t // N_EXPERTS
    bb = _largest_divisor(b, (4, 2))
    xc = x
    for l in range(depth):
        x2 = xc.reshape(n, d)
        u_r, qkv, gates = _in_projection(x2, norm_mix_g[l].astype(F32), w_in[l].astype(BF16), mu_prev[l], mu_next[l],
                                         tm, t)
        y_r = _rwkv_branch(u_r.reshape(b, t, RWKV_COLS), w0_f[l], w_up_f[l], w0_b[l],
                           w_up_b[l], a0_f[l], a_up_f[l], a0_b[l], a_up_b[l], g_up[l], k_k[l], k_a[l], r_k[l],
                           ln_x_w[l], ln_x_b[l])
        y_a = _attention(qkv.reshape(b, t, QKV_COLS), attn_sink[l])
        last = l == depth - 1
        x1, h2, aff, aff_t = _merge(y_r.reshape(n, RWKV_DIM), y_a.reshape(n, ATT_DIM), gates, x2, w_proj_rwkv[l],
                                    w_proj_attn[l], w_out[l], norm_ffn_g[l], w_router[l],
                                    _largest_divisor(t, (1024, 512, 256)), t)
        slot, start128 = _select(aff_t, cap)
        w0, fits = _route_tables(start128, cap)
        xs = _gather(w0, fits, slot, h2.reshape(b, t, d), cap)
        ys = _experts(xs, exp_w_gate[l].astype(BF16), exp_w_up[l].astype(BF16), exp_w_down[l].astype(BF16), bb)
        xc = _scatter(w0, fits, jnp.swapaxes(slot, 1, 2), aff.reshape(b, t, N_EXPERTS), ys, x1.reshape(b, t, d),
                      norm_final_g, final_norm=last)
    return xc
```

```python
import functools
import math

import numpy as np
import jax
import jax.numpy as jnp
from jax import lax
from jax.experimental import pallas as pl
from jax.experimental.pallas import tpu as pltpu

F32 = jnp.float32
BF16 = jnp.bfloat16

RWKV_HEAD = 64
RWKV_HEADS = 8
RWKV_DIM = RWKV_HEADS * RWKV_HEAD
W_LORA = 64
A_LORA = 64
G_LORA = 128
DECAY_SCALE = math.exp(-0.5)
LNX_EPS = 64e-5
ATT_HEADS = 8
ATT_KV_HEADS = 2
ATT_HEAD = 64
ATT_DIM = ATT_HEADS * ATT_HEAD
KV_DIM = ATT_KV_HEADS * ATT_HEAD
WINDOW = 128
BLOCK = 128
NEG_INF = -1e30
N_EXPERTS = 16
CAPACITY_FACTOR = 2
NORM_EPS = 1e-6
RWKV_COLS = 3 * RWKV_DIM + G_LORA + 2 * W_LORA + 2 * A_LORA
QKV_COLS = ATT_DIM + 2 * KV_DIM
OFF_R, OFF_K, OFF_V = 0, RWKV_DIM, 2 * RWKV_DIM
OFF_G = 3 * RWKV_DIM
OFF_WF = OFF_G + G_LORA
OFF_WB = OFF_WF + W_LORA
OFF_AF = OFF_WB + W_LORA
OFF_AB = OFF_AF + A_LORA

LANES = 128
VMEM_LIMIT = 48 * 1024 * 1024

CHUNK = 64


def _dot(a, b):
    return jnp.dot(a, b, preferred_element_type=F32)


def _dot_nt(a, b):
    return lax.dot_general(a, b, (((1,), (1,)), ((), ())), preferred_element_type=F32)


def _split2(x):
    hi = x.astype(BF16)
    lo = (x - hi.astype(F32)).astype(BF16)
    return hi, lo


def _seg_sum(x, bd):
    width = bd.shape[0]
    return jnp.concatenate([_dot(x[:, j:j + width].astype(BF16), bd) for j in range(0, x.shape[1], width)], axis=1)


def _sigmoid(x):
    return 0.5 * jnp.tanh(0.5 * x) + 0.5


def _largest_divisor(n, candidates):
    for c in candidates:
        if n % c == 0:
            return c
    return 1


def _params(*sem):
    return pltpu.CompilerParams(dimension_semantics=sem, vmem_limit_bytes=VMEM_LIMIT)


def _inproj_kernel(x_ref, xp_ref, xn_ref, g_ref, w_ref, mup_ref, mun_ref, ur_ref, qkv_ref, gate_ref, *,
                   col_chunk, tiles_per_seq):
    i = pl.program_id(0)
    tm = x_ref.shape[0]
    halo = xp_ref.shape[0]
    x = jnp.concatenate([xp_ref[...], x_ref[...], xn_ref[...]], axis=0)
    h = x * lax.rsqrt(jnp.mean(x * x, axis=-1, keepdims=True) + NORM_EPS) * g_ref[...]
    row = lax.broadcasted_iota(jnp.int32, (tm + 2 * halo, 1), 0)
    pos = i % tiles_per_seq
    lo = jnp.where(pos == 0, halo, 0)
    hi = jnp.where(pos == tiles_per_seq - 1, tm + halo, tm + 2 * halo)
    h = jnp.where((row < lo) | (row >= hi), 0.0, h)
    hb_all = h.astype(BF16)
    hb = h[halo:halo + tm].astype(BF16)
    for j in range(0, RWKV_COLS, col_chunk):
        w = min(col_chunk, RWKV_COLS - j)
        u = _dot(hb_all, w_ref[:, j:j + w])
        prev = pltpu.roll(u, 1, axis=0)
        nxt = pltpu.roll(u, tm + 2 * halo - 1, axis=0)
        us = u + mup_ref[:, j:j + w] * (prev - u) + mun_ref[:, j:j + w] * (nxt - u)
        ur_ref[:, j:j + w] = us[halo:halo + tm].astype(BF16)
    c0 = RWKV_COLS
    for ref in (qkv_ref, gate_ref):
        width = ref.shape[-1]
        for j in range(0, width, col_chunk):
            w = min(col_chunk, width - j)
            ref[:, j:j + w] = _dot(hb, w_ref[:, c0 + j:c0 + j + w]).astype(BF16)
        c0 += width


def _in_projection(x2, g, w_in_bf, mu_prev, mu_next, tm, seq_len):
    n, d = x2.shape
    cols = w_in_bf.shape[1]
    gate_cols = cols - RWKV_COLS - QKV_COLS
    halo = 8
    hb = tm // halo
    n_halo = n // halo
    return pl.pallas_call(
        functools.partial(_inproj_kernel, col_chunk=512, tiles_per_seq=seq_len // tm),
        grid=(n // tm,),
        in_specs=[pl.BlockSpec((tm, d), lambda i: (i, 0)),
                  pl.BlockSpec((halo, d), lambda i: (jnp.maximum(i * hb - 1, 0), 0)),
                  pl.BlockSpec((halo, d), lambda i: (jnp.minimum((i + 1) * hb, n_halo - 1), 0)),
                  pl.BlockSpec((1, d), lambda i: (0, 0)),
                  pl.BlockSpec((d, cols), lambda i: (0, 0)),
                  pl.BlockSpec((1, RWKV_COLS), lambda i: (0, 0)),
                  pl.BlockSpec((1, RWKV_COLS), lambda i: (0, 0))],
        out_specs=[pl.BlockSpec((tm, RWKV_COLS), lambda i: (i, 0)),
                   pl.BlockSpec((tm, QKV_COLS), lambda i: (i, 0)),
                   pl.BlockSpec((tm, gate_cols), lambda i: (i, 0))],
        out_shape=[jax.ShapeDtypeStruct((n, RWKV_COLS), BF16),
                   jax.ShapeDtypeStruct((n, QKV_COLS), BF16),
                   jax.ShapeDtypeStruct((n, gate_cols), BF16)],
        compiler_params=_params("parallel"),
    )(x2, x2, x2, g.reshape(1, d), w_in_bf, mu_prev.reshape(1, -1).astype(F32), mu_next.reshape(1, -1).astype(F32))


GROUP_HEADS = 4
GROUP = GROUP_HEADS * RWKV_HEAD
N_GROUPS = RWKV_HEADS // GROUP_HEADS
ROWS_PER_STEP = 256
SEQS_PER_STEP = 2


def _rwkv_kernel(*refs, reverse, final):
    if final:
        (u_ref, w0_ref, wup_ref, a0_ref, aup_ref, kk_ref, ka_ref,
         bd_ref, tri_ref, a0o_ref, aupo_ref, gup_ref, rk_ref, lnw_ref, lnb_ref, yb_ref,
         o_ref, st_ref) = refs
    else:
        (u_ref, w0_ref, wup_ref, a0_ref, aup_ref, kk_ref, ka_ref,
         bd_ref, tri_ref, o_ref, st_ref) = refs
    NB, TB = u_ref.shape[0], u_ref.shape[1]
    L = CHUNK
    N = RWKV_HEAD
    n_ck = TB // L
    step = pl.program_id(1)

    @pl.when(step == 0)
    def _():
        st_ref[...] = jnp.zeros_like(st_ref)

    bd = bd_ref[...]
    tri = tri_ref[...]

    def prep(bi):
        col = lambda off, width: u_ref[bi, :, off:off + width].astype(F32)
        r = col(OFF_R, RWKV_DIM)
        k = col(OFF_K, RWKV_DIM)
        v = col(OFF_V, RWKV_DIM)
        w_lo = col(OFF_WB if reverse else OFF_WF, W_LORA)
        a_lo = col(OFF_AB if reverse else OFF_AF, A_LORA)
        lw = -DECAY_SCALE * _sigmoid(w0_ref[...] + _dot(jnp.tanh(w_lo).astype(BF16), wup_ref[...]))
        a = _sigmoid(a0_ref[...] + _dot(a_lo.astype(BF16), aup_ref[...]))
        kkr = k * kk_ref[...]
        kk = kkr * lax.rsqrt(jnp.maximum(_seg_sum(kkr * kkr, bd), 1e-24))
        kd = k * (1.0 + (a - 1.0) * ka_ref[...])
        b = a * kk
        l_hi, l_lo = _split2(lw)
        cs = _dot(tri, l_hi) + _dot(tri, l_lo)
        edge = 0 if reverse else L - 1
        e_tot = [jnp.exp(cs[c * L + edge:c * L + edge + 1]) for c in range(n_ck)]
        e_neg = jnp.exp(-cs)
        e_rem = jnp.concatenate([e_neg[c * L:(c + 1) * L] * e_tot[c] for c in range(n_ck)], axis=0)
        return dict(r=r, k=k, v=v, kd=kd, col=col, e_tot=e_tot,
                    a_t=(kk * jnp.exp(cs - lw)).astype(BF16), r_t=(r * jnp.exp(cs)).astype(BF16),
                    k_t=(kd * e_neg).astype(BF16), b_t=(b * e_neg).astype(BF16),
                    k_h=(kd * e_rem).astype(BF16), b_hf=-b * e_rem)

    pre = [prep(bi) for bi in range(NB)]

    ii = lax.broadcasted_iota(jnp.int32, (GROUP, GROUP), 0)
    jj = lax.broadcasted_iota(jnp.int32, (GROUP, GROUP), 1)
    same_head = (ii // N) == (jj // N)
    ti = lax.broadcasted_iota(jnp.int32, (L, GROUP), 0)
    si = lax.broadcasted_iota(jnp.int32, (L, GROUP), 1) % N
    if reverse:
        strict = si > ti
        incl = si >= ti
    else:
        strict = si < ti
        incl = si <= ti
    eye = (si == ti).astype(F32)
    level_masks = []
    s = 1
    while s < L:
        same = (ti // (2 * s)) == (si // (2 * s))
        t_hi = (ti & s) != 0
        s_hi = (si & s) != 0
        if reverse:
            level_masks.append(same & jnp.logical_not(t_hi) & s_hi)
        else:
            level_masks.append(same & t_hi & jnp.logical_not(s_hi))
        s *= 2

    def bdiag(xp):
        return jnp.where(same_head, jnp.concatenate([xp.astype(BF16)] * GROUP_HEADS, axis=0), 0.0)

    def head_transpose(xp):
        return jnp.concatenate([xp[:, h * N:(h + 1) * N].T for h in range(GROUP_HEADS)], axis=1)

    order = list(range(n_ck))[::-1] if reverse else list(range(n_ck))
    cgs = [(bi, c, g) for bi in range(NB) for c in order for g in range(N_GROUPS)]

    def tile(name, cg):
        bi, c, g = cg
        return pre[bi][name][c * L:(c + 1) * L, g * GROUP:(g + 1) * GROUP]

    ar = {cg: jnp.concatenate([tile('a_t', cg), tile('r_t', cg)], axis=0) for cg in cgs}
    arb = {cg: _dot_nt(ar[cg], bdiag(tile('b_t', cg))) for cg in cgs}
    ark = {cg: _dot_nt(ar[cg], bdiag(tile('k_t', cg))) for cg in cgs}
    ab = {cg: arb[cg][0:L] for cg in cgs}
    rb = {cg: arb[cg][L:2 * L] for cg in cgs}
    ak = {cg: ark[cg][0:L] for cg in cgs}
    rk_ = {cg: ark[cg][L:2 * L] for cg in cgs}
    x = {cg: eye - jnp.where(level_masks[0], ab[cg], 0.0) for cg in cgs}
    for m in level_masks[1:]:
        xn = {cg: _dot(x[cg].astype(BF16), bdiag(jnp.where(m, ab[cg], 0.0))).astype(BF16) for cg in cgs}
        x = {cg: x[cg] - _dot(xn[cg], bdiag(x[cg])) for cg in cgs}
    xb = {cg: x[cg].astype(BF16) for cg in cgs}
    xa = {cg: _dot(xb[cg], bdiag(tile('a_t', cg))) for cg in cgs}
    xa_w = {cg: bdiag(xa[cg]) for cg in cgs}
    mrb_n = {cg: jnp.where(incl, -rb[cg], 0.0).astype(BF16) for cg in cgs}
    v_w = {cg: bdiag(tile('v', cg)) for cg in cgs}
    mv = {cg: _dot(jnp.concatenate([jnp.where(strict, ak[cg], 0.0).astype(BF16),
                                    jnp.where(incl, rk_[cg], 0.0).astype(BF16)], axis=0), v_w[cg]) for cg in cgs}
    xmv = {cg: _dot(xb[cg], bdiag(mv[cg][0:L])) for cg in cgs}
    bt = {cg: head_transpose(tile('b_hf', cg)).astype(BF16) for cg in cgs}
    lhs2 = {cg: jnp.concatenate([mrb_n[cg], bt[cg]], axis=0) for cg in cgs}
    pa = {cg: _dot(lhs2[cg], xa_w[cg]) for cg in cgs}
    rq = {cg: (tile('r_t', cg).astype(F32) + pa[cg][0:L]).astype(BF16) for cg in cgs}
    pm = {cg: _dot(lhs2[cg], bdiag(xmv[cg])) for cg in cgs}
    dd = {cg: mv[cg][L:2 * L] + pm[cg][0:L] for cg in cgs}
    eye_b = eye.astype(BF16)
    c2 = {cg: _dot_nt(eye_b, bdiag(pm[cg][L:2 * L])) for cg in cgs}
    cc = {cg: _dot(head_transpose(tile('v', cg)).astype(BF16), bdiag(tile('k_h', cg))) + c2[cg] for cg in cgs}
    wt_w = {cg: bdiag(pa[cg][L:2 * L]) for cg in cgs}

    state = {(bi, g): st_ref[bi, g] for bi in range(NB) for g in range(N_GROUPS)}
    y_tiles = {}
    for c in order:
        for bi in range(NB):
            for g in range(N_GROUPS):
                cg = (bi, c, g)
                sb = state[bi, g].astype(BF16)
                decay = pre[bi]['e_tot'][c][:, g * GROUP:(g + 1) * GROUP]
                state[bi, g] = state[bi, g] * decay + _dot_nt(sb, wt_w[cg]) + cc[cg]
                y_tiles[cg] = _dot_nt(rq[cg], bdiag(sb)) + dd[cg]
    for key, val in state.items():
        st_ref[key] = val

    for bi in range(NB):
        wkv = jnp.concatenate(
            [jnp.concatenate([y_tiles[(bi, c, g)] for g in range(N_GROUPS)], axis=1) for c in range(n_ck)], axis=0)
        if not final:
            o_ref[bi] = wkv
            continue
        p = pre[bi]
        r, k, v, kd, col = p['r'], p['k'], p['v'], p['kd'], p['col']
        wkv = wkv + yb_ref[bi]
        inv_n = 1.0 / N
        mean = _seg_sum(wkv, bd) * inv_n
        xc = wkv - mean
        var = _seg_sum(xc * xc, bd) * inv_n
        normed = xc * lax.rsqrt(var + LNX_EPS) * lnw_ref[...] + lnb_ref[...]
        a_o = _sigmoid(a0o_ref[...] + _dot(col(OFF_AB, A_LORA).astype(BF16), aupo_ref[...]))
        kd_o = k * (1.0 + (a_o - 1.0) * ka_ref[...])
        bonus = _seg_sum(r * rk_ref[...] * (kd + kd_o), bd) * v
        gg = _dot(_sigmoid(col(OFF_G, G_LORA)).astype(BF16), gup_ref[...])
        o_ref[bi] = ((normed + bonus) * gg).astype(o_ref.dtype)


def _rwkv_pass(u_r, consts, *, reverse, final, extra=()):
    b, t, _ = u_r.shape
    tb = _largest_divisor(t, (ROWS_PER_STEP, CHUNK))
    nb = _largest_divisor(b, (SEQS_PER_STEP,))
    n_steps = t // tb

    def sidx(s):
        return (n_steps - 1 - s) if reverse else s

    def full(shape):
        return pl.BlockSpec(shape, lambda bi, s: (0,) * len(shape))

    in_specs = [pl.BlockSpec((nb, tb, RWKV_COLS), lambda bi, s: (bi, sidx(s), 0))]
    args = [u_r]
    for arr in consts:
        in_specs.append(full(arr.shape))
        args.append(arr)
    if final:
        for arr in extra[:-1]:
            in_specs.append(full(arr.shape))
            args.append(arr)
        in_specs.append(pl.BlockSpec((nb, tb, RWKV_DIM), lambda bi, s: (bi, sidx(s), 0)))
        args.append(extra[-1])
    out_dtype = BF16 if final else F32
    return pl.pallas_call(
        functools.partial(_rwkv_kernel, reverse=reverse, final=final),
        grid=(b // nb, n_steps),
        in_specs=in_specs,
        out_specs=pl.BlockSpec((nb, tb, RWKV_DIM), lambda bi, s: (bi, sidx(s), 0)),
        out_shape=jax.ShapeDtypeStruct((b, t, RWKV_DIM), out_dtype),
        scratch_shapes=[pltpu.VMEM((nb, N_GROUPS, RWKV_HEAD, GROUP), F32)],
        compiler_params=_params("parallel", "arbitrary"),
    )(*args)


def _rwkv_branch(u_r, w0_f, w_up_f, w0_b, w_up_b, a0_f, a_up_f, a0_b, a_up_b,
                 g_up, k_k, k_a, r_k, ln_w, ln_b):
    t = u_r.shape[1]
    tb = _largest_divisor(t, (ROWS_PER_STEP, CHUNK))
    row = lambda a: a.reshape(1, -1).astype(F32)
    seg = np.arange(GROUP) // RWKV_HEAD
    bd = jnp.asarray(seg[:, None] == seg[None, :], BF16)
    ti = np.arange(tb)
    same_chunk = (ti[:, None] // CHUNK) == (ti[None, :] // CHUNK)
    tri_f = jnp.asarray(same_chunk & (ti[None, :] <= ti[:, None]), BF16)
    tri_b = jnp.asarray(same_chunk & (ti[None, :] >= ti[:, None]), BF16)
    common = lambda w0, wup, a0, aup, tri: [row(w0), wup.astype(BF16), row(a0), aup.astype(BF16), row(k_k), row(k_a),
                                            bd, tri]
    y_b = _rwkv_pass(u_r, common(w0_b, w_up_b, a0_b, a_up_b, tri_b), reverse=True, final=False)
    extra = [row(a0_b), a_up_b.astype(BF16), g_up.astype(BF16), row(r_k), row(ln_w), row(ln_b), y_b]
    return _rwkv_pass(u_r, common(w0_f, w_up_f, a0_f, a_up_f, tri_f), reverse=False, final=True, extra=extra)


Q_BLOCKS = 2


def _attn_kernel(sink_ref, *refs):
    bias_refs = refs[:Q_BLOCKS]
    q_ref, kp_ref, kc_ref, kn_ref, vp_ref, vc_ref, vn_ref, o_ref = refs[Q_BLOCKS:]
    span = BLOCK + 2 * WINDOW
    kk = jnp.concatenate([kp_ref[0], kc_ref[0], kn_ref[0]], axis=0)
    vv = jnp.concatenate([vp_ref[0], vc_ref[0], vn_ref[0]], axis=0)
    group = ATT_HEADS // ATT_KV_HEADS
    kv_lane = lax.broadcasted_iota(jnp.int32, (1, KV_DIM), 1) // ATT_HEAD
    v_ext = [jnp.where(kv_lane == kv, vv, 1.0) for kv in range(ATT_KV_HEADS)]
    qs = q_ref[0] * (ATT_HEAD ** -0.5)
    hsl = lambda h: slice(h * ATT_HEAD, (h + 1) * ATT_HEAD)
    qrows = lambda j: slice(j * BLOCK, (j + 1) * BLOCK)
    krows = lambda j: slice(j * BLOCK, j * BLOCK + span)
    jh = [(j, h) for j in range(Q_BLOCKS) for h in range(ATT_HEADS)]
    s = {c: _dot_nt(qs[qrows(c[0]), hsl(c[1])], kk[krows(c[0]), hsl(c[1] // group)]) + bias_refs[c[0]][0, c[1]]
         for c in jh}
    m = {c: jnp.maximum(jnp.max(s[c], axis=-1, keepdims=True), sink_ref[c[1]]) for c in jh}
    p = {c: jnp.exp(s[c] - m[c]).astype(BF16) for c in jh}
    pv = {c: _dot(p[c], v_ext[c[1] // group][krows(c[0])]) for c in jh}
    for j in range(Q_BLOCKS):
        outs = []
        for h in range(ATT_HEADS):
            kv = h // group
            other = (kv + 1) % ATT_KV_HEADS
            denom = pv[j, h][:, other * ATT_HEAD:other * ATT_HEAD + 1] + jnp.exp(sink_ref[h] - m[j, h])
            outs.append(pv[j, h][:, hsl(kv)] / denom)
        o_ref[0, qrows(j)] = jnp.concatenate(outs, axis=-1).astype(o_ref.dtype)


def _attention(qkv, sink):
    b, t, _ = qkv.shape
    nb = t // BLOCK
    assert nb % Q_BLOCKS == 0
    n_steps = nb // Q_BLOCKS
    rows = Q_BLOCKS * BLOCK
    span = BLOCK + 2 * WINDOW
    kcol = ATT_DIM // KV_DIM
    halo = lambda col, f: pl.BlockSpec((1, BLOCK, KV_DIM), lambda bi, i: (bi, f(i), col))
    prev = lambda i: jnp.maximum(i * Q_BLOCKS - 1, 0)
    nxt = lambda i: jnp.minimum((i + 1) * Q_BLOCKS, nb - 1)
    cur = lambda col: pl.BlockSpec((1, rows, KV_DIM), lambda bi, i: (bi, i, col))
    col = np.arange(span)[None, :]
    rel = np.abs(col - WINDOW - np.arange(BLOCK)[:, None])
    slopes = 2.0 ** (-8.0 * np.arange(1, ATT_HEADS + 1) / ATT_HEADS)
    base = np.where(rel[None] <= WINDOW, -slopes[:, None, None] * rel[None], NEG_INF)
    tables = []
    for variant in range(4):
        dead = ((col < WINDOW) & bool(variant & 1)) | ((col >= WINDOW + BLOCK) & bool(variant & 2))
        tables.append(np.where(dead[None], NEG_INF, base))
    bias = jnp.asarray(np.stack(tables).astype(np.float32))

    def bias_spec(j):
        variant = lambda i: (jnp.where(i * Q_BLOCKS + j == 0, 1, 0) + jnp.where(i * Q_BLOCKS + j == nb - 1, 2, 0))
        return pl.BlockSpec((1, ATT_HEADS, BLOCK, span), lambda bi, i: (variant(i), 0, 0, 0))

    return pl.pallas_call(
        _attn_kernel,
        grid=(b, n_steps),
        in_specs=[pl.BlockSpec(memory_space=pltpu.SMEM)] + [bias_spec(j) for j in range(Q_BLOCKS)]
        + [pl.BlockSpec((1, rows, ATT_DIM), lambda bi, i: (bi, i, 0)),
           halo(kcol, prev), cur(kcol), halo(kcol, nxt),
           halo(kcol + 1, prev), cur(kcol + 1), halo(kcol + 1, nxt)],
        out_specs=pl.BlockSpec((1, rows, ATT_DIM), lambda bi, i: (bi, i, 0)),
        out_shape=jax.ShapeDtypeStruct((b, t, ATT_DIM), BF16),
        compiler_params=_params("parallel", "parallel"),
    )(sink.astype(F32), *([bias] * Q_BLOCKS), qkv, qkv, qkv, qkv, qkv, qkv, qkv)


MERGE_SPLIT = 2


def _merge_kernel(yr_ref, ya_ref, gate_ref, x_ref, wpr_ref, wpa_ref, wo_ref, g_ref, wrt_ref,
                  x1_ref, h2_ref, aff_ref, afft_ref):
    tm, d = x_ref.shape
    part = tm // MERGE_SPLIT
    for j in range(MERGE_SPLIT):
        rs = slice(j * part, (j + 1) * part)
        gl_r = gate_ref[rs, 0:d].astype(F32)
        gl_a = gate_ref[rs, d:2 * d].astype(F32)
        merged = (_sigmoid(gl_r) * _dot(yr_ref[rs, :], wpr_ref[...])
                  + _sigmoid(gl_a) * _dot(ya_ref[rs, :], wpa_ref[...]))
        x1 = x_ref[rs, :] + _dot(merged.astype(BF16), wo_ref[...])
        x1_ref[rs, :] = x1
        h2 = (x1 * lax.rsqrt(jnp.mean(x1 * x1, axis=-1, keepdims=True) + NORM_EPS) * g_ref[...]).astype(BF16)
        h2_ref[rs, :] = h2
        logits_t = _dot_nt(wrt_ref[...], h2)
        mx = jnp.max(logits_t, axis=0, keepdims=True)
        ex = jnp.exp(logits_t - mx)
        aff_t = ex / jnp.sum(ex, axis=0, keepdims=True)
        afft_ref[0, :, rs] = aff_t
        aff_ref[rs, :] = aff_t.T


def _merge(y_r, y_a, gates, x2, wpr, wpa, wo, g, w_router, tm, seq_len):
    n, d = x2.shape
    e = w_router.shape[1]
    tiles_per_seq = seq_len // tm
    full = lambda shape: pl.BlockSpec(shape, lambda i: (0,) * len(shape))
    rows = lambda w: pl.BlockSpec((tm, w), lambda i: (i, 0))
    return pl.pallas_call(
        _merge_kernel,
        grid=(n // tm,),
        in_specs=[rows(RWKV_DIM), rows(ATT_DIM), rows(2 * d), rows(d),
                  full(wpr.shape), full(wpa.shape), full(wo.shape), full((1, d)), full((e, d))],
        out_specs=[rows(d), rows(d), rows(e),
                   pl.BlockSpec((1, e, tm), lambda i: (i // tiles_per_seq, 0, i % tiles_per_seq))],
        out_shape=[jax.ShapeDtypeStruct((n, d), F32), jax.ShapeDtypeStruct((n, d), BF16),
                   jax.ShapeDtypeStruct((n, e), F32),
                   jax.ShapeDtypeStruct((n // seq_len, e, seq_len), F32)],
        compiler_params=_params("parallel"),
    )(y_r, y_a, gates, x2, wpr.astype(BF16), wpa.astype(BF16), wo.astype(BF16), g.reshape(1, d).astype(F32),
      w_router.T.astype(BF16))


def _select_kernel(a_ref, gsame_ref, gprev_ref, triu_ref, slot_ref, start_ref, *, cap):
    nb, rows, _ = a_ref.shape
    affs = [a_ref[i] for i in range(nb)]
    gsame = gsame_ref[...]
    gprev = gprev_ref[...]
    triu = triu_ref[...]
    capf = float(cap)

    def group_count(mask):
        per_lane = _dot(gsame, jnp.where(mask, 1.0, 0.0).astype(BF16))
        return jnp.sum(per_lane, axis=-1, keepdims=True)

    def search(i, prefixes):
        out = []
        for aff, prefix in zip(affs, prefixes):
            cand = prefix | lax.shift_left(jnp.int32(1), 30 - i)
            out.append(jnp.where(group_count(aff >= pltpu.bitcast(cand, F32)) >= capf, cand, prefix))
        return tuple(out)

    thr_bits = lax.fori_loop(0, 31, search, tuple(jnp.zeros((rows, 1), jnp.int32) for _ in range(nb)))

    def prefix_count(mask):
        mb = jnp.where(mask, 1.0, 0.0)
        inc = _dot(mb.astype(BF16), triu)
        tot = jnp.broadcast_to(inc[:, LANES - 1:LANES], inc.shape).astype(BF16)
        before = _dot(gprev, tot)
        return inc - mb + before, before

    for i, aff in enumerate(affs):
        thr = pltpu.bitcast(thr_bits[i], F32)
        gt = aff > thr
        eq = aff == thr
        need = capf - group_count(gt)
        sel = gt | (eq & (prefix_count(eq)[0] < need))
        pos, before = prefix_count(sel)
        slot_ref[i] = jnp.where(sel, pos, -1.0).astype(jnp.int32)
        start_ref[i] = before.astype(jnp.int32)


def _select(aff_t, cap):
    b, e, t = aff_t.shape
    tiles = t // LANES
    rows = e * tiles
    nb = _largest_divisor(b, (4, 2))
    rid = np.arange(rows)
    same = (rid[:, None] // tiles) == (rid[None, :] // tiles)
    gsame = jnp.asarray(same, BF16)
    gprev = jnp.asarray(same & (rid[None, :] < rid[:, None]), BF16)
    li = np.arange(LANES)
    triu = jnp.asarray(li[:, None] <= li[None, :], BF16)
    full = lambda shape: pl.BlockSpec(shape, lambda bi: (0,) * len(shape))
    slot, start = pl.pallas_call(
        functools.partial(_select_kernel, cap=cap),
        grid=(b // nb,),
        in_specs=[pl.BlockSpec((nb, rows, LANES), lambda bi: (bi, 0, 0)),
                  full((rows, rows)), full((rows, rows)), full((LANES, LANES))],
        out_specs=[pl.BlockSpec((nb, rows, LANES), lambda bi: (bi, 0, 0))] * 2,
        out_shape=[jax.ShapeDtypeStruct((b, rows, LANES), jnp.int32)] * 2,
        compiler_params=_params("parallel"),
    )(aff_t.reshape(b, rows, LANES), gsame, gprev, triu)
    return slot.reshape(b, e, t), start[:, :, 0].reshape(b, e, tiles)


ROUTE_TILE = 256
ROUTE_WIN = 64
WIN_ALIGN = 16


def _route_tables(start128, cap):
    per = ROUTE_TILE // LANES
    start = start128[:, :, ::per]
    nxt = jnp.concatenate([start[:, :, 1:], jnp.full_like(start[:, :, :1], cap)], axis=-1)
    w0 = jnp.minimum((start // WIN_ALIGN) * WIN_ALIGN, cap - ROUTE_WIN)
    fits = jnp.all(nxt - w0 <= ROUTE_WIN, axis=1)
    return w0.reshape(-1).astype(jnp.int32), fits.reshape(-1).astype(jnp.int32)


def _gather_kernel(w0_ref, fits_ref, slot_ref, h_ref, xs_ref, *, col_chunk):
    n_exp, cap, d = xs_ref.shape[1], xs_ref.shape[2], xs_ref.shape[3]
    nt = h_ref.shape[1] // ROUTE_TILE
    bi = pl.program_id(0)
    xs_ref[...] = jnp.zeros_like(xs_ref)
    ridx = lax.broadcasted_iota(jnp.int32, (ROUTE_WIN, 1), 0)
    cidx = lax.broadcasted_iota(jnp.int32, (cap, 1), 0)
    for j in range(nt):
        tok = slice(j * ROUTE_TILE, (j + 1) * ROUTE_TILE)
        fit = fits_ref[bi * nt + j]

        @pl.when(fit != 0)
        def _():
            w0 = [pl.multiple_of(w0_ref[(bi * n_exp + e) * nt + j], WIN_ALIGN) for e in range(n_exp)]
            onehot = jnp.concatenate(
                [jnp.where(slot_ref[0, e:e + 1, tok] - w0[e] == ridx, 1.0, 0.0).astype(BF16) for e in range(n_exp)],
                axis=0)
            for c0 in range(0, d, col_chunk):
                rows = _dot(onehot, h_ref[0, tok, c0:c0 + col_chunk]).astype(BF16)
                for e in range(n_exp):
                    win = (0, e, pl.ds(w0[e], ROUTE_WIN), slice(c0, c0 + col_chunk))
                    xs_ref[win] = xs_ref[win] + rows[e * ROUTE_WIN:(e + 1) * ROUTE_WIN]

        @pl.when(fit == 0)
        def _():
            for e in range(n_exp):
                onehot = jnp.where(slot_ref[0, e:e + 1, tok] == cidx, 1.0, 0.0).astype(BF16)
                xs_ref[0, e] = xs_ref[0, e] + _dot(onehot, h_ref[0, tok, :]).astype(BF16)


def _gather(w0, fits, slot, h2, cap):
    b, t, d = h2.shape
    e = slot.shape[1]
    return pl.pallas_call(
        functools.partial(_gather_kernel, col_chunk=512),
        grid_spec=pltpu.PrefetchScalarGridSpec(
            num_scalar_prefetch=2,
            grid=(b,),
            in_specs=[pl.BlockSpec((1, e, t), lambda bi, *_: (bi, 0, 0)),
                      pl.BlockSpec((1, t, d), lambda bi, *_: (bi, 0, 0))],
            out_specs=pl.BlockSpec((1, e, cap, d), lambda bi, *_: (bi, 0, 0, 0))),
        out_shape=jax.ShapeDtypeStruct((b, e, cap, d), BF16),
        compiler_params=_params("parallel"),
    )(w0, fits, slot, h2)


def _expert_kernel(xs_ref, wg_ref, wu_ref, wd_ref, ys_ref):
    bb, _, cap, d = xs_ref.shape
    xs = xs_ref[...].reshape(bb * cap, d)
    gate = _dot(xs, wg_ref[0])
    up = _dot(xs, wu_ref[0])
    hid = (gate * _sigmoid(gate) * up).astype(BF16)
    ys_ref[...] = _dot(hid, wd_ref[0]).astype(ys_ref.dtype).reshape(bb, 1, cap, d)


def _experts(xs, wg, wu, wd, bb):
    b, e, cap, d = xs.shape
    ff = wg.shape[-1]
    tok = pl.BlockSpec((bb, 1, cap, d), lambda ei, bi: (bi, ei, 0, 0))
    return pl.pallas_call(
        _expert_kernel,
        grid=(e, b // bb),
        in_specs=[tok,
                  pl.BlockSpec((1, d, ff), lambda ei, bi: (ei, 0, 0)),
                  pl.BlockSpec((1, d, ff), lambda ei, bi: (ei, 0, 0)),
                  pl.BlockSpec((1, ff, d), lambda ei, bi: (ei, 0, 0))],
        out_specs=tok,
        out_shape=jax.ShapeDtypeStruct((b, e, cap, d), BF16),
        compiler_params=_params("parallel", "arbitrary"),
    )(xs, wg, wu, wd)


def _scatter_kernel(w0_ref, fits_ref, slot_ref, aff_ref, expand_ref, ys_ref, x1_ref, g_ref, o_ref, *, final_norm):
    n_exp, cap = ys_ref.shape[1], ys_ref.shape[2]
    bi, j = pl.program_id(0), pl.program_id(1)
    nt = pl.num_programs(1)
    slot = slot_ref[0]
    aff = aff_ref[0]
    fit = fits_ref[bi * nt + j]

    def finish(acc):
        if final_norm:
            acc = acc * lax.rsqrt(jnp.mean(acc * acc, axis=-1, keepdims=True) + NORM_EPS) * g_ref[...]
        o_ref[0] = acc

    @pl.when(fit != 0)
    def _():
        w0 = [pl.multiple_of(w0_ref[(bi * n_exp + e) * nt + j], WIN_ALIGN) for e in range(n_exp)]
        lane = lax.broadcasted_iota(jnp.int32, (1, n_exp * ROUTE_WIN), 1)
        w0_row = jnp.zeros((1, n_exp * ROUTE_WIN), jnp.int32)
        for e in range(n_exp):
            w0_row = jnp.where(lane // ROUTE_WIN == e, w0[e], w0_row)
        target = (w0_row + lane % ROUTE_WIN).astype(F32)
        slot_l = _dot(slot.astype(F32).astype(BF16), expand_ref[...])
        aff_l = _dot(aff.astype(BF16), expand_ref[...])
        onehot = jnp.where(slot_l == target, aff_l, 0.0).astype(BF16)
        rows = jnp.concatenate([ys_ref[0, e, pl.ds(w0[e], ROUTE_WIN), :] for e in range(n_exp)], axis=0)
        finish(x1_ref[0] + _dot(onehot, rows))

    @pl.when(fit == 0)
    def _():
        cidx = lax.broadcasted_iota(jnp.int32, (1, cap), 1)
        acc = x1_ref[0]
        for e in range(n_exp):
            onehot = jnp.where(slot[:, e:e + 1] == cidx, aff[:, e:e + 1], 0.0).astype(BF16)
            acc = acc + _dot(onehot, ys_ref[0, e])
        finish(acc)


def _scatter(w0, fits, slot_cols, aff, ys, x1, g, final_norm):
    b, t, d = x1.shape
    e, cap = ys.shape[1], ys.shape[2]
    assert cap <= 256, "slot ids must stay exact in bf16"
    lane_exp = np.arange(e * ROUTE_WIN) // ROUTE_WIN
    expand = jnp.asarray(np.arange(e)[:, None] == lane_exp[None, :], BF16)
    return pl.pallas_call(
        functools.partial(_scatter_kernel, final_norm=final_norm),
        grid_spec=pltpu.PrefetchScalarGridSpec(
            num_scalar_prefetch=2,
            grid=(b, t // ROUTE_TILE),
            in_specs=[pl.BlockSpec((1, ROUTE_TILE, e), lambda bi, ti, *_: (bi, ti, 0)),
                      pl.BlockSpec((1, ROUTE_TILE, e), lambda bi, ti, *_: (bi, ti, 0)),
                      pl.BlockSpec((e, e * ROUTE_WIN), lambda bi, ti, *_: (0, 0)),
                      pl.BlockSpec((1, e, cap, d), lambda bi, ti, *_: (bi, 0, 0, 0)),
                      pl.BlockSpec((1, ROUTE_TILE, d), lambda bi, ti, *_: (bi, ti, 0)),
                      pl.BlockSpec((1, d), lambda bi, ti, *_: (0, 0))],
            out_specs=pl.BlockSpec((1, ROUTE_TILE, d), lambda bi, ti, *_: (bi, ti, 0))),
        out_shape=jax.ShapeDtypeStruct((b, t, d), F32),
        compiler_params=_params("parallel", "arbitrary"),
    )(w0, fits, slot_cols, aff, expand, ys, x1, g.reshape(1, d).astype(F32))


def kernel(x, norm_mix_g, w_in, mu_prev, mu_next, w0_f, w_up_f, w0_b, w_up_b, a0_f, a_up_f, a0_b, a_up_b, g_up, k_k, k_a, r_k, ln_x_w, ln_x_b, attn_sink, w_proj_rwkv, w_proj_attn, w_out, norm_ffn_g, w_router, exp_w_gate, exp_w_up, exp_w_down, norm_final_g):
    b, t, d = x.shape
    depth = w_in.shape[0]
    n = b * t
    tm = _largest_divisor(t, (512, 256, 128))
    cap = CAPACITY_FACTOR * t // N_EXPERTS
    bb = _largest_divisor(b, (4, 2))
    xc = x
    for l in range(depth):
        x2 = xc.reshape(n, d)
        u_r, qkv, gates = _in_projection(x2, norm_mix_g[l].astype(F32), w_in[l].astype(BF16), mu_prev[l], mu_next[l],
                                         tm, t)
        y_r = _rwkv_branch(u_r.reshape(b, t, RWKV_COLS), w0_f[l], w_up_f[l], w0_b[l],
                           w_up_b[l], a0_f[l], a_up_f[l], a0_b[l], a_up_b[l], g_up[l], k_k[l], k_a[l], r_k[l],
                           ln_x_w[l], ln_x_b[l])
        y_a = _attention(qkv.reshape(b, t, QKV_COLS), attn_sink[l])
        last = l == depth - 1
        x1, h2, aff, aff_t = _merge(y_r.reshape(n, RWKV_DIM), y_a.reshape(n, ATT_DIM), gates, x2, w_proj_rwkv[l],
                                    w_proj_attn[l], w_out[l], norm_ffn_g[l], w_router[l],
                                    _largest_divisor(t, (1024, 512, 256)), t)
        slot, start128 = _select(aff_t, cap)
        w0, fits = _route_tables(start128, cap)
        xs = _gather(w0, fits, slot, h2.reshape(b, t, d), cap)
        ys = _experts(xs, exp_w_gate[l].astype(BF16), exp_w_up[l].astype(BF16), exp_w_down[l].astype(BF16), bb)
        xc = _scatter(w0, fits, jnp.swapaxes(slot, 1, 2), aff.reshape(b, t, N_EXPERTS), ys, x1.reshape(b, t, d),
                      norm_final_g, final_norm=last)
    return xc
```

```python
import functools
import math

import numpy as np
import jax
import jax.numpy as jnp
from jax import lax
from jax.experimental import pallas as pl
from jax.experimental.pallas import tpu as pltpu

F32 = jnp.float32
BF16 = jnp.bfloat16

RWKV_HEAD = 64
RWKV_HEADS = 8
RWKV_DIM = RWKV_HEADS * RWKV_HEAD
W_LORA = 64
A_LORA = 64
G_LORA = 128
DECAY_SCALE = math.exp(-0.5)
LNX_EPS = 64e-5
ATT_HEADS = 8
ATT_KV_HEADS = 2
ATT_HEAD = 64
ATT_DIM = ATT_HEADS * ATT_HEAD
KV_DIM = ATT_KV_HEADS * ATT_HEAD
WINDOW = 128
BLOCK = 128
NEG_INF = -1e30
N_EXPERTS = 16
CAPACITY_FACTOR = 2
NORM_EPS = 1e-6
RWKV_COLS = 3 * RWKV_DIM + G_LORA + 2 * W_LORA + 2 * A_LORA
QKV_COLS = ATT_DIM + 2 * KV_DIM
OFF_R, OFF_K, OFF_V = 0, RWKV_DIM, 2 * RWKV_DIM
OFF_G = 3 * RWKV_DIM
OFF_WF = OFF_G + G_LORA
OFF_WB = OFF_WF + W_LORA
OFF_AF = OFF_WB + W_LORA
OFF_AB = OFF_AF + A_LORA

LANES = 128
VMEM_LIMIT = 48 * 1024 * 1024

CHUNK = 64


def _dot(a, b):
    return jnp.dot(a, b, preferred_element_type=F32)


def _dot_nt(a, b):
    return lax.dot_general(a, b, (((1,), (1,)), ((), ())), preferred_element_type=F32)


def _split2(x):
    hi = x.astype(BF16)
    lo = (x - hi.astype(F32)).astype(BF16)
    return hi, lo


def _seg_sum(x, bd):
    width = bd.shape[0]
    return jnp.concatenate([_dot(x[:, j:j + width].astype(BF16), bd) for j in range(0, x.shape[1], width)], axis=1)


def _sigmoid(x):
    return 0.5 * jnp.tanh(0.5 * x) + 0.5


def _largest_divisor(n, candidates):
    for c in candidates:
        if n % c == 0:
            return c
    return 1


def _params(*sem):
    return pltpu.CompilerParams(dimension_semantics=sem, vmem_limit_bytes=VMEM_LIMIT)


def _inproj_kernel(x_ref, xp_ref, xn_ref, g_ref, w_ref, mup_ref, mun_ref, ur_ref, qkv_ref, gate_ref, *,
                   col_chunk, tiles_per_seq):
    i = pl.program_id(0)
    tm = x_ref.shape[0]
    halo = xp_ref.shape[0]
    x = jnp.concatenate([xp_ref[...], x_ref[...], xn_ref[...]], axis=0)
    h = x * lax.rsqrt(jnp.mean(x * x, axis=-1, keepdims=True) + NORM_EPS) * g_ref[...]
    row = lax.broadcasted_iota(jnp.int32, (tm + 2 * halo, 1), 0)
    pos = i % tiles_per_seq
    lo = jnp.where(pos == 0, halo, 0)
    hi = jnp.where(pos == tiles_per_seq - 1, tm + halo, tm + 2 * halo)
    h = jnp.where((row < lo) | (row >= hi), 0.0, h)
    hb_all = h.astype(BF16)
    hb = h[halo:halo + tm].astype(BF16)
    for j in range(0, RWKV_COLS, col_chunk):
        w = min(col_chunk, RWKV_COLS - j)
        u = _dot(hb_all, w_ref[:, j:j + w])
        prev = pltpu.roll(u, 1, axis=0)
        nxt = pltpu.roll(u, tm + 2 * halo - 1, axis=0)
        us = u + mup_ref[:, j:j + w] * (prev - u) + mun_ref[:, j:j + w] * (nxt - u)
        ur_ref[:, j:j + w] = us[halo:halo + tm].astype(BF16)
    c0 = RWKV_COLS
    for ref in (qkv_ref, gate_ref):
        width = ref.shape[-1]
        for j in range(0, width, col_chunk):
            w = min(col_chunk, width - j)
            ref[:, j:j + w] = _dot(hb, w_ref[:, c0 + j:c0 + j + w]).astype(BF16)
        c0 += width


def _in_projection(x2, g, w_in_bf, mu_prev, mu_next, tm, seq_len):
    n, d = x2.shape
    cols = w_in_bf.shape[1]
    gate_cols = cols - RWKV_COLS - QKV_COLS
    halo = 8
    hb = tm // halo
    n_halo = n // halo
    return pl.pallas_call(
        functools.partial(_inproj_kernel, col_chunk=512, tiles_per_seq=seq_len // tm),
        grid=(n // tm,),
        in_specs=[pl.BlockSpec((tm, d), lambda i: (i, 0)),
                  pl.BlockSpec((halo, d), lambda i: (jnp.maximum(i * hb - 1, 0), 0)),
                  pl.BlockSpec((halo, d), lambda i: (jnp.minimum((i + 1) * hb, n_halo - 1), 0)),
                  pl.BlockSpec((1, d), lambda i: (0, 0)),
                  pl.BlockSpec((d, cols), lambda i: (0, 0)),
                  pl.BlockSpec((1, RWKV_COLS), lambda i: (0, 0)),
                  pl.BlockSpec((1, RWKV_COLS), lambda i: (0, 0))],
        out_specs=[pl.BlockSpec((tm, RWKV_COLS), lambda i: (i, 0)),
                   pl.BlockSpec((tm, QKV_COLS), lambda i: (i, 0)),
                   pl.BlockSpec((tm, gate_cols), lambda i: (i, 0))],
        out_shape=[jax.ShapeDtypeStruct((n, RWKV_COLS), BF16),
                   jax.ShapeDtypeStruct((n, QKV_COLS), BF16),
                   jax.ShapeDtypeStruct((n, gate_cols), BF16)],
        compiler_params=_params("parallel"),
    )(x2, x2, x2, g.reshape(1, d), w_in_bf, mu_prev.reshape(1, -1).astype(F32), mu_next.reshape(1, -1).astype(F32))


GROUP_HEADS = 4
GROUP = GROUP_HEADS * RWKV_HEAD
N_GROUPS = RWKV_HEADS // GROUP_HEADS
ROWS_PER_STEP = 256
SEQS_PER_STEP = 2


def _rwkv_kernel(*refs, reverse, final):
    if final:
        (u_ref, w0_ref, wup_ref, a0_ref, aup_ref, kk_ref, ka_ref,
         bd_ref, tri_ref, a0o_ref, aupo_ref, gup_ref, rk_ref, lnw_ref, lnb_ref, yb_ref,
         o_ref, st_ref) = refs
    else:
        (u_ref, w0_ref, wup_ref, a0_ref, aup_ref, kk_ref, ka_ref,
         bd_ref, tri_ref, o_ref, st_ref) = refs
    NB, TB = u_ref.shape[0], u_ref.shape[1]
    L = CHUNK
    N = RWKV_HEAD
    n_ck = TB // L
    step = pl.program_id(1)

    @pl.when(step == 0)
    def _():
        st_ref[...] = jnp.zeros_like(st_ref)

    bd = bd_ref[...]
    tri = tri_ref[...]

    def prep(bi):
        col = lambda off, width: u_ref[bi, :, off:off + width].astype(F32)
        r = col(OFF_R, RWKV_DIM)
        k = col(OFF_K, RWKV_DIM)
        v = col(OFF_V, RWKV_DIM)
        w_lo = col(OFF_WB if reverse else OFF_WF, W_LORA)
        a_lo = col(OFF_AB if reverse else OFF_AF, A_LORA)
        lw = -DECAY_SCALE * _sigmoid(w0_ref[...] + _dot(jnp.tanh(w_lo).astype(BF16), wup_ref[...]))
        a = _sigmoid(a0_ref[...] + _dot(a_lo.astype(BF16), aup_ref[...]))
        kkr = k * kk_ref[...]
        kk = kkr * lax.rsqrt(jnp.maximum(_seg_sum(kkr * kkr, bd), 1e-24))
        kd = k * (1.0 + (a - 1.0) * ka_ref[...])
        b = a * kk
        l_hi, l_lo = _split2(lw)
        cs = _dot(tri, l_hi) + _dot(tri, l_lo)
        edge = 0 if reverse else L - 1
        e_tot = [jnp.exp(cs[c * L + edge:c * L + edge + 1]) for c in range(n_ck)]
        e_neg = jnp.exp(-cs)
        e_rem = jnp.concatenate([e_neg[c * L:(c + 1) * L] * e_tot[c] for c in range(n_ck)], axis=0)
        return dict(r=r, k=k, v=v, kd=kd, col=col, e_tot=e_tot,
                    a_t=(kk * jnp.exp(cs - lw)).astype(BF16), r_t=(r * jnp.exp(cs)).astype(BF16),
                    k_t=(kd * e_neg).astype(BF16), b_t=(b * e_neg).astype(BF16),
                    k_h=(kd * e_rem).astype(BF16), b_hf=-b * e_rem)

    pre = [prep(bi) for bi in range(NB)]

    ii = lax.broadcasted_iota(jnp.int32, (GROUP, GROUP), 0)
    jj = lax.broadcasted_iota(jnp.int32, (GROUP, GROUP), 1)
    same_head = (ii // N) == (jj // N)
    ti = lax.broadcasted_iota(jnp.int32, (L, GROUP), 0)
    si = lax.broadcasted_iota(jnp.int32, (L, GROUP), 1) % N
    if reverse:
        strict = si > ti
        incl = si >= ti
    else:
        strict = si < ti
        incl = si <= ti
    eye = (si == ti).astype(F32)
    level_masks = []
    s = 1
    while s < L:
        same = (ti // (2 * s)) == (si // (2 * s))
        t_hi = (ti & s) != 0
        s_hi = (si & s) != 0
        if reverse:
            level_masks.append(same & jnp.logical_not(t_hi) & s_hi)
        else:
            level_masks.append(same & t_hi & jnp.logical_not(s_hi))
        s *= 2

    def bdiag(xp):
        return jnp.where(same_head, jnp.concatenate([xp.astype(BF16)] * GROUP_HEADS, axis=0), 0.0)

    def head_transpose(xp):
        return jnp.concatenate([xp[:, h * N:(h + 1) * N].T for h in range(GROUP_HEADS)], axis=1)

    order = list(range(n_ck))[::-1] if reverse else list(range(n_ck))
    cgs = [(bi, c, g) for bi in range(NB) for c in order for g in range(N_GROUPS)]

    def tile(name, cg):
        bi, c, g = cg
        return pre[bi][name][c * L:(c + 1) * L, g * GROUP:(g + 1) * GROUP]

    ar = {cg: jnp.concatenate([tile('a_t', cg), tile('r_t', cg)], axis=0) for cg in cgs}
    arb = {cg: _dot_nt(ar[cg], bdiag(tile('b_t', cg))) for cg in cgs}
    ark = {cg: _dot_nt(ar[cg], bdiag(tile('k_t', cg))) for cg in cgs}
    ab = {cg: arb[cg][0:L] for cg in cgs}
    rb = {cg: arb[cg][L:2 * L] for cg in cgs}
    ak = {cg: ark[cg][0:L] for cg in cgs}
    rk_ = {cg: ark[cg][L:2 * L] for cg in cgs}
    x = {cg: eye - jnp.where(level_masks[0], ab[cg], 0.0) for cg in cgs}
    for m in level_masks[1:]:
        xn = {cg: _dot(x[cg].astype(BF16), bdiag(jnp.where(m, ab[cg], 0.0))).astype(BF16) for cg in cgs}
        x = {cg: x[cg] - _dot(xn[cg], bdiag(x[cg])) for cg in cgs}
    xb = {cg: x[cg].astype(BF16) for cg in cgs}
    xa = {cg: _dot(xb[cg], bdiag(tile('a_t', cg))) for cg in cgs}
    xa_w = {cg: bdiag(xa[cg]) for cg in cgs}
    mrb_n = {cg: jnp.where(incl, -rb[cg], 0.0).astype(BF16) for cg in cgs}
    v_w = {cg: bdiag(tile('v', cg)) for cg in cgs}
    mv = {cg: _dot(jnp.concatenate([jnp.where(strict, ak[cg], 0.0).astype(BF16),
                                    jnp.where(incl, rk_[cg], 0.0).astype(BF16)], axis=0), v_w[cg]) for cg in cgs}
    xmv = {cg: _dot(xb[cg], bdiag(mv[cg][0:L])) for cg in cgs}
    bt = {cg: head_transpose(tile('b_hf', cg)).astype(BF16) for cg in cgs}
    lhs2 = {cg: jnp.concatenate([mrb_n[cg], bt[cg]], axis=0) for cg in cgs}
    pa = {cg: _dot(lhs2[cg], xa_w[cg]) for cg in cgs}
    rq = {cg: (tile('r_t', cg).astype(F32) + pa[cg][0:L]).astype(BF16) for cg in cgs}
    pm = {cg: _dot(lhs2[cg], bdiag(xmv[cg])) for cg in cgs}
    dd = {cg: mv[cg][L:2 * L] + pm[cg][0:L] for cg in cgs}
    eye_b = eye.astype(BF16)
    c2 = {cg: _dot_nt(eye_b, bdiag(pm[cg][L:2 * L])) for cg in cgs}
    cc = {cg: _dot(head_transpose(tile('v', cg)).astype(BF16), bdiag(tile('k_h', cg))) + c2[cg] for cg in cgs}
    wt_w = {cg: bdiag(pa[cg][L:2 * L]) for cg in cgs}

    state = {(bi, g): st_ref[bi, g] for bi in range(NB) for g in range(N_GROUPS)}
    y_tiles = {}
    for c in order:
        for bi in range(NB):
            for g in range(N_GROUPS):
                cg = (bi, c, g)
                sb = state[bi, g].astype(BF16)
                decay = pre[bi]['e_tot'][c][:, g * GROUP:(g + 1) * GROUP]
                state[bi, g] = state[bi, g] * decay + _dot_nt(sb, wt_w[cg]) + cc[cg]
                y_tiles[cg] = _dot_nt(rq[cg], bdiag(sb)) + dd[cg]
    for key, val in state.items():
        st_ref[key] = val

    for bi in range(NB):
        wkv = jnp.concatenate(
            [jnp.concatenate([y_tiles[(bi, c, g)] for g in range(N_GROUPS)], axis=1) for c in range(n_ck)], axis=0)
        if not final:
            o_ref[bi] = wkv
            continue
        p = pre[bi]
        r, k, v, kd, col = p['r'], p['k'], p['v'], p['kd'], p['col']
        wkv = wkv + yb_ref[bi]
        inv_n = 1.0 / N
        mean = _seg_sum(wkv, bd) * inv_n
        xc = wkv - mean
        var = _seg_sum(xc * xc, bd) * inv_n
        normed = xc * lax.rsqrt(var + LNX_EPS) * lnw_ref[...] + lnb_ref[...]
        a_o = _sigmoid(a0o_ref[...] + _dot(col(OFF_AB, A_LORA).astype(BF16), aupo_ref[...]))
        kd_o = k * (1.0 + (a_o - 1.0) * ka_ref[...])
        bonus = _seg_sum(r * rk_ref[...] * (kd + kd_o), bd) * v
        gg = _dot(_sigmoid(col(OFF_G, G_LORA)).astype(BF16), gup_ref[...])
        o_ref[bi] = ((normed + bonus) * gg).astype(o_ref.dtype)


def _rwkv_pass(u_r, consts, *, reverse, final, extra=()):
    b, t, _ = u_r.shape
    tb = _largest_divisor(t, (ROWS_PER_STEP, CHUNK))
    nb = _largest_divisor(b, (SEQS_PER_STEP,))
    n_steps = t // tb

    def sidx(s):
        return (n_steps - 1 - s) if reverse else s

    def full(shape):
        return pl.BlockSpec(shape, lambda bi, s: (0,) * len(shape))

    in_specs = [pl.BlockSpec((nb, tb, RWKV_COLS), lambda bi, s: (bi, sidx(s), 0))]
    args = [u_r]
    for arr in consts:
        in_specs.append(full(arr.shape))
        args.append(arr)
    if final:
        for arr in extra[:-1]:
            in_specs.append(full(arr.shape))
            args.append(arr)
        in_specs.append(pl.BlockSpec((nb, tb, RWKV_DIM), lambda bi, s: (bi, sidx(s), 0)))
        args.append(extra[-1])
    out_dtype = BF16 if final else F32
    return pl.pallas_call(
        functools.partial(_rwkv_kernel, reverse=reverse, final=final),
        grid=(b // nb, n_steps),
        in_specs=in_specs,
        out_specs=pl.BlockSpec((nb, tb, RWKV_DIM), lambda bi, s: (bi, sidx(s), 0)),
        out_shape=jax.ShapeDtypeStruct((b, t, RWKV_DIM), out_dtype),
        scratch_shapes=[pltpu.VMEM((nb, N_GROUPS, RWKV_HEAD, GROUP), F32)],
        compiler_params=_params("parallel", "arbitrary"),
    )(*args)


def _rwkv_branch(u_r, w0_f, w_up_f, w0_b, w_up_b, a0_f, a_up_f, a0_b, a_up_b,
                 g_up, k_k, k_a, r_k, ln_w, ln_b):
    t = u_r.shape[1]
    tb = _largest_divisor(t, (ROWS_PER_STEP, CHUNK))
    row = lambda a: a.reshape(1, -1).astype(F32)
    seg = np.arange(GROUP) // RWKV_HEAD
    bd = jnp.asarray(seg[:, None] == seg[None, :], BF16)
    ti = np.arange(tb)
    same_chunk = (ti[:, None] // CHUNK) == (ti[None, :] // CHUNK)
    tri_f = jnp.asarray(same_chunk & (ti[None, :] <= ti[:, None]), BF16)
    tri_b = jnp.asarray(same_chunk & (ti[None, :] >= ti[:, None]), BF16)
    common = lambda w0, wup, a0, aup, tri: [row(w0), wup.astype(BF16), row(a0), aup.astype(BF16), row(k_k), row(k_a),
                                            bd, tri]
    y_b = _rwkv_pass(u_r, common(w0_b, w_up_b, a0_b, a_up_b, tri_b), reverse=True, final=False)
    extra = [row(a0_b), a_up_b.astype(BF16), g_up.astype(BF16), row(r_k), row(ln_w), row(ln_b), y_b]
    return _rwkv_pass(u_r, common(w0_f, w_up_f, a0_f, a_up_f, tri_f), reverse=False, final=True, extra=extra)


Q_BLOCKS = 4


def _attn_kernel(sink_ref, *refs):
    bias_refs = refs[:Q_BLOCKS]
    q_ref, kp_ref, kc_ref, kn_ref, vp_ref, vc_ref, vn_ref, o_ref = refs[Q_BLOCKS:]
    span = BLOCK + 2 * WINDOW
    kk = jnp.concatenate([kp_ref[0], kc_ref[0], kn_ref[0]], axis=0)
    vv = jnp.concatenate([vp_ref[0], vc_ref[0], vn_ref[0]], axis=0)
    group = ATT_HEADS // ATT_KV_HEADS
    kv_lane = lax.broadcasted_iota(jnp.int32, (1, KV_DIM), 1) // ATT_HEAD
    v_ext = [jnp.where(kv_lane == kv, vv, 1.0) for kv in range(ATT_KV_HEADS)]
    qs = q_ref[0] * (ATT_HEAD ** -0.5)
    hsl = lambda h: slice(h * ATT_HEAD, (h + 1) * ATT_HEAD)
    qrows = lambda j: slice(j * BLOCK, (j + 1) * BLOCK)
    krows = lambda j: slice(j * BLOCK, j * BLOCK + span)
    jh = [(j, h) for j in range(Q_BLOCKS) for h in range(ATT_HEADS)]
    s = {c: _dot_nt(qs[qrows(c[0]), hsl(c[1])], kk[krows(c[0]), hsl(c[1] // group)]) + bias_refs[c[0]][0, c[1]]
         for c in jh}
    m = {c: jnp.maximum(jnp.max(s[c], axis=-1, keepdims=True), sink_ref[c[1]]) for c in jh}
    p = {c: jnp.exp(s[c] - m[c]).astype(BF16) for c in jh}
    pv = {c: _dot(p[c], v_ext[c[1] // group][krows(c[0])]) for c in jh}
    for j in range(Q_BLOCKS):
        outs = []
        for h in range(ATT_HEADS):
            kv = h // group
            other = (kv + 1) % ATT_KV_HEADS
            denom = pv[j, h][:, other * ATT_HEAD:other * ATT_HEAD + 1] + jnp.exp(sink_ref[h] - m[j, h])
            outs.append(pv[j, h][:, hsl(kv)] / denom)
        o_ref[0, qrows(j)] = jnp.concatenate(outs, axis=-1).astype(o_ref.dtype)


def _attention(qkv, sink):
    b, t, _ = qkv.shape
    nb = t // BLOCK
    assert nb % Q_BLOCKS == 0
    n_steps = nb // Q_BLOCKS
    rows = Q_BLOCKS * BLOCK
    span = BLOCK + 2 * WINDOW
    kcol = ATT_DIM // KV_DIM
    halo = lambda col, f: pl.BlockSpec((1, BLOCK, KV_DIM), lambda bi, i: (bi, f(i), col))
    prev = lambda i: jnp.maximum(i * Q_BLOCKS - 1, 0)
    nxt = lambda i: jnp.minimum((i + 1) * Q_BLOCKS, nb - 1)
    cur = lambda col: pl.BlockSpec((1, rows, KV_DIM), lambda bi, i: (bi, i, col))
    col = np.arange(span)[None, :]
    rel = np.abs(col - WINDOW - np.arange(BLOCK)[:, None])
    slopes = 2.0 ** (-8.0 * np.arange(1, ATT_HEADS + 1) / ATT_HEADS)
    base = np.where(rel[None] <= WINDOW, -slopes[:, None, None] * rel[None], NEG_INF)
    tables = []
    for variant in range(4):
        dead = ((col < WINDOW) & bool(variant & 1)) | ((col >= WINDOW + BLOCK) & bool(variant & 2))
        tables.append(np.where(dead[None], NEG_INF, base))
    bias = jnp.asarray(np.stack(tables).astype(np.float32))

    def bias_spec(j):
        variant = lambda i: (jnp.where(i * Q_BLOCKS + j == 0, 1, 0) + jnp.where(i * Q_BLOCKS + j == nb - 1, 2, 0))
        return pl.BlockSpec((1, ATT_HEADS, BLOCK, span), lambda bi, i: (variant(i), 0, 0, 0))

    return pl.pallas_call(
        _attn_kernel,
        grid=(b, n_steps),
        in_specs=[pl.BlockSpec(memory_space=pltpu.SMEM)] + [bias_spec(j) for j in range(Q_BLOCKS)]
        + [pl.BlockSpec((1, rows, ATT_DIM), lambda bi, i: (bi, i, 0)),
           halo(kcol, prev), cur(kcol), halo(kcol, nxt),
           halo(kcol + 1, prev), cur(kcol + 1), halo(kcol + 1, nxt)],
        out_specs=pl.BlockSpec((1, rows, ATT_DIM), lambda bi, i: (bi, i, 0)),
        out_shape=jax.ShapeDtypeStruct((b, t, ATT_DIM), BF16),
        compiler_params=_params("parallel", "parallel"),
    )(sink.astype(F32), *([bias] * Q_BLOCKS), qkv, qkv, qkv, qkv, qkv, qkv, qkv)


MERGE_SPLIT = 2


def _merge_kernel(yr_ref, ya_ref, gate_ref, x_ref, wpr_ref, wpa_ref, wo_ref, g_ref, wrt_ref,
                  x1_ref, h2_ref, aff_ref, afft_ref):
    tm, d = x_ref.shape
    part = tm // MERGE_SPLIT
    for j in range(MERGE_SPLIT):
        rs = slice(j * part, (j + 1) * part)
        gl_r = gate_ref[rs, 0:d].astype(F32)
        gl_a = gate_ref[rs, d:2 * d].astype(F32)
        merged = (_sigmoid(gl_r) * _dot(yr_ref[rs, :], wpr_ref[...])
                  + _sigmoid(gl_a) * _dot(ya_ref[rs, :], wpa_ref[...]))
        x1 = x_ref[rs, :] + _dot(merged.astype(BF16), wo_ref[...])
        x1_ref[rs, :] = x1
        h2 = (x1 * lax.rsqrt(jnp.mean(x1 * x1, axis=-1, keepdims=True) + NORM_EPS) * g_ref[...]).astype(BF16)
        h2_ref[rs, :] = h2
        logits_t = _dot_nt(wrt_ref[...], h2)
        mx = jnp.max(logits_t, axis=0, keepdims=True)
        ex = jnp.exp(logits_t - mx)
        aff_t = ex / jnp.sum(ex, axis=0, keepdims=True)
        afft_ref[0, :, rs] = aff_t
        aff_ref[rs, :] = aff_t.T


def _merge(y_r, y_a, gates, x2, wpr, wpa, wo, g, w_router, tm, seq_len):
    n, d = x2.shape
    e = w_router.shape[1]
    tiles_per_seq = seq_len // tm
    full = lambda shape: pl.BlockSpec(shape, lambda i: (0,) * len(shape))
    rows = lambda w: pl.BlockSpec((tm, w), lambda i: (i, 0))
    return pl.pallas_call(
        _merge_kernel,
        grid=(n // tm,),
        in_specs=[rows(RWKV_DIM), rows(ATT_DIM), rows(2 * d), rows(d),
                  full(wpr.shape), full(wpa.shape), full(wo.shape), full((1, d)), full((e, d))],
        out_specs=[rows(d), rows(d), rows(e),
                   pl.BlockSpec((1, e, tm), lambda i: (i // tiles_per_seq, 0, i % tiles_per_seq))],
        out_shape=[jax.ShapeDtypeStruct((n, d), F32), jax.ShapeDtypeStruct((n, d), BF16),
                   jax.ShapeDtypeStruct((n, e), F32),
                   jax.ShapeDtypeStruct((n // seq_len, e, seq_len), F32)],
        compiler_params=_params("parallel"),
    )(y_r, y_a, gates, x2, wpr.astype(BF16), wpa.astype(BF16), wo.astype(BF16), g.reshape(1, d).astype(F32),
      w_router.T.astype(BF16))


def _select_kernel(a_ref, gsame_ref, gprev_ref, triu_ref, slot_ref, start_ref, *, cap):
    nb, rows, _ = a_ref.shape
    affs = [a_ref[i] for i in range(nb)]
    gsame = gsame_ref[...]
    gprev = gprev_ref[...]
    triu = triu_ref[...]
    capf = float(cap)

    def group_count(mask):
        per_lane = _dot(gsame, jnp.where(mask, 1.0, 0.0).astype(BF16))
        return jnp.sum(per_lane, axis=-1, keepdims=True)

    def search(i, prefixes):
        out = []
        for aff, prefix in zip(affs, prefixes):
            cand = prefix | lax.shift_left(jnp.int32(1), 30 - i)
            out.append(jnp.where(group_count(aff >= pltpu.bitcast(cand, F32)) >= capf, cand, prefix))
        return tuple(out)

    thr_bits = lax.fori_loop(0, 31, search, tuple(jnp.zeros((rows, 1), jnp.int32) for _ in range(nb)))

    def prefix_count(mask):
        mb = jnp.where(mask, 1.0, 0.0)
        inc = _dot(mb.astype(BF16), triu)
        tot = jnp.broadcast_to(inc[:, LANES - 1:LANES], inc.shape).astype(BF16)
        before = _dot(gprev, tot)
        return inc - mb + before, before

    for i, aff in enumerate(affs):
        thr = pltpu.bitcast(thr_bits[i], F32)
        gt = aff > thr
        eq = aff == thr
        need = capf - group_count(gt)
        sel = gt | (eq & (prefix_count(eq)[0] < need))
        pos, before = prefix_count(sel)
        slot_ref[i] = jnp.where(sel, pos, -1.0).astype(jnp.int32)
        start_ref[i] = before.astype(jnp.int32)


def _select(aff_t, cap):
    b, e, t = aff_t.shape
    tiles = t // LANES
    rows = e * tiles
    nb = _largest_divisor(b, (4, 2))
    rid = np.arange(rows)
    same = (rid[:, None] // tiles) == (rid[None, :] // tiles)
    gsame = jnp.asarray(same, BF16)
    gprev = jnp.asarray(same & (rid[None, :] < rid[:, None]), BF16)
    li = np.arange(LANES)
    triu = jnp.asarray(li[:, None] <= li[None, :], BF16)
    full = lambda shape: pl.BlockSpec(shape, lambda bi: (0,) * len(shape))
    slot, start = pl.pallas_call(
        functools.partial(_select_kernel, cap=cap),
        grid=(b // nb,),
        in_specs=[pl.BlockSpec((nb, rows, LANES), lambda bi: (bi, 0, 0)),
                  full((rows, rows)), full((rows, rows)), full((LANES, LANES))],
        out_specs=[pl.BlockSpec((nb, rows, LANES), lambda bi: (bi, 0, 0))] * 2,
        out_shape=[jax.ShapeDtypeStruct((b, rows, LANES), jnp.int32)] * 2,
        compiler_params=_params("parallel"),
    )(aff_t.reshape(b, rows, LANES), gsame, gprev, triu)
    return slot.reshape(b, e, t), start[:, :, 0].reshape(b, e, tiles)


ROUTE_TILE = 256
ROUTE_WIN = 64
WIN_ALIGN = 16


def _route_tables(start128, cap):
    per = ROUTE_TILE // LANES
    start = start128[:, :, ::per]
    nxt = jnp.concatenate([start[:, :, 1:], jnp.full_like(start[:, :, :1], cap)], axis=-1)
    w0 = jnp.minimum((start // WIN_ALIGN) * WIN_ALIGN, cap - ROUTE_WIN)
    fits = jnp.all(nxt - w0 <= ROUTE_WIN, axis=1)
    return w0.reshape(-1).astype(jnp.int32), fits.reshape(-1).astype(jnp.int32)


def _gather_kernel(w0_ref, fits_ref, slot_ref, h_ref, xs_ref, *, col_chunk):
    n_exp, cap, d = xs_ref.shape[1], xs_ref.shape[2], xs_ref.shape[3]
    nt = h_ref.shape[1] // ROUTE_TILE
    bi = pl.program_id(0)
    xs_ref[...] = jnp.zeros_like(xs_ref)
    ridx = lax.broadcasted_iota(jnp.int32, (ROUTE_WIN, 1), 0)
    cidx = lax.broadcasted_iota(jnp.int32, (cap, 1), 0)
    for j in range(nt):
        tok = slice(j * ROUTE_TILE, (j + 1) * ROUTE_TILE)
        fit = fits_ref[bi * nt + j]

        @pl.when(fit != 0)
        def _():
            w0 = [pl.multiple_of(w0_ref[(bi * n_exp + e) * nt + j], WIN_ALIGN) for e in range(n_exp)]
            onehot = jnp.concatenate(
                [jnp.where(slot_ref[0, e:e + 1, tok] - w0[e] == ridx, 1.0, 0.0).astype(BF16) for e in range(n_exp)],
                axis=0)
            for c0 in range(0, d, col_chunk):
                rows = _dot(onehot, h_ref[0, tok, c0:c0 + col_chunk]).astype(BF16)
                for e in range(n_exp):
                    win = (0, e, pl.ds(w0[e], ROUTE_WIN), slice(c0, c0 + col_chunk))
                    xs_ref[win] = xs_ref[win] + rows[e * ROUTE_WIN:(e + 1) * ROUTE_WIN]

        @pl.when(fit == 0)
        def _():
            for e in range(n_exp):
                onehot = jnp.where(slot_ref[0, e:e + 1, tok] == cidx, 1.0, 0.0).astype(BF16)
                xs_ref[0, e] = xs_ref[0, e] + _dot(onehot, h_ref[0, tok, :]).astype(BF16)


def _gather(w0, fits, slot, h2, cap):
    b, t, d = h2.shape
    e = slot.shape[1]
    return pl.pallas_call(
        functools.partial(_gather_kernel, col_chunk=512),
        grid_spec=pltpu.PrefetchScalarGridSpec(
            num_scalar_prefetch=2,
            grid=(b,),
            in_specs=[pl.BlockSpec((1, e, t), lambda bi, *_: (bi, 0, 0)),
                      pl.BlockSpec((1, t, d), lambda bi, *_: (bi, 0, 0))],
            out_specs=pl.BlockSpec((1, e, cap, d), lambda bi, *_: (bi, 0, 0, 0))),
        out_shape=jax.ShapeDtypeStruct((b, e, cap, d), BF16),
        compiler_params=_params("parallel"),
    )(w0, fits, slot, h2)


def _expert_kernel(xs_ref, wg_ref, wu_ref, wd_ref, ys_ref):
    bb, _, cap, d = xs_ref.shape
    xs = xs_ref[...].reshape(bb * cap, d)
    gate = _dot(xs, wg_ref[0])
    up = _dot(xs, wu_ref[0])
    hid = (gate * _sigmoid(gate) * up).astype(BF16)
    ys_ref[...] = _dot(hid, wd_ref[0]).astype(ys_ref.dtype).reshape(bb, 1, cap, d)


def _experts(xs, wg, wu, wd, bb):
    b, e, cap, d = xs.shape
    ff = wg.shape[-1]
    tok = pl.BlockSpec((bb, 1, cap, d), lambda ei, bi: (bi, ei, 0, 0))
    return pl.pallas_call(
        _expert_kernel,
        grid=(e, b // bb),
        in_specs=[tok,
                  pl.BlockSpec((1, d, ff), lambda ei, bi: (ei, 0, 0)),
                  pl.BlockSpec((1, d, ff), lambda ei, bi: (ei, 0, 0)),
                  pl.BlockSpec((1, ff, d), lambda ei, bi: (ei, 0, 0))],
        out_specs=tok,
        out_shape=jax.ShapeDtypeStruct((b, e, cap, d), BF16),
        compiler_params=_params("parallel", "arbitrary"),
    )(xs, wg, wu, wd)


def _scatter_kernel(w0_ref, fits_ref, slot_ref, aff_ref, expand_ref, ys_ref, x1_ref, g_ref, o_ref, *, final_norm):
    n_exp, cap = ys_ref.shape[1], ys_ref.shape[2]
    bi, j = pl.program_id(0), pl.program_id(1)
    nt = pl.num_programs(1)
    slot = slot_ref[0]
    aff = aff_ref[0]
    fit = fits_ref[bi * nt + j]

    def finish(acc):
        if final_norm:
            acc = acc * lax.rsqrt(jnp.mean(acc * acc, axis=-1, keepdims=True) + NORM_EPS) * g_ref[...]
        o_ref[0] = acc

    @pl.when(fit != 0)
    def _():
        w0 = [pl.multiple_of(w0_ref[(bi * n_exp + e) * nt + j], WIN_ALIGN) for e in range(n_exp)]
        lane = lax.broadcasted_iota(jnp.int32, (1, n_exp * ROUTE_WIN), 1)
        w0_row = jnp.zeros((1, n_exp * ROUTE_WIN), jnp.int32)
        for e in range(n_exp):
            w0_row = jnp.where(lane // ROUTE_WIN == e, w0[e], w0_row)
        target = (w0_row + lane % ROUTE_WIN).astype(F32)
        slot_l = _dot(slot.astype(F32).astype(BF16), expand_ref[...])
        aff_l = _dot(aff.astype(BF16), expand_ref[...])
        onehot = jnp.where(slot_l == target, aff_l, 0.0).astype(BF16)
        rows = jnp.concatenate([ys_ref[0, e, pl.ds(w0[e], ROUTE_WIN), :] for e in range(n_exp)], axis=0)
        finish(x1_ref[0] + _dot(onehot, rows))

    @pl.when(fit == 0)
    def _():
        cidx = lax.broadcasted_iota(jnp.int32, (1, cap), 1)
        acc = x1_ref[0]
        for e in range(n_exp):
            onehot = jnp.where(slot[:, e:e + 1] == cidx, aff[:, e:e + 1], 0.0).astype(BF16)
            acc = acc + _dot(onehot, ys_ref[0, e])
        finish(acc)


def _scatter(w0, fits, slot_cols, aff, ys, x1, g, final_norm):
    b, t, d = x1.shape
    e, cap = ys.shape[1], ys.shape[2]
    assert cap <= 256, "slot ids must stay exact in bf16"
    lane_exp = np.arange(e * ROUTE_WIN) // ROUTE_WIN
    expand = jnp.asarray(np.arange(e)[:, None] == lane_exp[None, :], BF16)
    return pl.pallas_call(
        functools.partial(_scatter_kernel, final_norm=final_norm),
        grid_spec=pltpu.PrefetchScalarGridSpec(
            num_scalar_prefetch=2,
            grid=(b, t // ROUTE_TILE),
            in_specs=[pl.BlockSpec((1, ROUTE_TILE, e), lambda bi, ti, *_: (bi, ti, 0)),
                      pl.BlockSpec((1, ROUTE_TILE, e), lambda bi, ti, *_: (bi, ti, 0)),
                      pl.BlockSpec((e, e * ROUTE_WIN), lambda bi, ti, *_: (0, 0)),
                      pl.BlockSpec((1, e, cap, d), lambda bi, ti, *_: (bi, 0, 0, 0)),
                      pl.BlockSpec((1, ROUTE_TILE, d), lambda bi, ti, *_: (bi, ti, 0)),
                      pl.BlockSpec((1, d), lambda bi, ti, *_: (0, 0))],
            out_specs=pl.BlockSpec((1, ROUTE_TILE, d), lambda bi, ti, *_: (bi, ti, 0))),
        out_shape=jax.ShapeDtypeStruct((b, t, d), F32),
        compiler_params=_params("parallel", "arbitrary"),
    )(w0, fits, slot_cols, aff, expand, ys, x1, g.reshape(1, d).astype(F32))


def kernel(x, norm_mix_g, w_in, mu_prev, mu_next, w0_f, w_up_f, w0_b, w_up_b, a0_f, a_up_f, a0_b, a_up_b, g_up, k_k, k_a, r_k, ln_x_w, ln_x_b, attn_sink, w_proj_rwkv, w_proj_attn, w_out, norm_ffn_g, w_router, exp_w_gate, exp_w_up, exp_w_down, norm_final_g):
    b, t, d = x.shape
    depth = w_in.shape[0]
    n = b * t
    tm = _largest_divisor(t, (512, 256, 128))
    cap = CAPACITY_FACTOR * t // N_EXPERTS
    bb = _largest_divisor(b, (4, 2))
    xc = x
    for l in range(depth):
        x2 = xc.reshape(n, d)
        u_r, qkv, gates = _in_projection(x2, norm_mix_g[l].astype(F32), w_in[l].astype(BF16), mu_prev[l], mu_next[l],
                                         tm, t)
        y_r = _rwkv_branch(u_r.reshape(b, t, RWKV_COLS), w0_f[l], w_up_f[l], w0_b[l],
                           w_up_b[l], a0_f[l], a_up_f[l], a0_b[l], a_up_b[l], g_up[l], k_k[l], k_a[l], r_k[l],
                           ln_x_w[l], ln_x_b[l])
        y_a = _attention(qkv.reshape(b, t, QKV_COLS), attn_sink[l])
        last = l == depth - 1
        x1, h2, aff, aff_t = _merge(y_r.reshape(n, RWKV_DIM), y_a.reshape(n, ATT_DIM), gates, x2, w_proj_rwkv[l],
                                    w_proj_attn[l], w_out[l], norm_ffn_g[l], w_router[l],
                                    _largest_divisor(t, (1024, 512, 256)), t)
        slot, start128 = _select(aff_t, cap)
        w0, fits = _route_tables(start128, cap)
        xs = _gather(w0, fits, slot, h2.reshape(b, t, d), cap)
        ys = _experts(xs, exp_w_gate[l].astype(BF16), exp_w_up[l].astype(BF16), exp_w_down[l].astype(BF16), bb)
        xc = _scatter(w0, fits, jnp.swapaxes(slot, 1, 2), aff.reshape(b, t, N_EXPERTS), ys, x1.reshape(b, t, d),
                      norm_final_g, final_norm=last)
    return xc
```

```python
import functools
import math

import numpy as np
import jax
import jax.numpy as jnp
from jax import lax
from jax.experimental import pallas as pl
from jax.experimental.pallas import tpu as pltpu

F32 = jnp.float32
BF16 = jnp.bfloat16

RWKV_HEAD = 64
RWKV_HEADS = 8
RWKV_DIM = RWKV_HEADS * RWKV_HEAD
W_LORA = 64
A_LORA = 64
G_LORA = 128
DECAY_SCALE = math.exp(-0.5)
LNX_EPS = 64e-5
ATT_HEADS = 8
ATT_KV_HEADS = 2
ATT_HEAD = 64
ATT_DIM = ATT_HEADS * ATT_HEAD
KV_DIM = ATT_KV_HEADS * ATT_HEAD
WINDOW = 128
BLOCK = 128
NEG_INF = -1e30
N_EXPERTS = 16
CAPACITY_FACTOR = 2
NORM_EPS = 1e-6
RWKV_COLS = 3 * RWKV_DIM + G_LORA + 2 * W_LORA + 2 * A_LORA
QKV_COLS = ATT_DIM + 2 * KV_DIM
OFF_R, OFF_K, OFF_V = 0, RWKV_DIM, 2 * RWKV_DIM
OFF_G = 3 * RWKV_DIM
OFF_WF = OFF_G + G_LORA
OFF_WB = OFF_WF + W_LORA
OFF_AF = OFF_WB + W_LORA
OFF_AB = OFF_AF + A_LORA

LANES = 128
VMEM_LIMIT = 48 * 1024 * 1024

CHUNK = 64


def _dot(a, b):
    return jnp.dot(a, b, preferred_element_type=F32)


def _dot_nt(a, b):
    return lax.dot_general(a, b, (((1,), (1,)), ((), ())), preferred_element_type=F32)


def _split2(x):
    hi = x.astype(BF16)
    lo = (x - hi.astype(F32)).astype(BF16)
    return hi, lo


def _seg_sum(x, bd):
    width = bd.shape[0]
    return jnp.concatenate([_dot(x[:, j:j + width].astype(BF16), bd) for j in range(0, x.shape[1], width)], axis=1)


def _sigmoid(x):
    return 0.5 * jnp.tanh(0.5 * x) + 0.5


def _largest_divisor(n, candidates):
    for c in candidates:
        if n % c == 0:
            return c
    return 1


def _params(*sem):
    return pltpu.CompilerParams(dimension_semantics=sem, vmem_limit_bytes=VMEM_LIMIT)


def _inproj_kernel(x_ref, xp_ref, xn_ref, g_ref, w_ref, mup_ref, mun_ref, ur_ref, qkv_ref, gate_ref, *,
                   col_chunk, tiles_per_seq):
    i = pl.program_id(0)
    tm = x_ref.shape[0]
    halo = xp_ref.shape[0]
    x = jnp.concatenate([xp_ref[...], x_ref[...], xn_ref[...]], axis=0)
    h = x * lax.rsqrt(jnp.mean(x * x, axis=-1, keepdims=True) + NORM_EPS) * g_ref[...]
    row = lax.broadcasted_iota(jnp.int32, (tm + 2 * halo, 1), 0)
    pos = i % tiles_per_seq
    lo = jnp.where(pos == 0, halo, 0)
    hi = jnp.where(pos == tiles_per_seq - 1, tm + halo, tm + 2 * halo)
    h = jnp.where((row < lo) | (row >= hi), 0.0, h)
    hb_all = h.astype(BF16)
    hb = h[halo:halo + tm].astype(BF16)
    for j in range(0, RWKV_COLS, col_chunk):
        w = min(col_chunk, RWKV_COLS - j)
        u = _dot(hb_all, w_ref[:, j:j + w])
        prev = pltpu.roll(u, 1, axis=0)
        nxt = pltpu.roll(u, tm + 2 * halo - 1, axis=0)
        us = u + mup_ref[:, j:j + w] * (prev - u) + mun_ref[:, j:j + w] * (nxt - u)
        ur_ref[:, j:j + w] = us[halo:halo + tm].astype(BF16)
    c0 = RWKV_COLS
    for ref in (qkv_ref, gate_ref):
        width = ref.shape[-1]
        for j in range(0, width, col_chunk):
            w = min(col_chunk, width - j)
            ref[:, j:j + w] = _dot(hb, w_ref[:, c0 + j:c0 + j + w]).astype(BF16)
        c0 += width


def _in_projection(x2, g, w_in_bf, mu_prev, mu_next, tm, seq_len):
    n, d = x2.shape
    cols = w_in_bf.shape[1]
    gate_cols = cols - RWKV_COLS - QKV_COLS
    halo = 8
    hb = tm // halo
    n_halo = n // halo
    return pl.pallas_call(
        functools.partial(_inproj_kernel, col_chunk=512, tiles_per_seq=seq_len // tm),
        grid=(n // tm,),
        in_specs=[pl.BlockSpec((tm, d), lambda i: (i, 0)),
                  pl.BlockSpec((halo, d), lambda i: (jnp.maximum(i * hb - 1, 0), 0)),
                  pl.BlockSpec((halo, d), lambda i: (jnp.minimum((i + 1) * hb, n_halo - 1), 0)),
                  pl.BlockSpec((1, d), lambda i: (0, 0)),
                  pl.BlockSpec((d, cols), lambda i: (0, 0)),
                  pl.BlockSpec((1, RWKV_COLS), lambda i: (0, 0)),
                  pl.BlockSpec((1, RWKV_COLS), lambda i: (0, 0))],
        out_specs=[pl.BlockSpec((tm, RWKV_COLS), lambda i: (i, 0)),
                   pl.BlockSpec((tm, QKV_COLS), lambda i: (i, 0)),
                   pl.BlockSpec((tm, gate_cols), lambda i: (i, 0))],
        out_shape=[jax.ShapeDtypeStruct((n, RWKV_COLS), BF16),
                   jax.ShapeDtypeStruct((n, QKV_COLS), BF16),
                   jax.ShapeDtypeStruct((n, gate_cols), BF16)],
        compiler_params=_params("parallel"),
    )(x2, x2, x2, g.reshape(1, d), w_in_bf, mu_prev.reshape(1, -1).astype(F32), mu_next.reshape(1, -1).astype(F32))


GROUP_HEADS = 4
GROUP = GROUP_HEADS * RWKV_HEAD
N_GROUPS = RWKV_HEADS // GROUP_HEADS
ROWS_PER_STEP = 256
SEQS_PER_STEP = 2
CG_BATCH = 8


def _rwkv_kernel(*refs, reverse, final):
    if final:
        (u_ref, w0_ref, wup_ref, a0_ref, aup_ref, kk_ref, ka_ref,
         bd_ref, tri_ref, a0o_ref, aupo_ref, gup_ref, rk_ref, lnw_ref, lnb_ref, yb_ref,
         o_ref, st_ref) = refs
    else:
        (u_ref, w0_ref, wup_ref, a0_ref, aup_ref, kk_ref, ka_ref,
         bd_ref, tri_ref, o_ref, st_ref) = refs
    NB, TB = u_ref.shape[0], u_ref.shape[1]
    L = CHUNK
    N = RWKV_HEAD
    n_ck = TB // L
    step = pl.program_id(1)

    @pl.when(step == 0)
    def _():
        st_ref[...] = jnp.zeros_like(st_ref)

    bd = bd_ref[...]
    tri = tri_ref[...]

    def prep(bi):
        col = lambda off, width: u_ref[bi, :, off:off + width].astype(F32)
        r = col(OFF_R, RWKV_DIM)
        k = col(OFF_K, RWKV_DIM)
        v = col(OFF_V, RWKV_DIM)
        w_lo = col(OFF_WB if reverse else OFF_WF, W_LORA)
        a_lo = col(OFF_AB if reverse else OFF_AF, A_LORA)
        lw = -DECAY_SCALE * _sigmoid(w0_ref[...] + _dot(jnp.tanh(w_lo).astype(BF16), wup_ref[...]))
        a = _sigmoid(a0_ref[...] + _dot(a_lo.astype(BF16), aup_ref[...]))
        kkr = k * kk_ref[...]
        kk = kkr * lax.rsqrt(jnp.maximum(_seg_sum(kkr * kkr, bd), 1e-24))
        kd = k * (1.0 + (a - 1.0) * ka_ref[...])
        b = a * kk
        l_hi, l_lo = _split2(lw)
        cs = _dot(tri, l_hi) + _dot(tri, l_lo)
        edge = 0 if reverse else L - 1
        e_tot = [jnp.exp(cs[c * L + edge:c * L + edge + 1]) for c in range(n_ck)]
        e_neg = jnp.exp(-cs)
        e_rem = jnp.concatenate([e_neg[c * L:(c + 1) * L] * e_tot[c] for c in range(n_ck)], axis=0)
        return dict(r=r, k=k, v=v, kd=kd, col=col, e_tot=e_tot,
                    a_t=(kk * jnp.exp(cs - lw)).astype(BF16), r_t=(r * jnp.exp(cs)).astype(BF16),
                    k_t=(kd * e_neg).astype(BF16), b_t=(b * e_neg).astype(BF16),
                    k_h=(kd * e_rem).astype(BF16), b_hf=-b * e_rem)

    pre = [prep(bi) for bi in range(NB)]

    ii = lax.broadcasted_iota(jnp.int32, (GROUP, GROUP), 0)
    jj = lax.broadcasted_iota(jnp.int32, (GROUP, GROUP), 1)
    same_head = (ii // N) == (jj // N)
    ti = lax.broadcasted_iota(jnp.int32, (L, GROUP), 0)
    si = lax.broadcasted_iota(jnp.int32, (L, GROUP), 1) % N
    if reverse:
        strict = si > ti
        incl = si >= ti
    else:
        strict = si < ti
        incl = si <= ti
    eye = (si == ti).astype(F32)
    level_masks = []
    s = 1
    while s < L:
        same = (ti // (2 * s)) == (si // (2 * s))
        t_hi = (ti & s) != 0
        s_hi = (si & s) != 0
        if reverse:
            level_masks.append(same & jnp.logical_not(t_hi) & s_hi)
        else:
            level_masks.append(same & t_hi & jnp.logical_not(s_hi))
        s *= 2

    def bdiag(xp):
        return jnp.where(same_head, jnp.concatenate([xp.astype(BF16)] * GROUP_HEADS, axis=0), 0.0)

    def head_transpose(xp):
        return jnp.concatenate([xp[:, h * N:(h + 1) * N].T for h in range(GROUP_HEADS)], axis=1)

    order = list(range(n_ck))[::-1] if reverse else list(range(n_ck))

    def tile(name, cg):
        bi, c, g = cg
        return pre[bi][name][c * L:(c + 1) * L, g * GROUP:(g + 1) * GROUP]

    def chunk_operators(cgs):
        ar = {cg: jnp.concatenate([tile('a_t', cg), tile('r_t', cg)], axis=0) for cg in cgs}
        arb = {cg: _dot_nt(ar[cg], bdiag(tile('b_t', cg))) for cg in cgs}
        ark = {cg: _dot_nt(ar[cg], bdiag(tile('k_t', cg))) for cg in cgs}
        ab = {cg: arb[cg][0:L] for cg in cgs}
        rb = {cg: arb[cg][L:2 * L] for cg in cgs}
        ak = {cg: ark[cg][0:L] for cg in cgs}
        rk_ = {cg: ark[cg][L:2 * L] for cg in cgs}
        x = {cg: eye - jnp.where(level_masks[0], ab[cg], 0.0) for cg in cgs}
        for m in level_masks[1:]:
            xn = {cg: _dot(x[cg].astype(BF16), bdiag(jnp.where(m, ab[cg], 0.0))).astype(BF16) for cg in cgs}
            x = {cg: x[cg] - _dot(xn[cg], bdiag(x[cg])) for cg in cgs}
        xb = {cg: x[cg].astype(BF16) for cg in cgs}
        xa = {cg: _dot(xb[cg], bdiag(tile('a_t', cg))) for cg in cgs}
        xa_w = {cg: bdiag(xa[cg]) for cg in cgs}
        mrb_n = {cg: jnp.where(incl, -rb[cg], 0.0).astype(BF16) for cg in cgs}
        v_w = {cg: bdiag(tile('v', cg)) for cg in cgs}
        mv = {cg: _dot(jnp.concatenate([jnp.where(strict, ak[cg], 0.0).astype(BF16),
                                        jnp.where(incl, rk_[cg], 0.0).astype(BF16)], axis=0), v_w[cg]) for cg in cgs}
        xmv = {cg: _dot(xb[cg], bdiag(mv[cg][0:L])) for cg in cgs}
        bt = {cg: head_transpose(tile('b_hf', cg)).astype(BF16) for cg in cgs}
        lhs2 = {cg: jnp.concatenate([mrb_n[cg], bt[cg]], axis=0) for cg in cgs}
        pa = {cg: _dot(lhs2[cg], xa_w[cg]) for cg in cgs}
        rq = {cg: (tile('r_t', cg).astype(F32) + pa[cg][0:L]).astype(BF16) for cg in cgs}
        pm = {cg: _dot(lhs2[cg], bdiag(xmv[cg])) for cg in cgs}
        dd = {cg: mv[cg][L:2 * L] + pm[cg][0:L] for cg in cgs}
        eye_b = eye.astype(BF16)
        c2 = {cg: _dot_nt(eye_b, bdiag(pm[cg][L:2 * L])) for cg in cgs}
        cc = {cg: _dot(head_transpose(tile('v', cg)).astype(BF16), bdiag(tile('k_h', cg))) + c2[cg] for cg in cgs}
        wt_w = {cg: bdiag(pa[cg][L:2 * L]) for cg in cgs}
        return wt_w, cc, rq, dd

    all_cgs = [(bi, c, g) for bi in range(NB) for c in order for g in range(N_GROUPS)]
    wt_w, cc, rq, dd = {}, {}, {}, {}
    for i in range(0, len(all_cgs), CG_BATCH):
        for store, part in zip((wt_w, cc, rq, dd), chunk_operators(all_cgs[i:i + CG_BATCH])):
            store.update(part)

    state = {(bi, g): st_ref[bi, g] for bi in range(NB) for g in range(N_GROUPS)}
    y_tiles = {}
    for c in order:
        for bi in range(NB):
            for g in range(N_GROUPS):
                cg = (bi, c, g)
                sb = state[bi, g].astype(BF16)
                decay = pre[bi]['e_tot'][c][:, g * GROUP:(g + 1) * GROUP]
                state[bi, g] = state[bi, g] * decay + _dot_nt(sb, wt_w[cg]) + cc[cg]
                y_tiles[cg] = _dot_nt(rq[cg], bdiag(sb)) + dd[cg]
    for key, val in state.items():
        st_ref[key] = val

    for bi in range(NB):
        wkv = jnp.concatenate(
            [jnp.concatenate([y_tiles[(bi, c, g)] for g in range(N_GROUPS)], axis=1) for c in range(n_ck)], axis=0)
        if not final:
            o_ref[bi] = wkv
            continue
        p = pre[bi]
        r, k, v, kd, col = p['r'], p['k'], p['v'], p['kd'], p['col']
        wkv = wkv + yb_ref[bi]
        inv_n = 1.0 / N
        mean = _seg_sum(wkv, bd) * inv_n
        xc = wkv - mean
        var = _seg_sum(xc * xc, bd) * inv_n
        normed = xc * lax.rsqrt(var + LNX_EPS) * lnw_ref[...] + lnb_ref[...]
        a_o = _sigmoid(a0o_ref[...] + _dot(col(OFF_AB, A_LORA).astype(BF16), aupo_ref[...]))
        kd_o = k * (1.0 + (a_o - 1.0) * ka_ref[...])
        bonus = _seg_sum(r * rk_ref[...] * (kd + kd_o), bd) * v
        gg = _dot(_sigmoid(col(OFF_G, G_LORA)).astype(BF16), gup_ref[...])
        o_ref[bi] = ((normed + bonus) * gg).astype(o_ref.dtype)


def _rwkv_pass(u_r, consts, *, reverse, final, extra=()):
    b, t, _ = u_r.shape
    tb = _largest_divisor(t, (ROWS_PER_STEP, CHUNK))
    nb = _largest_divisor(b, (SEQS_PER_STEP,))
    n_steps = t // tb

    def sidx(s):
        return (n_steps - 1 - s) if reverse else s

    def full(shape):
        return pl.BlockSpec(shape, lambda bi, s: (0,) * len(shape))

    in_specs = [pl.BlockSpec((nb, tb, RWKV_COLS), lambda bi, s: (bi, sidx(s), 0))]
    args = [u_r]
    for arr in consts:
        in_specs.append(full(arr.shape))
        args.append(arr)
    if final:
        for arr in extra[:-1]:
            in_specs.append(full(arr.shape))
            args.append(arr)
        in_specs.append(pl.BlockSpec((nb, tb, RWKV_DIM), lambda bi, s: (bi, sidx(s), 0)))
        args.append(extra[-1])
    out_dtype = BF16 if final else F32
    return pl.pallas_call(
        functools.partial(_rwkv_kernel, reverse=reverse, final=final),
        grid=(b // nb, n_steps),
        in_specs=in_specs,
        out_specs=pl.BlockSpec((nb, tb, RWKV_DIM), lambda bi, s: (bi, sidx(s), 0)),
        out_shape=jax.ShapeDtypeStruct((b, t, RWKV_DIM), out_dtype),
        scratch_shapes=[pltpu.VMEM((nb, N_GROUPS, RWKV_HEAD, GROUP), F32)],
        compiler_params=_params("parallel", "arbitrary"),
    )(*args)


def _rwkv_branch(u_r, w0_f, w_up_f, w0_b, w_up_b, a0_f, a_up_f, a0_b, a_up_b,
                 g_up, k_k, k_a, r_k, ln_w, ln_b):
    t = u_r.shape[1]
    tb = _largest_divisor(t, (ROWS_PER_STEP, CHUNK))
    row = lambda a: a.reshape(1, -1).astype(F32)
    seg = np.arange(GROUP) // RWKV_HEAD
    bd = jnp.asarray(seg[:, None] == seg[None, :], BF16)
    ti = np.arange(tb)
    same_chunk = (ti[:, None] // CHUNK) == (ti[None, :] // CHUNK)
    tri_f = jnp.asarray(same_chunk & (ti[None, :] <= ti[:, None]), BF16)
    tri_b = jnp.asarray(same_chunk & (ti[None, :] >= ti[:, None]), BF16)
    common = lambda w0, wup, a0, aup, tri: [row(w0), wup.astype(BF16), row(a0), aup.astype(BF16), row(k_k), row(k_a),
                                            bd, tri]
    y_b = _rwkv_pass(u_r, common(w0_b, w_up_b, a0_b, a_up_b, tri_b), reverse=True, final=False)
    extra = [row(a0_b), a_up_b.astype(BF16), g_up.astype(BF16), row(r_k), row(ln_w), row(ln_b), y_b]
    return _rwkv_pass(u_r, common(w0_f, w_up_f, a0_f, a_up_f, tri_f), reverse=False, final=True, extra=extra)


Q_BLOCKS = 4


def _attn_kernel(sink_ref, *refs):
    bias_refs = refs[:Q_BLOCKS]
    q_ref, kp_ref, kc_ref, kn_ref, vp_ref, vc_ref, vn_ref, o_ref = refs[Q_BLOCKS:]
    span = BLOCK + 2 * WINDOW
    kk = jnp.concatenate([kp_ref[0], kc_ref[0], kn_ref[0]], axis=0)
    vv = jnp.concatenate([vp_ref[0], vc_ref[0], vn_ref[0]], axis=0)
    group = ATT_HEADS // ATT_KV_HEADS
    kv_lane = lax.broadcasted_iota(jnp.int32, (1, KV_DIM), 1) // ATT_HEAD
    v_ext = [jnp.where(kv_lane == kv, vv, 1.0) for kv in range(ATT_KV_HEADS)]
    qs = q_ref[0] * (ATT_HEAD ** -0.5)
    hsl = lambda h: slice(h * ATT_HEAD, (h + 1) * ATT_HEAD)
    qrows = lambda j: slice(j * BLOCK, (j + 1) * BLOCK)
    krows = lambda j: slice(j * BLOCK, j * BLOCK + span)
    jh = [(j, h) for j in range(Q_BLOCKS) for h in range(ATT_HEADS)]
    s = {c: _dot_nt(qs[qrows(c[0]), hsl(c[1])], kk[krows(c[0]), hsl(c[1] // group)]) + bias_refs[c[0]][0, c[1]]
         for c in jh}
    m = {c: jnp.maximum(jnp.max(s[c], axis=-1, keepdims=True), sink_ref[c[1]]) for c in jh}
    p = {c: jnp.exp(s[c] - m[c]).astype(BF16) for c in jh}
    pv = {c: _dot(p[c], v_ext[c[1] // group][krows(c[0])]) for c in jh}
    for j in range(Q_BLOCKS):
        outs = []
        for h in range(ATT_HEADS):
            kv = h // group
            other = (kv + 1) % ATT_KV_HEADS
            denom = pv[j, h][:, other * ATT_HEAD:other * ATT_HEAD + 1] + jnp.exp(sink_ref[h] - m[j, h])
            outs.append(pv[j, h][:, hsl(kv)] / denom)
        o_ref[0, qrows(j)] = jnp.concatenate(outs, axis=-1).astype(o_ref.dtype)


def _attention(qkv, sink):
    b, t, _ = qkv.shape
    nb = t // BLOCK
    assert nb % Q_BLOCKS == 0
    n_steps = nb // Q_BLOCKS
    rows = Q_BLOCKS * BLOCK
    span = BLOCK + 2 * WINDOW
    kcol = ATT_DIM // KV_DIM
    halo = lambda col, f: pl.BlockSpec((1, BLOCK, KV_DIM), lambda bi, i: (bi, f(i), col))
    prev = lambda i: jnp.maximum(i * Q_BLOCKS - 1, 0)
    nxt = lambda i: jnp.minimum((i + 1) * Q_BLOCKS, nb - 1)
    cur = lambda col: pl.BlockSpec((1, rows, KV_DIM), lambda bi, i: (bi, i, col))
    col = np.arange(span)[None, :]
    rel = np.abs(col - WINDOW - np.arange(BLOCK)[:, None])
    slopes = 2.0 ** (-8.0 * np.arange(1, ATT_HEADS + 1) / ATT_HEADS)
    base = np.where(rel[None] <= WINDOW, -slopes[:, None, None] * rel[None], NEG_INF)
    tables = []
    for variant in range(4):
        dead = ((col < WINDOW) & bool(variant & 1)) | ((col >= WINDOW + BLOCK) & bool(variant & 2))
        tables.append(np.where(dead[None], NEG_INF, base))
    bias = jnp.asarray(np.stack(tables).astype(np.float32))

    def bias_spec(j):
        variant = lambda i: (jnp.where(i * Q_BLOCKS + j == 0, 1, 0) + jnp.where(i * Q_BLOCKS + j == nb - 1, 2, 0))
        return pl.BlockSpec((1, ATT_HEADS, BLOCK, span), lambda bi, i: (variant(i), 0, 0, 0))

    return pl.pallas_call(
        _attn_kernel,
        grid=(b, n_steps),
        in_specs=[pl.BlockSpec(memory_space=pltpu.SMEM)] + [bias_spec(j) for j in range(Q_BLOCKS)]
        + [pl.BlockSpec((1, rows, ATT_DIM), lambda bi, i: (bi, i, 0)),
           halo(kcol, prev), cur(kcol), halo(kcol, nxt),
           halo(kcol + 1, prev), cur(kcol + 1), halo(kcol + 1, nxt)],
        out_specs=pl.BlockSpec((1, rows, ATT_DIM), lambda bi, i: (bi, i, 0)),
        out_shape=jax.ShapeDtypeStruct((b, t, ATT_DIM), BF16),
        compiler_params=_params("parallel", "parallel"),
    )(sink.astype(F32), *([bias] * Q_BLOCKS), qkv, qkv, qkv, qkv, qkv, qkv, qkv)


MERGE_SPLIT = 2


def _merge_kernel(yr_ref, ya_ref, gate_ref, x_ref, wpr_ref, wpa_ref, wo_ref, g_ref, wrt_ref,
                  x1_ref, h2_ref, aff_ref, afft_ref):
    tm, d = x_ref.shape
    part = tm // MERGE_SPLIT
    for j in range(MERGE_SPLIT):
        rs = slice(j * part, (j + 1) * part)
        gl_r = gate_ref[rs, 0:d].astype(F32)
        gl_a = gate_ref[rs, d:2 * d].astype(F32)
        merged = (_sigmoid(gl_r) * _dot(yr_ref[rs, :], wpr_ref[...])
                  + _sigmoid(gl_a) * _dot(ya_ref[rs, :], wpa_ref[...]))
        x1 = x_ref[rs, :] + _dot(merged.astype(BF16), wo_ref[...])
        x1_ref[rs, :] = x1
        h2 = (x1 * lax.rsqrt(jnp.mean(x1 * x1, axis=-1, keepdims=True) + NORM_EPS) * g_ref[...]).astype(BF16)
        h2_ref[rs, :] = h2
        logits_t = _dot_nt(wrt_ref[...], h2)
        mx = jnp.max(logits_t, axis=0, keepdims=True)
        ex = jnp.exp(logits_t - mx)
        aff_t = ex / jnp.sum(ex, axis=0, keepdims=True)
        afft_ref[0, :, rs] = aff_t
        aff_ref[rs, :] = aff_t.T


def _merge(y_r, y_a, gates, x2, wpr, wpa, wo, g, w_router, tm, seq_len):
    n, d = x2.shape
    e = w_router.shape[1]
    tiles_per_seq = seq_len // tm
    full = lambda shape: pl.BlockSpec(shape, lambda i: (0,) * len(shape))
    rows = lambda w: pl.BlockSpec((tm, w), lambda i: (i, 0))
    return pl.pallas_call(
        _merge_kernel,
        grid=(n // tm,),
        in_specs=[rows(RWKV_DIM), rows(ATT_DIM), rows(2 * d), rows(d),
                  full(wpr.shape), full(wpa.shape), full(wo.shape), full((1, d)), full((e, d))],
        out_specs=[rows(d), rows(d), rows(e),
                   pl.BlockSpec((1, e, tm), lambda i: (i // tiles_per_seq, 0, i % tiles_per_seq))],
        out_shape=[jax.ShapeDtypeStruct((n, d), F32), jax.ShapeDtypeStruct((n, d), BF16),
                   jax.ShapeDtypeStruct((n, e), F32),
                   jax.ShapeDtypeStruct((n // seq_len, e, seq_len), F32)],
        compiler_params=_params("parallel"),
    )(y_r, y_a, gates, x2, wpr.astype(BF16), wpa.astype(BF16), wo.astype(BF16), g.reshape(1, d).astype(F32),
      w_router.T.astype(BF16))


def _select_kernel(a_ref, gsame_ref, gprev_ref, triu_ref, slot_ref, start_ref, *, cap):
    nb, rows, _ = a_ref.shape
    affs = [a_ref[i] for i in range(nb)]
    gsame = gsame_ref[...]
    gprev = gprev_ref[...]
    triu = triu_ref[...]
    capf = float(cap)

    def group_count(mask):
        per_lane = _dot(gsame, jnp.where(mask, 1.0, 0.0).astype(BF16))
        return jnp.sum(per_lane, axis=-1, keepdims=True)

    def search(i, prefixes):
        out = []
        for aff, prefix in zip(affs, prefixes):
            cand = prefix | lax.shift_left(jnp.int32(1), 30 - i)
            out.append(jnp.where(group_count(aff >= pltpu.bitcast(cand, F32)) >= capf, cand, prefix))
        return tuple(out)

    thr_bits = lax.fori_loop(0, 31, search, tuple(jnp.zeros((rows, 1), jnp.int32) for _ in range(nb)))

    def prefix_count(mask):
        mb = jnp.where(mask, 1.0, 0.0)
        inc = _dot(mb.astype(BF16), triu)
        tot = jnp.broadcast_to(inc[:, LANES - 1:LANES], inc.shape).astype(BF16)
        before = _dot(gprev, tot)
        return inc - mb + before, before

    for i, aff in enumerate(affs):
        thr = pltpu.bitcast(thr_bits[i], F32)
        gt = aff > thr
        eq = aff == thr
        need = capf - group_count(gt)
        sel = gt | (eq & (prefix_count(eq)[0] < need))
        pos, before = prefix_count(sel)
        slot_ref[i] = jnp.where(sel, pos, -1.0).astype(jnp.int32)
        start_ref[i] = before.astype(jnp.int32)


def _select(aff_t, cap):
    b, e, t = aff_t.shape
    tiles = t // LANES
    rows = e * tiles
    nb = _largest_divisor(b, (4, 2))
    rid = np.arange(rows)
    same = (rid[:, None] // tiles) == (rid[None, :] // tiles)
    gsame = jnp.asarray(same, BF16)
    gprev = jnp.asarray(same & (rid[None, :] < rid[:, None]), BF16)
    li = np.arange(LANES)
    triu = jnp.asarray(li[:, None] <= li[None, :], BF16)
    full = lambda shape: pl.BlockSpec(shape, lambda bi: (0,) * len(shape))
    slot, start = pl.pallas_call(
        functools.partial(_select_kernel, cap=cap),
        grid=(b // nb,),
        in_specs=[pl.BlockSpec((nb, rows, LANES), lambda bi: (bi, 0, 0)),
                  full((rows, rows)), full((rows, rows)), full((LANES, LANES))],
        out_specs=[pl.BlockSpec((nb, rows, LANES), lambda bi: (bi, 0, 0))] * 2,
        out_shape=[jax.ShapeDtypeStruct((b, rows, LANES), jnp.int32)] * 2,
        compiler_params=_params("parallel"),
    )(aff_t.reshape(b, rows, LANES), gsame, gprev, triu)
    return slot.reshape(b, e, t), start[:, :, 0].reshape(b, e, tiles)


ROUTE_TILE = 256
ROUTE_WIN = 64
WIN_ALIGN = 16


def _route_tables(start128, cap):
    per = ROUTE_TILE // LANES
    start = start128[:, :, ::per]
    nxt = jnp.concatenate([start[:, :, 1:], jnp.full_like(start[:, :, :1], cap)], axis=-1)
    w0 = jnp.minimum((start // WIN_ALIGN) * WIN_ALIGN, cap - ROUTE_WIN)
    fits = jnp.all(nxt - w0 <= ROUTE_WIN, axis=1)
    return w0.reshape(-1).astype(jnp.int32), fits.reshape(-1).astype(jnp.int32)


def _gather_kernel(w0_ref, fits_ref, slot_ref, h_ref, xs_ref, *, col_chunk):
    n_exp, cap, d = xs_ref.shape[1], xs_ref.shape[2], xs_ref.shape[3]
    nt = h_ref.shape[1] // ROUTE_TILE
    bi = pl.program_id(0)
    xs_ref[...] = jnp.zeros_like(xs_ref)
    ridx = lax.broadcasted_iota(jnp.int32, (ROUTE_WIN, 1), 0)
    cidx = lax.broadcasted_iota(jnp.int32, (cap, 1), 0)
    for j in range(nt):
        tok = slice(j * ROUTE_TILE, (j + 1) * ROUTE_TILE)
        fit = fits_ref[bi * nt + j]

        @pl.when(fit != 0)
        def _():
            w0 = [pl.multiple_of(w0_ref[(bi * n_exp + e) * nt + j], WIN_ALIGN) for e in range(n_exp)]
            onehot = jnp.concatenate(
                [jnp.where(slot_ref[0, e:e + 1, tok] - w0[e] == ridx, 1.0, 0.0).astype(BF16) for e in range(n_exp)],
                axis=0)
            for c0 in range(0, d, col_chunk):
                rows = _dot(onehot, h_ref[0, tok, c0:c0 + col_chunk]).astype(BF16)
                for e in range(n_exp):
                    win = (0, e, pl.ds(w0[e], ROUTE_WIN), slice(c0, c0 + col_chunk))
                    xs_ref[win] = xs_ref[win] + rows[e * ROUTE_WIN:(e + 1) * ROUTE_WIN]

        @pl.when(fit == 0)
        def _():
            for e in range(n_exp):
                onehot = jnp.where(slot_ref[0, e:e + 1, tok] == cidx, 1.0, 0.0).astype(BF16)
                xs_ref[0, e] = xs_ref[0, e] + _dot(onehot, h_ref[0, tok, :]).astype(BF16)


def _gather(w0, fits, slot, h2, cap):
    b, t, d = h2.shape
    e = slot.shape[1]
    return pl.pallas_call(
        functools.partial(_gather_kernel, col_chunk=512),
        grid_spec=pltpu.PrefetchScalarGridSpec(
            num_scalar_prefetch=2,
            grid=(b,),
            in_specs=[pl.BlockSpec((1, e, t), lambda bi, *_: (bi, 0, 0)),
                      pl.BlockSpec((1, t, d), lambda bi, *_: (bi, 0, 0))],
            out_specs=pl.BlockSpec((1, e, cap, d), lambda bi, *_: (bi, 0, 0, 0))),
        out_shape=jax.ShapeDtypeStruct((b, e, cap, d), BF16),
        compiler_params=_params("parallel"),
    )(w0, fits, slot, h2)


def _expert_kernel(xs_ref, wg_ref, wu_ref, wd_ref, ys_ref):
    bb, _, cap, d = xs_ref.shape
    xs = xs_ref[...].reshape(bb * cap, d)
    gate = _dot(xs, wg_ref[0])
    up = _dot(xs, wu_ref[0])
    hid = (gate * _sigmoid(gate) * up).astype(BF16)
    ys_ref[...] = _dot(hid, wd_ref[0]).astype(ys_ref.dtype).reshape(bb, 1, cap, d)


def _experts(xs, wg, wu, wd, bb):
    b, e, cap, d = xs.shape
    ff = wg.shape[-1]
    tok = pl.BlockSpec((bb, 1, cap, d), lambda ei, bi: (bi, ei, 0, 0))
    return pl.pallas_call(
        _expert_kernel,
        grid=(e, b // bb),
        in_specs=[tok,
                  pl.BlockSpec((1, d, ff), lambda ei, bi: (ei, 0, 0)),
                  pl.BlockSpec((1, d, ff), lambda ei, bi: (ei, 0, 0)),
                  pl.BlockSpec((1, ff, d), lambda ei, bi: (ei, 0, 0))],
        out_specs=tok,
        out_shape=jax.ShapeDtypeStruct((b, e, cap, d), BF16),
        compiler_params=_params("parallel", "arbitrary"),
    )(xs, wg, wu, wd)


def _scatter_kernel(w0_ref, fits_ref, slot_ref, aff_ref, expand_ref, ys_ref, x1_ref, g_ref, o_ref, *, final_norm):
    n_exp, cap = ys_ref.shape[1], ys_ref.shape[2]
    bi, j = pl.program_id(0), pl.program_id(1)
    nt = pl.num_programs(1)
    slot = slot_ref[0]
    aff = aff_ref[0]
    fit = fits_ref[bi * nt + j]

    def finish(acc):
        if final_norm:
            acc = acc * lax.rsqrt(jnp.mean(acc * acc, axis=-1, keepdims=True) + NORM_EPS) * g_ref[...]
        o_ref[0] = acc

    @pl.when(fit != 0)
    def _():
        w0 = [pl.multiple_of(w0_ref[(bi * n_exp + e) * nt + j], WIN_ALIGN) for e in range(n_exp)]
        lane = lax.broadcasted_iota(jnp.int32, (1, n_exp * ROUTE_WIN), 1)
        w0_row = jnp.zeros((1, n_exp * ROUTE_WIN), jnp.int32)
        for e in range(n_exp):
            w0_row = jnp.where(lane // ROUTE_WIN == e, w0[e], w0_row)
        target = (w0_row + lane % ROUTE_WIN).astype(F32)
        slot_l = _dot(slot.astype(F32).astype(BF16), expand_ref[...])
        aff_l = _dot(aff.astype(BF16), expand_ref[...])
        onehot = jnp.where(slot_l == target, aff_l, 0.0).astype(BF16)
        rows = jnp.concatenate([ys_ref[0, e, pl.ds(w0[e], ROUTE_WIN), :] for e in range(n_exp)], axis=0)
        finish(x1_ref[0] + _dot(onehot, rows))

    @pl.when(fit == 0)
    def _():
        cidx = lax.broadcasted_iota(jnp.int32, (1, cap), 1)
        acc = x1_ref[0]
        for e in range(n_exp):
            onehot = jnp.where(slot[:, e:e + 1] == cidx, aff[:, e:e + 1], 0.0).astype(BF16)
            acc = acc + _dot(onehot, ys_ref[0, e])
        finish(acc)


def _scatter(w0, fits, slot_cols, aff, ys, x1, g, final_norm):
    b, t, d = x1.shape
    e, cap = ys.shape[1], ys.shape[2]
    assert cap <= 256, "slot ids must stay exact in bf16"
    lane_exp = np.arange(e * ROUTE_WIN) // ROUTE_WIN
    expand = jnp.asarray(np.arange(e)[:, None] == lane_exp[None, :], BF16)
    return pl.pallas_call(
        functools.partial(_scatter_kernel, final_norm=final_norm),
        grid_spec=pltpu.PrefetchScalarGridSpec(
            num_scalar_prefetch=2,
            grid=(b, t // ROUTE_TILE),
            in_specs=[pl.BlockSpec((1, ROUTE_TILE, e), lambda bi, ti, *_: (bi, ti, 0)),
                      pl.BlockSpec((1, ROUTE_TILE, e), lambda bi, ti, *_: (bi, ti, 0)),
                      pl.BlockSpec((e, e * ROUTE_WIN), lambda bi, ti, *_: (0, 0)),
                      pl.BlockSpec((1, e, cap, d), lambda bi, ti, *_: (bi, 0, 0, 0)),
                      pl.BlockSpec((1, ROUTE_TILE, d), lambda bi, ti, *_: (bi, ti, 0)),
                      pl.BlockSpec((1, d), lambda bi, ti, *_: (0, 0))],
            out_specs=pl.BlockSpec((1, ROUTE_TILE, d), lambda bi, ti, *_: (bi, ti, 0))),
        out_shape=jax.ShapeDtypeStruct((b, t, d), F32),
        compiler_params=_params("parallel", "arbitrary"),
    )(w0, fits, slot_cols, aff, expand, ys, x1, g.reshape(1, d).astype(F32))


def kernel(x, norm_mix_g, w_in, mu_prev, mu_next, w0_f, w_up_f, w0_b, w_up_b, a0_f, a_up_f, a0_b, a_up_b, g_up, k_k, k_a, r_k, ln_x_w, ln_x_b, attn_sink, w_proj_rwkv, w_proj_attn, w_out, norm_ffn_g, w_router, exp_w_gate, exp_w_up, exp_w_down, norm_final_g):
    b, t, d = x.shape
    depth = w_in.shape[0]
    n = b * t
    tm = _largest_divisor(t, (512, 256, 128))
    cap = CAPACITY_FACTOR * t // N_EXPERTS
    bb = _largest_divisor(b, (4, 2))
    xc = x
    for l in range(depth):
        x2 = xc.reshape(n, d)
        u_r, qkv, gates = _in_projection(x2, norm_mix_g[l].astype(F32), w_in[l].astype(BF16), mu_prev[l], mu_next[l],
                                         tm, t)
        y_r = _rwkv_branch(u_r.reshape(b, t, RWKV_COLS), w0_f[l], w_up_f[l], w0_b[l],
                           w_up_b[l], a0_f[l], a_up_f[l], a0_b[l], a_up_b[l], g_up[l], k_k[l], k_a[l], r_k[l],
                           ln_x_w[l], ln_x_b[l])
        y_a = _attention(qkv.reshape(b, t, QKV_COLS), attn_sink[l])
        last = l == depth - 1
        x1, h2, aff, aff_t = _merge(y_r.reshape(n, RWKV_DIM), y_a.reshape(n, ATT_DIM), gates, x2, w_proj_rwkv[l],
                                    w_proj_attn[l], w_out[l], norm_ffn_g[l], w_router[l],
                                    _largest_divisor(t, (1024, 512, 256)), t)
        slot, start128 = _select(aff_t, cap)
        w0, fits = _route_tables(start128, cap)
        xs = _gather(w0, fits, slot, h2.reshape(b, t, d), cap)
        ys = _experts(xs, exp_w_gate[l].astype(BF16), exp_w_up[l].astype(BF16), exp_w_down[l].astype(BF16), bb)
        xc = _scatter(w0, fits, jnp.swapaxes(slot, 1, 2), aff.reshape(b, t, N_EXPERTS), ys, x1.reshape(b, t, d),
                      norm_final_g, final_norm=last)
    return xc
```

```python
import functools
import math

import numpy as np
import jax
import jax.numpy as jnp
from jax import lax
from jax.experimental import pallas as pl
from jax.experimental.pallas import tpu as pltpu

F32 = jnp.float32
BF16 = jnp.bfloat16

RWKV_HEAD = 64
RWKV_HEADS = 8
RWKV_DIM = RWKV_HEADS * RWKV_HEAD
W_LORA = 64
A_LORA = 64
G_LORA = 128
DECAY_SCALE = math.exp(-0.5)
LNX_EPS = 64e-5
ATT_HEADS = 8
ATT_KV_HEADS = 2
ATT_HEAD = 64
ATT_DIM = ATT_HEADS * ATT_HEAD
KV_DIM = ATT_KV_HEADS * ATT_HEAD
WINDOW = 128
BLOCK = 128
NEG_INF = -1e30
N_EXPERTS = 16
CAPACITY_FACTOR = 2
NORM_EPS = 1e-6
RWKV_COLS = 3 * RWKV_DIM + G_LORA + 2 * W_LORA + 2 * A_LORA
QKV_COLS = ATT_DIM + 2 * KV_DIM
OFF_R, OFF_K, OFF_V = 0, RWKV_DIM, 2 * RWKV_DIM
OFF_G = 3 * RWKV_DIM
OFF_WF = OFF_G + G_LORA
OFF_WB = OFF_WF + W_LORA
OFF_AF = OFF_WB + W_LORA
OFF_AB = OFF_AF + A_LORA

LANES = 128
VMEM_LIMIT = 48 * 1024 * 1024

CHUNK = 64


def _dot(a, b):
    return jnp.dot(a, b, preferred_element_type=F32)


def _dot_nt(a, b):
    return lax.dot_general(a, b, (((1,), (1,)), ((), ())), preferred_element_type=F32)


def _split2(x):
    hi = x.astype(BF16)
    lo = (x - hi.astype(F32)).astype(BF16)
    return hi, lo


def _seg_sum(x, bd):
    width = bd.shape[0]
    return jnp.concatenate([_dot(x[:, j:j + width].astype(BF16), bd) for j in range(0, x.shape[1], width)], axis=1)


def _sigmoid(x):
    return 0.5 * jnp.tanh(0.5 * x) + 0.5


def _largest_divisor(n, candidates):
    for c in candidates:
        if n % c == 0:
            return c
    return 1


def _params(*sem):
    return pltpu.CompilerParams(dimension_semantics=sem, vmem_limit_bytes=VMEM_LIMIT)


def _inproj_kernel(x_ref, xp_ref, xn_ref, g_ref, w_ref, mup_ref, mun_ref, ur_ref, qkv_ref, gate_ref, *,
                   col_chunk, tiles_per_seq):
    i = pl.program_id(0)
    tm = x_ref.shape[0]
    halo = xp_ref.shape[0]
    x = jnp.concatenate([xp_ref[...], x_ref[...], xn_ref[...]], axis=0)
    h = x * lax.rsqrt(jnp.mean(x * x, axis=-1, keepdims=True) + NORM_EPS) * g_ref[...]
    row = lax.broadcasted_iota(jnp.int32, (tm + 2 * halo, 1), 0)
    pos = i % tiles_per_seq
    lo = jnp.where(pos == 0, halo, 0)
    hi = jnp.where(pos == tiles_per_seq - 1, tm + halo, tm + 2 * halo)
    h = jnp.where((row < lo) | (row >= hi), 0.0, h)
    hb_all = h.astype(BF16)
    hb = h[halo:halo + tm].astype(BF16)
    for j in range(0, RWKV_COLS, col_chunk):
        w = min(col_chunk, RWKV_COLS - j)
        u = _dot(hb_all, w_ref[:, j:j + w])
        prev = pltpu.roll(u, 1, axis=0)
        nxt = pltpu.roll(u, tm + 2 * halo - 1, axis=0)
        us = u + mup_ref[:, j:j + w] * (prev - u) + mun_ref[:, j:j + w] * (nxt - u)
        ur_ref[:, j:j + w] = us[halo:halo + tm].astype(BF16)
    c0 = RWKV_COLS
    for ref in (qkv_ref, gate_ref):
        width = ref.shape[-1]
        for j in range(0, width, col_chunk):
            w = min(col_chunk, width - j)
            ref[:, j:j + w] = _dot(hb, w_ref[:, c0 + j:c0 + j + w]).astype(BF16)
        c0 += width


def _in_projection(x2, g, w_in_bf, mu_prev, mu_next, tm, seq_len):
    n, d = x2.shape
    cols = w_in_bf.shape[1]
    gate_cols = cols - RWKV_COLS - QKV_COLS
    halo = 8
    hb = tm // halo
    n_halo = n // halo
    return pl.pallas_call(
        functools.partial(_inproj_kernel, col_chunk=512, tiles_per_seq=seq_len // tm),
        grid=(n // tm,),
        in_specs=[pl.BlockSpec((tm, d), lambda i: (i, 0)),
                  pl.BlockSpec((halo, d), lambda i: (jnp.maximum(i * hb - 1, 0), 0)),
                  pl.BlockSpec((halo, d), lambda i: (jnp.minimum((i + 1) * hb, n_halo - 1), 0)),
                  pl.BlockSpec((1, d), lambda i: (0, 0)),
                  pl.BlockSpec((d, cols), lambda i: (0, 0)),
                  pl.BlockSpec((1, RWKV_COLS), lambda i: (0, 0)),
                  pl.BlockSpec((1, RWKV_COLS), lambda i: (0, 0))],
        out_specs=[pl.BlockSpec((tm, RWKV_COLS), lambda i: (i, 0)),
                   pl.BlockSpec((tm, QKV_COLS), lambda i: (i, 0)),
                   pl.BlockSpec((tm, gate_cols), lambda i: (i, 0))],
        out_shape=[jax.ShapeDtypeStruct((n, RWKV_COLS), BF16),
                   jax.ShapeDtypeStruct((n, QKV_COLS), BF16),
                   jax.ShapeDtypeStruct((n, gate_cols), BF16)],
        compiler_params=_params("parallel"),
    )(x2, x2, x2, g.reshape(1, d), w_in_bf, mu_prev.reshape(1, -1).astype(F32), mu_next.reshape(1, -1).astype(F32))


GROUP_HEADS = 4
GROUP = GROUP_HEADS * RWKV_HEAD
N_GROUPS = RWKV_HEADS // GROUP_HEADS
ROWS_PER_STEP = 256
SEQS_PER_STEP = 4


def _rwkv_kernel(*refs, reverse, final):
    if final:
        (u_ref, w0_ref, wup_ref, a0_ref, aup_ref, kk_ref, ka_ref,
         bd_ref, tri_ref, a0o_ref, aupo_ref, gup_ref, rk_ref, lnw_ref, lnb_ref, yb_ref,
         o_ref, st_ref) = refs
    else:
        (u_ref, w0_ref, wup_ref, a0_ref, aup_ref, kk_ref, ka_ref,
         bd_ref, tri_ref, o_ref, st_ref) = refs
    NB, TB = u_ref.shape[0], u_ref.shape[1]
    L = CHUNK
    N = RWKV_HEAD
    n_ck = TB // L
    step = pl.program_id(1)

    @pl.when(step == 0)
    def _():
        st_ref[...] = jnp.zeros_like(st_ref)

    bd = bd_ref[...]
    tri = tri_ref[...]

    def prep(bi):
        col = lambda off, width: u_ref[bi, :, off:off + width].astype(F32)
        r = col(OFF_R, RWKV_DIM)
        k = col(OFF_K, RWKV_DIM)
        v = col(OFF_V, RWKV_DIM)
        w_lo = col(OFF_WB if reverse else OFF_WF, W_LORA)
        a_lo = col(OFF_AB if reverse else OFF_AF, A_LORA)
        lw = -DECAY_SCALE * _sigmoid(w0_ref[...] + _dot(jnp.tanh(w_lo).astype(BF16), wup_ref[...]))
        a = _sigmoid(a0_ref[...] + _dot(a_lo.astype(BF16), aup_ref[...]))
        kkr = k * kk_ref[...]
        kk = kkr * lax.rsqrt(jnp.maximum(_seg_sum(kkr * kkr, bd), 1e-24))
        kd = k * (1.0 + (a - 1.0) * ka_ref[...])
        b = a * kk
        l_hi, l_lo = _split2(lw)
        cs = _dot(tri, l_hi) + _dot(tri, l_lo)
        edge = 0 if reverse else L - 1
        e_tot = [jnp.exp(cs[c * L + edge:c * L + edge + 1]) for c in range(n_ck)]
        e_neg = jnp.exp(-cs)
        e_rem = jnp.concatenate([e_neg[c * L:(c + 1) * L] * e_tot[c] for c in range(n_ck)], axis=0)
        return dict(r=r, k=k, v=v, kd=kd, col=col, e_tot=e_tot,
                    a_t=(kk * jnp.exp(cs - lw)).astype(BF16), r_t=(r * jnp.exp(cs)).astype(BF16),
                    k_t=(kd * e_neg).astype(BF16), b_t=(b * e_neg).astype(BF16),
                    k_h=(kd * e_rem).astype(BF16), b_hf=-b * e_rem)

    pre = [prep(bi) for bi in range(NB)]

    ii = lax.broadcasted_iota(jnp.int32, (GROUP, GROUP), 0)
    jj = lax.broadcasted_iota(jnp.int32, (GROUP, GROUP), 1)
    same_head = (ii // N) == (jj // N)
    ti = lax.broadcasted_iota(jnp.int32, (L, GROUP), 0)
    si = lax.broadcasted_iota(jnp.int32, (L, GROUP), 1) % N
    if reverse:
        strict = si > ti
        incl = si >= ti
    else:
        strict = si < ti
        incl = si <= ti
    eye = (si == ti).astype(F32)
    level_masks = []
    s = 1
    while s < L:
        same = (ti // (2 * s)) == (si // (2 * s))
        t_hi = (ti & s) != 0
        s_hi = (si & s) != 0
        if reverse:
            level_masks.append(same & jnp.logical_not(t_hi) & s_hi)
        else:
            level_masks.append(same & t_hi & jnp.logical_not(s_hi))
        s *= 2

    def bdiag(xp):
        return jnp.where(same_head, jnp.concatenate([xp.astype(BF16)] * GROUP_HEADS, axis=0), 0.0)

    def head_transpose(xp):
        return jnp.concatenate([xp[:, h * N:(h + 1) * N].T for h in range(GROUP_HEADS)], axis=1)

    order = list(range(n_ck))[::-1] if reverse else list(range(n_ck))
    cgs = [(bi, c, g) for bi in range(NB) for c in order for g in range(N_GROUPS)]

    def tile(name, cg):
        bi, c, g = cg
        return pre[bi][name][c * L:(c + 1) * L, g * GROUP:(g + 1) * GROUP]

    ar = {cg: jnp.concatenate([tile('a_t', cg), tile('r_t', cg)], axis=0) for cg in cgs}
    arb = {cg: _dot_nt(ar[cg], bdiag(tile('b_t', cg))) for cg in cgs}
    ark = {cg: _dot_nt(ar[cg], bdiag(tile('k_t', cg))) for cg in cgs}
    ab = {cg: arb[cg][0:L] for cg in cgs}
    rb = {cg: arb[cg][L:2 * L] for cg in cgs}
    ak = {cg: ark[cg][0:L] for cg in cgs}
    rk_ = {cg: ark[cg][L:2 * L] for cg in cgs}
    x = {cg: eye - jnp.where(level_masks[0], ab[cg], 0.0) for cg in cgs}
    for m in level_masks[1:]:
        xn = {cg: _dot(x[cg].astype(BF16), bdiag(jnp.where(m, ab[cg], 0.0))).astype(BF16) for cg in cgs}
        x = {cg: x[cg] - _dot(xn[cg], bdiag(x[cg])) for cg in cgs}
    xb = {cg: x[cg].astype(BF16) for cg in cgs}
    xa = {cg: _dot(xb[cg], bdiag(tile('a_t', cg))) for cg in cgs}
    xa_w = {cg: bdiag(xa[cg]) for cg in cgs}
    mrb_n = {cg: jnp.where(incl, -rb[cg], 0.0).astype(BF16) for cg in cgs}
    v_w = {cg: bdiag(tile('v', cg)) for cg in cgs}
    mv = {cg: _dot(jnp.concatenate([jnp.where(strict, ak[cg], 0.0).astype(BF16),
                                    jnp.where(incl, rk_[cg], 0.0).astype(BF16)], axis=0), v_w[cg]) for cg in cgs}
    xmv = {cg: _dot(xb[cg], bdiag(mv[cg][0:L])) for cg in cgs}
    bt = {cg: head_transpose(tile('b_hf', cg)).astype(BF16) for cg in cgs}
    lhs2 = {cg: jnp.concatenate([mrb_n[cg], bt[cg]], axis=0) for cg in cgs}
    pa = {cg: _dot(lhs2[cg], xa_w[cg]) for cg in cgs}
    rq = {cg: (tile('r_t', cg).astype(F32) + pa[cg][0:L]).astype(BF16) for cg in cgs}
    pm = {cg: _dot(lhs2[cg], bdiag(xmv[cg])) for cg in cgs}
    dd = {cg: mv[cg][L:2 * L] + pm[cg][0:L] for cg in cgs}
    eye_b = eye.astype(BF16)
    c2 = {cg: _dot_nt(eye_b, bdiag(pm[cg][L:2 * L])) for cg in cgs}
    cc = {cg: _dot(head_transpose(tile('v', cg)).astype(BF16), bdiag(tile('k_h', cg))) + c2[cg] for cg in cgs}
    wt_w = {cg: bdiag(pa[cg][L:2 * L]) for cg in cgs}

    state = {(bi, g): st_ref[bi, g] for bi in range(NB) for g in range(N_GROUPS)}
    y_tiles = {}
    for c in order:
        for bi in range(NB):
            for g in range(N_GROUPS):
                cg = (bi, c, g)
                sb = state[bi, g].astype(BF16)
                decay = pre[bi]['e_tot'][c][:, g * GROUP:(g + 1) * GROUP]
                state[bi, g] = state[bi, g] * decay + _dot_nt(sb, wt_w[cg]) + cc[cg]
                y_tiles[cg] = _dot_nt(rq[cg], bdiag(sb)) + dd[cg]
    for key, val in state.items():
        st_ref[key] = val

    for bi in range(NB):
        wkv = jnp.concatenate(
            [jnp.concatenate([y_tiles[(bi, c, g)] for g in range(N_GROUPS)], axis=1) for c in range(n_ck)], axis=0)
        if not final:
            o_ref[bi] = wkv
            continue
        p = pre[bi]
        r, k, v, kd, col = p['r'], p['k'], p['v'], p['kd'], p['col']
        wkv = wkv + yb_ref[bi]
        inv_n = 1.0 / N
        mean = _seg_sum(wkv, bd) * inv_n
        xc = wkv - mean
        var = _seg_sum(xc * xc, bd) * inv_n
        normed = xc * lax.rsqrt(var + LNX_EPS) * lnw_ref[...] + lnb_ref[...]
        a_o = _sigmoid(a0o_ref[...] + _dot(col(OFF_AB, A_LORA).astype(BF16), aupo_ref[...]))
        kd_o = k * (1.0 + (a_o - 1.0) * ka_ref[...])
        bonus = _seg_sum(r * rk_ref[...] * (kd + kd_o), bd) * v
        gg = _dot(_sigmoid(col(OFF_G, G_LORA)).astype(BF16), gup_ref[...])
        o_ref[bi] = ((normed + bonus) * gg).astype(o_ref.dtype)


def _rwkv_pass(u_r, consts, *, reverse, final, extra=()):
    b, t, _ = u_r.shape
    tb = _largest_divisor(t, (ROWS_PER_STEP, CHUNK))
    nb = _largest_divisor(b, (SEQS_PER_STEP,))
    n_steps = t // tb

    def sidx(s):
        return (n_steps - 1 - s) if reverse else s

    def full(shape):
        return pl.BlockSpec(shape, lambda bi, s: (0,) * len(shape))

    in_specs = [pl.BlockSpec((nb, tb, RWKV_COLS), lambda bi, s: (bi, sidx(s), 0))]
    args = [u_r]
    for arr in consts:
        in_specs.append(full(arr.shape))
        args.append(arr)
    if final:
        for arr in extra[:-1]:
            in_specs.append(full(arr.shape))
            args.append(arr)
        in_specs.append(pl.BlockSpec((nb, tb, RWKV_DIM), lambda bi, s: (bi, sidx(s), 0)))
        args.append(extra[-1])
    out_dtype = BF16 if final else F32
    return pl.pallas_call(
        functools.partial(_rwkv_kernel, reverse=reverse, final=final),
        grid=(b // nb, n_steps),
        in_specs=in_specs,
        out_specs=pl.BlockSpec((nb, tb, RWKV_DIM), lambda bi, s: (bi, sidx(s), 0)),
        out_shape=jax.ShapeDtypeStruct((b, t, RWKV_DIM), out_dtype),
        scratch_shapes=[pltpu.VMEM((nb, N_GROUPS, RWKV_HEAD, GROUP), F32)],
        compiler_params=_params("parallel", "arbitrary"),
    )(*args)


def _rwkv_branch(u_r, w0_f, w_up_f, w0_b, w_up_b, a0_f, a_up_f, a0_b, a_up_b,
                 g_up, k_k, k_a, r_k, ln_w, ln_b):
    t = u_r.shape[1]
    tb = _largest_divisor(t, (ROWS_PER_STEP, CHUNK))
    row = lambda a: a.reshape(1, -1).astype(F32)
    seg = np.arange(GROUP) // RWKV_HEAD
    bd = jnp.asarray(seg[:, None] == seg[None, :], BF16)
    ti = np.arange(tb)
    same_chunk = (ti[:, None] // CHUNK) == (ti[None, :] // CHUNK)
    tri_f = jnp.asarray(same_chunk & (ti[None, :] <= ti[:, None]), BF16)
    tri_b = jnp.asarray(same_chunk & (ti[None, :] >= ti[:, None]), BF16)
    common = lambda w0, wup, a0, aup, tri: [row(w0), wup.astype(BF16), row(a0), aup.astype(BF16), row(k_k), row(k_a),
                                            bd, tri]
    y_b = _rwkv_pass(u_r, common(w0_b, w_up_b, a0_b, a_up_b, tri_b), reverse=True, final=False)
    extra = [row(a0_b), a_up_b.astype(BF16), g_up.astype(BF16), row(r_k), row(ln_w), row(ln_b), y_b]
    return _rwkv_pass(u_r, common(w0_f, w_up_f, a0_f, a_up_f, tri_f), reverse=False, final=True, extra=extra)


Q_BLOCKS = 4


def _attn_kernel(sink_ref, *refs):
    bias_refs = refs[:Q_BLOCKS]
    q_ref, kp_ref, kc_ref, kn_ref, vp_ref, vc_ref, vn_ref, o_ref = refs[Q_BLOCKS:]
    span = BLOCK + 2 * WINDOW
    kk = jnp.concatenate([kp_ref[0], kc_ref[0], kn_ref[0]], axis=0)
    vv = jnp.concatenate([vp_ref[0], vc_ref[0], vn_ref[0]], axis=0)
    group = ATT_HEADS // ATT_KV_HEADS
    kv_lane = lax.broadcasted_iota(jnp.int32, (1, KV_DIM), 1) // ATT_HEAD
    v_ext = [jnp.where(kv_lane == kv, vv, 1.0) for kv in range(ATT_KV_HEADS)]
    qs = q_ref[0] * (ATT_HEAD ** -0.5)
    hsl = lambda h: slice(h * ATT_HEAD, (h + 1) * ATT_HEAD)
    qrows = lambda j: slice(j * BLOCK, (j + 1) * BLOCK)
    krows = lambda j: slice(j * BLOCK, j * BLOCK + span)
    jh = [(j, h) for j in range(Q_BLOCKS) for h in range(ATT_HEADS)]
    s = {c: _dot_nt(qs[qrows(c[0]), hsl(c[1])], kk[krows(c[0]), hsl(c[1] // group)]) + bias_refs[c[0]][0, c[1]]
         for c in jh}
    m = {c: jnp.maximum(jnp.max(s[c], axis=-1, keepdims=True), sink_ref[c[1]]) for c in jh}
    p = {c: jnp.exp(s[c] - m[c]).astype(BF16) for c in jh}
    pv = {c: _dot(p[c], v_ext[c[1] // group][krows(c[0])]) for c in jh}
    for j in range(Q_BLOCKS):
        outs = []
        for h in range(ATT_HEADS):
            kv = h // group
            other = (kv + 1) % ATT_KV_HEADS
            denom = pv[j, h][:, other * ATT_HEAD:other * ATT_HEAD + 1] + jnp.exp(sink_ref[h] - m[j, h])
            outs.append(pv[j, h][:, hsl(kv)] / denom)
        o_ref[0, qrows(j)] = jnp.concatenate(outs, axis=-1).astype(o_ref.dtype)


def _attention(qkv, sink):
    b, t, _ = qkv.shape
    nb = t // BLOCK
    assert nb % Q_BLOCKS == 0
    n_steps = nb // Q_BLOCKS
    rows = Q_BLOCKS * BLOCK
    span = BLOCK + 2 * WINDOW
    kcol = ATT_DIM // KV_DIM
    halo = lambda col, f: pl.BlockSpec((1, BLOCK, KV_DIM), lambda bi, i: (bi, f(i), col))
    prev = lambda i: jnp.maximum(i * Q_BLOCKS - 1, 0)
    nxt = lambda i: jnp.minimum((i + 1) * Q_BLOCKS, nb - 1)
    cur = lambda col: pl.BlockSpec((1, rows, KV_DIM), lambda bi, i: (bi, i, col))
    col = np.arange(span)[None, :]
    rel = np.abs(col - WINDOW - np.arange(BLOCK)[:, None])
    slopes = 2.0 ** (-8.0 * np.arange(1, ATT_HEADS + 1) / ATT_HEADS)
    base = np.where(rel[None] <= WINDOW, -slopes[:, None, None] * rel[None], NEG_INF)
    tables = []
    for variant in range(4):
        dead = ((col < WINDOW) & bool(variant & 1)) | ((col >= WINDOW + BLOCK) & bool(variant & 2))
        tables.append(np.where(dead[None], NEG_INF, base))
    bias = jnp.asarray(np.stack(tables).astype(np.float32))

    def bias_spec(j):
        variant = lambda i: (jnp.where(i * Q_BLOCKS + j == 0, 1, 0) + jnp.where(i * Q_BLOCKS + j == nb - 1, 2, 0))
        return pl.BlockSpec((1, ATT_HEADS, BLOCK, span), lambda bi, i: (variant(i), 0, 0, 0))

    return pl.pallas_call(
        _attn_kernel,
        grid=(b, n_steps),
        in_specs=[pl.BlockSpec(memory_space=pltpu.SMEM)] + [bias_spec(j) for j in range(Q_BLOCKS)]
        + [pl.BlockSpec((1, rows, ATT_DIM), lambda bi, i: (bi, i, 0)),
           halo(kcol, prev), cur(kcol), halo(kcol, nxt),
           halo(kcol + 1, prev), cur(kcol + 1), halo(kcol + 1, nxt)],
        out_specs=pl.BlockSpec((1, rows, ATT_DIM), lambda bi, i: (bi, i, 0)),
        out_shape=jax.ShapeDtypeStruct((b, t, ATT_DIM), BF16),
        compiler_params=_params("parallel", "parallel"),
    )(sink.astype(F32), *([bias] * Q_BLOCKS), qkv, qkv, qkv, qkv, qkv, qkv, qkv)


MERGE_SPLIT = 2


def _merge_kernel(yr_ref, ya_ref, gate_ref, x_ref, wpr_ref, wpa_ref, wo_ref, g_ref, wrt_ref,
                  x1_ref, h2_ref, aff_ref, afft_ref):
    tm, d = x_ref.shape
    part = tm // MERGE_SPLIT
    for j in range(MERGE_SPLIT):
        rs = slice(j * part, (j + 1) * part)
        gl_r = gate_ref[rs, 0:d].astype(F32)
        gl_a = gate_ref[rs, d:2 * d].astype(F32)
        merged = (_sigmoid(gl_r) * _dot(yr_ref[rs, :], wpr_ref[...])
                  + _sigmoid(gl_a) * _dot(ya_ref[rs, :], wpa_ref[...]))
        x1 = x_ref[rs, :] + _dot(merged.astype(BF16), wo_ref[...])
        x1_ref[rs, :] = x1
        h2 = (x1 * lax.rsqrt(jnp.mean(x1 * x1, axis=-1, keepdims=True) + NORM_EPS) * g_ref[...]).astype(BF16)
        h2_ref[rs, :] = h2
        logits_t = _dot_nt(wrt_ref[...], h2)
        mx = jnp.max(logits_t, axis=0, keepdims=True)
        ex = jnp.exp(logits_t - mx)
        aff_t = ex / jnp.sum(ex, axis=0, keepdims=True)
        afft_ref[0, :, rs] = aff_t
        aff_ref[rs, :] = aff_t.T


def _merge(y_r, y_a, gates, x2, wpr, wpa, wo, g, w_router, tm, seq_len):
    n, d = x2.shape
    e = w_router.shape[1]
    tiles_per_seq = seq_len // tm
    full = lambda shape: pl.BlockSpec(shape, lambda i: (0,) * len(shape))
    rows = lambda w: pl.BlockSpec((tm, w), lambda i: (i, 0))
    return pl.pallas_call(
        _merge_kernel,
        grid=(n // tm,),
        in_specs=[rows(RWKV_DIM), rows(ATT_DIM), rows(2 * d), rows(d),
                  full(wpr.shape), full(wpa.shape), full(wo.shape), full((1, d)), full((e, d))],
        out_specs=[rows(d), rows(d), rows(e),
                   pl.BlockSpec((1, e, tm), lambda i: (i // tiles_per_seq, 0, i % tiles_per_seq))],
        out_shape=[jax.ShapeDtypeStruct((n, d), F32), jax.ShapeDtypeStruct((n, d), BF16),
                   jax.ShapeDtypeStruct((n, e), F32),
                   jax.ShapeDtypeStruct((n // seq_len, e, seq_len), F32)],
        compiler_params=_params("parallel"),
    )(y_r, y_a, gates, x2, wpr.astype(BF16), wpa.astype(BF16), wo.astype(BF16), g.reshape(1, d).astype(F32),
      w_router.T.astype(BF16))


def _select_kernel(a_ref, gsame_ref, gprev_ref, triu_ref, slot_ref, start_ref, *, cap):
    nb, rows, _ = a_ref.shape
    affs = [a_ref[i] for i in range(nb)]
    gsame = gsame_ref[...]
    gprev = gprev_ref[...]
    triu = triu_ref[...]
    capf = float(cap)

    def group_count(mask):
        per_lane = _dot(gsame, jnp.where(mask, 1.0, 0.0).astype(BF16))
        return jnp.sum(per_lane, axis=-1, keepdims=True)

    def search(i, prefixes):
        out = []
        for aff, prefix in zip(affs, prefixes):
            cand = prefix | lax.shift_left(jnp.int32(1), 30 - i)
            out.append(jnp.where(group_count(aff >= pltpu.bitcast(cand, F32)) >= capf, cand, prefix))
        return tuple(out)

    thr_bits = lax.fori_loop(0, 31, search, tuple(jnp.zeros((rows, 1), jnp.int32) for _ in range(nb)))

    def prefix_count(mask):
        mb = jnp.where(mask, 1.0, 0.0)
        inc = _dot(mb.astype(BF16), triu)
        tot = jnp.broadcast_to(inc[:, LANES - 1:LANES], inc.shape).astype(BF16)
        before = _dot(gprev, tot)
        return inc - mb + before, before

    for i, aff in enumerate(affs):
        thr = pltpu.bitcast(thr_bits[i], F32)
        gt = aff > thr
        eq = aff == thr
        need = capf - group_count(gt)
        sel = gt | (eq & (prefix_count(eq)[0] < need))
        pos, before = prefix_count(sel)
        slot_ref[i] = jnp.where(sel, pos, -1.0).astype(jnp.int32)
        start_ref[i] = before.astype(jnp.int32)


def _select(aff_t, cap):
    b, e, t = aff_t.shape
    tiles = t // LANES
    rows = e * tiles
    nb = _largest_divisor(b, (4, 2))
    rid = np.arange(rows)
    same = (rid[:, None] // tiles) == (rid[None, :] // tiles)
    gsame = jnp.asarray(same, BF16)
    gprev = jnp.asarray(same & (rid[None, :] < rid[:, None]), BF16)
    li = np.arange(LANES)
    triu = jnp.asarray(li[:, None] <= li[None, :], BF16)
    full = lambda shape: pl.BlockSpec(shape, lambda bi: (0,) * len(shape))
    slot, start = pl.pallas_call(
        functools.partial(_select_kernel, cap=cap),
        grid=(b // nb,),
        in_specs=[pl.BlockSpec((nb, rows, LANES), lambda bi: (bi, 0, 0)),
                  full((rows, rows)), full((rows, rows)), full((LANES, LANES))],
        out_specs=[pl.BlockSpec((nb, rows, LANES), lambda bi: (bi, 0, 0))] * 2,
        out_shape=[jax.ShapeDtypeStruct((b, rows, LANES), jnp.int32)] * 2,
        compiler_params=_params("parallel"),
    )(aff_t.reshape(b, rows, LANES), gsame, gprev, triu)
    return slot.reshape(b, e, t), start[:, :, 0].reshape(b, e, tiles)


ROUTE_TILE = 256
ROUTE_WIN = 64
WIN_ALIGN = 16


def _route_tables(start128, cap):
    per = ROUTE_TILE // LANES
    start = start128[:, :, ::per]
    nxt = jnp.concatenate([start[:, :, 1:], jnp.full_like(start[:, :, :1], cap)], axis=-1)
    w0 = jnp.minimum((start // WIN_ALIGN) * WIN_ALIGN, cap - ROUTE_WIN)
    fits = jnp.all(nxt - w0 <= ROUTE_WIN, axis=1)
    return w0.reshape(-1).astype(jnp.int32), fits.reshape(-1).astype(jnp.int32)


def _gather_kernel(w0_ref, fits_ref, slot_ref, h_ref, xs_ref, *, col_chunk):
    n_exp, cap, d = xs_ref.shape[1], xs_ref.shape[2], xs_ref.shape[3]
    nt = h_ref.shape[1] // ROUTE_TILE
    bi = pl.program_id(0)
    xs_ref[...] = jnp.zeros_like(xs_ref)
    ridx = lax.broadcasted_iota(jnp.int32, (ROUTE_WIN, 1), 0)
    cidx = lax.broadcasted_iota(jnp.int32, (cap, 1), 0)
    for j in range(nt):
        tok = slice(j * ROUTE_TILE, (j + 1) * ROUTE_TILE)
        fit = fits_ref[bi * nt + j]

        @pl.when(fit != 0)
        def _():
            w0 = [pl.multiple_of(w0_ref[(bi * n_exp + e) * nt + j], WIN_ALIGN) for e in range(n_exp)]
            onehot = jnp.concatenate(
                [jnp.where(slot_ref[0, e:e + 1, tok] - w0[e] == ridx, 1.0, 0.0).astype(BF16) for e in range(n_exp)],
                axis=0)
            for c0 in range(0, d, col_chunk):
                rows = _dot(onehot, h_ref[0, tok, c0:c0 + col_chunk]).astype(BF16)
                for e in range(n_exp):
                    win = (0, e, pl.ds(w0[e], ROUTE_WIN), slice(c0, c0 + col_chunk))
                    xs_ref[win] = xs_ref[win] + rows[e * ROUTE_WIN:(e + 1) * ROUTE_WIN]

        @pl.when(fit == 0)
        def _():
            for e in range(n_exp):
                onehot = jnp.where(slot_ref[0, e:e + 1, tok] == cidx, 1.0, 0.0).astype(BF16)
                xs_ref[0, e] = xs_ref[0, e] + _dot(onehot, h_ref[0, tok, :]).astype(BF16)


def _gather(w0, fits, slot, h2, cap):
    b, t, d = h2.shape
    e = slot.shape[1]
    return pl.pallas_call(
        functools.partial(_gather_kernel, col_chunk=512),
        grid_spec=pltpu.PrefetchScalarGridSpec(
            num_scalar_prefetch=2,
            grid=(b,),
            in_specs=[pl.BlockSpec((1, e, t), lambda bi, *_: (bi, 0, 0)),
                      pl.BlockSpec((1, t, d), lambda bi, *_: (bi, 0, 0))],
            out_specs=pl.BlockSpec((1, e, cap, d), lambda bi, *_: (bi, 0, 0, 0))),
        out_shape=jax.ShapeDtypeStruct((b, e, cap, d), BF16),
        compiler_params=_params("parallel"),
    )(w0, fits, slot, h2)


def _expert_kernel(xs_ref, wg_ref, wu_ref, wd_ref, ys_ref):
    bb, _, cap, d = xs_ref.shape
    xs = xs_ref[...].reshape(bb * cap, d)
    gate = _dot(xs, wg_ref[0])
    up = _dot(xs, wu_ref[0])
    hid = (gate * _sigmoid(gate) * up).astype(BF16)
    ys_ref[...] = _dot(hid, wd_ref[0]).astype(ys_ref.dtype).reshape(bb, 1, cap, d)


def _experts(xs, wg, wu, wd, bb):
    b, e, cap, d = xs.shape
    ff = wg.shape[-1]
    tok = pl.BlockSpec((bb, 1, cap, d), lambda ei, bi: (bi, ei, 0, 0))
    return pl.pallas_call(
        _expert_kernel,
        grid=(e, b // bb),
        in_specs=[tok,
                  pl.BlockSpec((1, d, ff), lambda ei, bi: (ei, 0, 0)),
                  pl.BlockSpec((1, d, ff), lambda ei, bi: (ei, 0, 0)),
                  pl.BlockSpec((1, ff, d), lambda ei, bi: (ei, 0, 0))],
        out_specs=tok,
        out_shape=jax.ShapeDtypeStruct((b, e, cap, d), BF16),
        compiler_params=_params("parallel", "arbitrary"),
    )(xs, wg, wu, wd)


def _scatter_kernel(w0_ref, fits_ref, slot_ref, aff_ref, expand_ref, ys_ref, x1_ref, g_ref, o_ref, *, final_norm):
    n_exp, cap = ys_ref.shape[1], ys_ref.shape[2]
    bi, j = pl.program_id(0), pl.program_id(1)
    nt = pl.num_programs(1)
    slot = slot_ref[0]
    aff = aff_ref[0]
    fit = fits_ref[bi * nt + j]

    def finish(acc):
        if final_norm:
            acc = acc * lax.rsqrt(jnp.mean(acc * acc, axis=-1, keepdims=True) + NORM_EPS) * g_ref[...]
        o_ref[0] = acc

    @pl.when(fit != 0)
    def _():
        w0 = [pl.multiple_of(w0_ref[(bi * n_exp + e) * nt + j], WIN_ALIGN) for e in range(n_exp)]
        lane = lax.broadcasted_iota(jnp.int32, (1, n_exp * ROUTE_WIN), 1)
        w0_row = jnp.zeros((1, n_exp * ROUTE_WIN), jnp.int32)
        for e in range(n_exp):
            w0_row = jnp.where(lane // ROUTE_WIN == e, w0[e], w0_row)
        target = (w0_row + lane % ROUTE_WIN).astype(F32)
        slot_l = _dot(slot.astype(F32).astype(BF16), expand_ref[...])
        aff_l = _dot(aff.astype(BF16), expand_ref[...])
        onehot = jnp.where(slot_l == target, aff_l, 0.0).astype(BF16)
        rows = jnp.concatenate([ys_ref[0, e, pl.ds(w0[e], ROUTE_WIN), :] for e in range(n_exp)], axis=0)
        finish(x1_ref[0] + _dot(onehot, rows))

    @pl.when(fit == 0)
    def _():
        cidx = lax.broadcasted_iota(jnp.int32, (1, cap), 1)
        acc = x1_ref[0]
        for e in range(n_exp):
            onehot = jnp.where(slot[:, e:e + 1] == cidx, aff[:, e:e + 1], 0.0).astype(BF16)
            acc = acc + _dot(onehot, ys_ref[0, e])
        finish(acc)


def _scatter(w0, fits, slot_cols, aff, ys, x1, g, final_norm):
    b, t, d = x1.shape
    e, cap = ys.shape[1], ys.shape[2]
    assert cap <= 256, "slot ids must stay exact in bf16"
    lane_exp = np.arange(e * ROUTE_WIN) // ROUTE_WIN
    expand = jnp.asarray(np.arange(e)[:, None] == lane_exp[None, :], BF16)
    return pl.pallas_call(
        functools.partial(_scatter_kernel, final_norm=final_norm),
        grid_spec=pltpu.PrefetchScalarGridSpec(
            num_scalar_prefetch=2,
            grid=(b, t // ROUTE_TILE),
            in_specs=[pl.BlockSpec((1, ROUTE_TILE, e), lambda bi, ti, *_: (bi, ti, 0)),
                      pl.BlockSpec((1, ROUTE_TILE, e), lambda bi, ti, *_: (bi, ti, 0)),
                      pl.BlockSpec((e, e * ROUTE_WIN), lambda bi, ti, *_: (0, 0)),
                      pl.BlockSpec((1, e, cap, d), lambda bi, ti, *_: (bi, 0, 0, 0)),
                      pl.BlockSpec((1, ROUTE_TILE, d), lambda bi, ti, *_: (bi, ti, 0)),
                      pl.BlockSpec((1, d), lambda bi, ti, *_: (0, 0))],
            out_specs=pl.BlockSpec((1, ROUTE_TILE, d), lambda bi, ti, *_: (bi, ti, 0))),
        out_shape=jax.ShapeDtypeStruct((b, t, d), F32),
        compiler_params=_params("parallel", "arbitrary"),
    )(w0, fits, slot_cols, aff, expand, ys, x1, g.reshape(1, d).astype(F32))


def kernel(x, norm_mix_g, w_in, mu_prev, mu_next, w0_f, w_up_f, w0_b, w_up_b, a0_f, a_up_f, a0_b, a_up_b, g_up, k_k, k_a, r_k, ln_x_w, ln_x_b, attn_sink, w_proj_rwkv, w_proj_attn, w_out, norm_ffn_g, w_router, exp_w_gate, exp_w_up, exp_w_down, norm_final_g):
    b, t, d = x.shape
    depth = w_in.shape[0]
    n = b * t
    tm = _largest_divisor(t, (512, 256, 128))
    cap = CAPACITY_FACTOR * t // N_EXPERTS
    bb = _largest_divisor(b, (4, 2))
    xc = x
    for l in range(depth):
        x2 = xc.reshape(n, d)
        u_r, qkv, gates = _in_projection(x2, norm_mix_g[l].astype(F32), w_in[l].astype(BF16), mu_prev[l], mu_next[l],
                                         tm, t)
        y_r = _rwkv_branch(u_r.reshape(b, t, RWKV_COLS), w0_f[l], w_up_f[l], w0_b[l],
                           w_up_b[l], a0_f[l], a_up_f[l], a0_b[l], a_up_b[l], g_up[l], k_k[l], k_a[l], r_k[l],
                           ln_x_w[l], ln_x_b[l])
        y_a = _attention(qkv.reshape(b, t, QKV_COLS), attn_sink[l])
        last = l == depth - 1
        x1, h2, aff, aff_t = _merge(y_r.reshape(n, RWKV_DIM), y_a.reshape(n, ATT_DIM), gates, x2, w_proj_rwkv[l],
                                    w_proj_attn[l], w_out[l], norm_ffn_g[l], w_router[l],
                                    _largest_divisor(t, (1024, 512, 256)), t)
        slot, start128 = _select(aff_t, cap)
        w0, fits = _route_tables(start128, cap)
        xs = _gather(w0, fits, slot, h2.reshape(b, t, d), cap)
        ys = _experts(xs, exp_w_gate[l].astype(BF16), exp_w_up[l].astype(BF16), exp_w_down[l].astype(BF16), bb)
        xc = _scatter(w0, fits, jnp.swapaxes(slot, 1, 2), aff.reshape(b, t, N_EXPERTS), ys, x1.reshape(b, t, d),
                      norm_final_g, final_norm=last)
    return xc
```

```python
import functools
import math

import numpy as np
import jax
import jax.numpy as jnp
from jax import lax
from jax.experimental import pallas as pl
from jax.experimental.pallas import tpu as pltpu

F32 = jnp.float32
BF16 = jnp.bfloat16

RWKV_HEAD = 64
RWKV_HEADS = 8
RWKV_DIM = RWKV_HEADS * RWKV_HEAD
W_LORA = 64
A_LORA = 64
G_LORA = 128
DECAY_SCALE = math.exp(-0.5)
LNX_EPS = 64e-5
ATT_HEADS = 8
ATT_KV_HEADS = 2
ATT_HEAD = 64
ATT_DIM = ATT_HEADS * ATT_HEAD
KV_DIM = ATT_KV_HEADS * ATT_HEAD
WINDOW = 128
BLOCK = 128
NEG_INF = -1e30
N_EXPERTS = 16
CAPACITY_FACTOR = 2
NORM_EPS = 1e-6
RWKV_COLS = 3 * RWKV_DIM + G_LORA + 2 * W_LORA + 2 * A_LORA
QKV_COLS = ATT_DIM + 2 * KV_DIM
OFF_R, OFF_K, OFF_V = 0, RWKV_DIM, 2 * RWKV_DIM
OFF_G = 3 * RWKV_DIM
OFF_WF = OFF_G + G_LORA
OFF_WB = OFF_WF + W_LORA
OFF_AF = OFF_WB + W_LORA
OFF_AB = OFF_AF + A_LORA

LANES = 128
VMEM_LIMIT = 48 * 1024 * 1024

CHUNK = 64


def _dot(a, b):
    return jnp.dot(a, b, preferred_element_type=F32)


def _dot_nt(a, b):
    return lax.dot_general(a, b, (((1,), (1,)), ((), ())), preferred_element_type=F32)


def _split2(x):
    hi = x.astype(BF16)
    lo = (x - hi.astype(F32)).astype(BF16)
    return hi, lo


def _seg_sum(x, bd):
    width = bd.shape[0]
    return jnp.concatenate([_dot(x[:, j:j + width].astype(BF16), bd) for j in range(0, x.shape[1], width)], axis=1)


def _sigmoid(x):
    return 0.5 * jnp.tanh(0.5 * x) + 0.5


def _largest_divisor(n, candidates):
    for c in candidates:
        if n % c == 0:
            return c
    return 1


def _params(*sem):
    return pltpu.CompilerParams(dimension_semantics=sem, vmem_limit_bytes=VMEM_LIMIT)


def _inproj_kernel(x_ref, xp_ref, xn_ref, g_ref, w_ref, mup_ref, mun_ref, ur_ref, qkv_ref, gate_ref, *,
                   col_chunk, tiles_per_seq):
    i = pl.program_id(0)
    tm = x_ref.shape[0]
    halo = xp_ref.shape[0]
    x = jnp.concatenate([xp_ref[...], x_ref[...], xn_ref[...]], axis=0)
    h = x * lax.rsqrt(jnp.mean(x * x, axis=-1, keepdims=True) + NORM_EPS) * g_ref[...]
    row = lax.broadcasted_iota(jnp.int32, (tm + 2 * halo, 1), 0)
    pos = i % tiles_per_seq
    lo = jnp.where(pos == 0, halo, 0)
    hi = jnp.where(pos == tiles_per_seq - 1, tm + halo, tm + 2 * halo)
    h = jnp.where((row < lo) | (row >= hi), 0.0, h)
    hb_all = h.astype(BF16)
    hb = h[halo:halo + tm].astype(BF16)
    for j in range(0, RWKV_COLS, col_chunk):
        w = min(col_chunk, RWKV_COLS - j)
        u = _dot(hb_all, w_ref[:, j:j + w])
        prev = pltpu.roll(u, 1, axis=0)
        nxt = pltpu.roll(u, tm + 2 * halo - 1, axis=0)
        us = u + mup_ref[:, j:j + w] * (prev - u) + mun_ref[:, j:j + w] * (nxt - u)
        ur_ref[:, j:j + w] = us[halo:halo + tm].astype(BF16)
    c0 = RWKV_COLS
    for ref in (qkv_ref, gate_ref):
        width = ref.shape[-1]
        for j in range(0, width, col_chunk):
            w = min(col_chunk, width - j)
            ref[:, j:j + w] = _dot(hb, w_ref[:, c0 + j:c0 + j + w]).astype(BF16)
        c0 += width


def _in_projection(x2, g, w_in_bf, mu_prev, mu_next, tm, seq_len):
    n, d = x2.shape
    cols = w_in_bf.shape[1]
    gate_cols = cols - RWKV_COLS - QKV_COLS
    halo = 8
    hb = tm // halo
    n_halo = n // halo
    return pl.pallas_call(
        functools.partial(_inproj_kernel, col_chunk=512, tiles_per_seq=seq_len // tm),
        grid=(n // tm,),
        in_specs=[pl.BlockSpec((tm, d), lambda i: (i, 0)),
                  pl.BlockSpec((halo, d), lambda i: (jnp.maximum(i * hb - 1, 0), 0)),
                  pl.BlockSpec((halo, d), lambda i: (jnp.minimum((i + 1) * hb, n_halo - 1), 0)),
                  pl.BlockSpec((1, d), lambda i: (0, 0)),
                  pl.BlockSpec((d, cols), lambda i: (0, 0), pipeline_mode=pl.Buffered(1)),
                  pl.BlockSpec((1, RWKV_COLS), lambda i: (0, 0)),
                  pl.BlockSpec((1, RWKV_COLS), lambda i: (0, 0))],
        out_specs=[pl.BlockSpec((tm, RWKV_COLS), lambda i: (i, 0)),
                   pl.BlockSpec((tm, QKV_COLS), lambda i: (i, 0)),
                   pl.BlockSpec((tm, gate_cols), lambda i: (i, 0))],
        out_shape=[jax.ShapeDtypeStruct((n, RWKV_COLS), BF16),
                   jax.ShapeDtypeStruct((n, QKV_COLS), BF16),
                   jax.ShapeDtypeStruct((n, gate_cols), BF16)],
        compiler_params=_params("parallel"),
    )(x2, x2, x2, g.reshape(1, d), w_in_bf, mu_prev.reshape(1, -1).astype(F32), mu_next.reshape(1, -1).astype(F32))


GROUP_HEADS = 4
GROUP = GROUP_HEADS * RWKV_HEAD
N_GROUPS = RWKV_HEADS // GROUP_HEADS
ROWS_PER_STEP = 256
SEQS_PER_STEP = 4


def _rwkv_kernel(*refs, reverse, final):
    if final:
        (u_ref, w0_ref, wup_ref, a0_ref, aup_ref, kk_ref, ka_ref,
         bd_ref, tri_ref, a0o_ref, aupo_ref, gup_ref, rk_ref, lnw_ref, lnb_ref, yb_ref,
         o_ref, st_ref) = refs
    else:
        (u_ref, w0_ref, wup_ref, a0_ref, aup_ref, kk_ref, ka_ref,
         bd_ref, tri_ref, o_ref, st_ref) = refs
    NB, TB = u_ref.shape[0], u_ref.shape[1]
    L = CHUNK
    N = RWKV_HEAD
    n_ck = TB // L
    step = pl.program_id(1)

    @pl.when(step == 0)
    def _():
        st_ref[...] = jnp.zeros_like(st_ref)

    bd = bd_ref[...]
    tri = tri_ref[...]

    def prep(bi):
        col = lambda off, width: u_ref[bi, :, off:off + width].astype(F32)
        r = col(OFF_R, RWKV_DIM)
        k = col(OFF_K, RWKV_DIM)
        v = col(OFF_V, RWKV_DIM)
        w_lo = col(OFF_WB if reverse else OFF_WF, W_LORA)
        a_lo = col(OFF_AB if reverse else OFF_AF, A_LORA)
        lw = -DECAY_SCALE * _sigmoid(w0_ref[...] + _dot(jnp.tanh(w_lo).astype(BF16), wup_ref[...]))
        a = _sigmoid(a0_ref[...] + _dot(a_lo.astype(BF16), aup_ref[...]))
        kkr = k * kk_ref[...]
        kk = kkr * lax.rsqrt(jnp.maximum(_seg_sum(kkr * kkr, bd), 1e-24))
        kd = k * (1.0 + (a - 1.0) * ka_ref[...])
        b = a * kk
        l_hi, l_lo = _split2(lw)
        cs = _dot(tri, l_hi) + _dot(tri, l_lo)
        edge = 0 if reverse else L - 1
        e_tot = [jnp.exp(cs[c * L + edge:c * L + edge + 1]) for c in range(n_ck)]
        e_neg = jnp.exp(-cs)
        e_rem = jnp.concatenate([e_neg[c * L:(c + 1) * L] * e_tot[c] for c in range(n_ck)], axis=0)
        return dict(r=r, k=k, v=v, kd=kd, col=col, e_tot=e_tot,
                    a_t=(kk * jnp.exp(cs - lw)).astype(BF16), r_t=(r * jnp.exp(cs)).astype(BF16),
                    k_t=(kd * e_neg).astype(BF16), b_t=(b * e_neg).astype(BF16),
                    k_h=(kd * e_rem).astype(BF16), b_hf=-b * e_rem)

    pre = [prep(bi) for bi in range(NB)]

    ii = lax.broadcasted_iota(jnp.int32, (GROUP, GROUP), 0)
    jj = lax.broadcasted_iota(jnp.int32, (GROUP, GROUP), 1)
    same_head = (ii // N) == (jj // N)
    ti = lax.broadcasted_iota(jnp.int32, (L, GROUP), 0)
    si = lax.broadcasted_iota(jnp.int32, (L, GROUP), 1) % N
    if reverse:
        strict = si > ti
        incl = si >= ti
    else:
        strict = si < ti
        incl = si <= ti
    eye = (si == ti).astype(F32)
    level_masks = []
    s = 1
    while s < L:
        same = (ti // (2 * s)) == (si // (2 * s))
        t_hi = (ti & s) != 0
        s_hi = (si & s) != 0
        if reverse:
            level_masks.append(same & jnp.logical_not(t_hi) & s_hi)
        else:
            level_masks.append(same & t_hi & jnp.logical_not(s_hi))
        s *= 2

    def bdiag(xp):
        return jnp.where(same_head, jnp.concatenate([xp.astype(BF16)] * GROUP_HEADS, axis=0), 0.0)

    def head_transpose(xp):
        return jnp.concatenate([xp[:, h * N:(h + 1) * N].T for h in range(GROUP_HEADS)], axis=1)

    order = list(range(n_ck))[::-1] if reverse else list(range(n_ck))
    cgs = [(bi, c, g) for bi in range(NB) for c in order for g in range(N_GROUPS)]

    def tile(name, cg):
        bi, c, g = cg
        return pre[bi][name][c * L:(c + 1) * L, g * GROUP:(g + 1) * GROUP]

    ar = {cg: jnp.concatenate([tile('a_t', cg), tile('r_t', cg)], axis=0) for cg in cgs}
    arb = {cg: _dot_nt(ar[cg], bdiag(tile('b_t', cg))) for cg in cgs}
    ark = {cg: _dot_nt(ar[cg], bdiag(tile('k_t', cg))) for cg in cgs}
    ab = {cg: arb[cg][0:L] for cg in cgs}
    rb = {cg: arb[cg][L:2 * L] for cg in cgs}
    ak = {cg: ark[cg][0:L] for cg in cgs}
    rk_ = {cg: ark[cg][L:2 * L] for cg in cgs}
    x = {cg: eye - jnp.where(level_masks[0], ab[cg], 0.0) for cg in cgs}
    for m in level_masks[1:]:
        xn = {cg: _dot(x[cg].astype(BF16), bdiag(jnp.where(m, ab[cg], 0.0))).astype(BF16) for cg in cgs}
        x = {cg: x[cg] - _dot(xn[cg], bdiag(x[cg])) for cg in cgs}
    xb = {cg: x[cg].astype(BF16) for cg in cgs}
    xa = {cg: _dot(xb[cg], bdiag(tile('a_t', cg))) for cg in cgs}
    xa_w = {cg: bdiag(xa[cg]) for cg in cgs}
    mrb_n = {cg: jnp.where(incl, -rb[cg], 0.0).astype(BF16) for cg in cgs}
    v_w = {cg: bdiag(tile('v', cg)) for cg in cgs}
    mv = {cg: _dot(jnp.concatenate([jnp.where(strict, ak[cg], 0.0).astype(BF16),
                                    jnp.where(incl, rk_[cg], 0.0).astype(BF16)], axis=0), v_w[cg]) for cg in cgs}
    xmv = {cg: _dot(xb[cg], bdiag(mv[cg][0:L])) for cg in cgs}
    bt = {cg: head_transpose(tile('b_hf', cg)).astype(BF16) for cg in cgs}
    lhs2 = {cg: jnp.concatenate([mrb_n[cg], bt[cg]], axis=0) for cg in cgs}
    pa = {cg: _dot(lhs2[cg], xa_w[cg]) for cg in cgs}
    rq = {cg: (tile('r_t', cg).astype(F32) + pa[cg][0:L]).astype(BF16) for cg in cgs}
    pm = {cg: _dot(lhs2[cg], bdiag(xmv[cg])) for cg in cgs}
    dd = {cg: mv[cg][L:2 * L] + pm[cg][0:L] for cg in cgs}
    eye_b = eye.astype(BF16)
    c2 = {cg: _dot_nt(eye_b, bdiag(pm[cg][L:2 * L])) for cg in cgs}
    cc = {cg: _dot(head_transpose(tile('v', cg)).astype(BF16), bdiag(tile('k_h', cg))) + c2[cg] for cg in cgs}
    wt_w = {cg: bdiag(pa[cg][L:2 * L]) for cg in cgs}

    state = {(bi, g): st_ref[bi, g] for bi in range(NB) for g in range(N_GROUPS)}
    y_tiles = {}
    for c in order:
        for bi in range(NB):
            for g in range(N_GROUPS):
                cg = (bi, c, g)
                sb = state[bi, g].astype(BF16)
                decay = pre[bi]['e_tot'][c][:, g * GROUP:(g + 1) * GROUP]
                state[bi, g] = state[bi, g] * decay + _dot_nt(sb, wt_w[cg]) + cc[cg]
                y_tiles[cg] = _dot_nt(rq[cg], bdiag(sb)) + dd[cg]
    for key, val in state.items():
        st_ref[key] = val

    for bi in range(NB):
        wkv = jnp.concatenate(
            [jnp.concatenate([y_tiles[(bi, c, g)] for g in range(N_GROUPS)], axis=1) for c in range(n_ck)], axis=0)
        if not final:
            o_ref[bi] = wkv
            continue
        p = pre[bi]
        r, k, v, kd, col = p['r'], p['k'], p['v'], p['kd'], p['col']
        wkv = wkv + yb_ref[bi]
        inv_n = 1.0 / N
        mean = _seg_sum(wkv, bd) * inv_n
        xc = wkv - mean
        var = _seg_sum(xc * xc, bd) * inv_n
        normed = xc * lax.rsqrt(var + LNX_EPS) * lnw_ref[...] + lnb_ref[...]
        a_o = _sigmoid(a0o_ref[...] + _dot(col(OFF_AB, A_LORA).astype(BF16), aupo_ref[...]))
        kd_o = k * (1.0 + (a_o - 1.0) * ka_ref[...])
        bonus = _seg_sum(r * rk_ref[...] * (kd + kd_o), bd) * v
        gg = _dot(_sigmoid(col(OFF_G, G_LORA)).astype(BF16), gup_ref[...])
        o_ref[bi] = ((normed + bonus) * gg).astype(o_ref.dtype)


def _rwkv_pass(u_r, consts, *, reverse, final, extra=()):
    b, t, _ = u_r.shape
    tb = _largest_divisor(t, (ROWS_PER_STEP, CHUNK))
    nb = _largest_divisor(b, (SEQS_PER_STEP,))
    n_steps = t // tb

    def sidx(s):
        return (n_steps - 1 - s) if reverse else s

    def full(shape):
        return pl.BlockSpec(shape, lambda bi, s: (0,) * len(shape))

    in_specs = [pl.BlockSpec((nb, tb, RWKV_COLS), lambda bi, s: (bi, sidx(s), 0))]
    args = [u_r]
    for arr in consts:
        in_specs.append(full(arr.shape))
        args.append(arr)
    if final:
        for arr in extra[:-1]:
            in_specs.append(full(arr.shape))
            args.append(arr)
        in_specs.append(pl.BlockSpec((nb, tb, RWKV_DIM), lambda bi, s: (bi, sidx(s), 0)))
        args.append(extra[-1])
    out_dtype = BF16 if final else F32
    return pl.pallas_call(
        functools.partial(_rwkv_kernel, reverse=reverse, final=final),
        grid=(b // nb, n_steps),
        in_specs=in_specs,
        out_specs=pl.BlockSpec((nb, tb, RWKV_DIM), lambda bi, s: (bi, sidx(s), 0)),
        out_shape=jax.ShapeDtypeStruct((b, t, RWKV_DIM), out_dtype),
        scratch_shapes=[pltpu.VMEM((nb, N_GROUPS, RWKV_HEAD, GROUP), F32)],
        compiler_params=_params("parallel", "arbitrary"),
    )(*args)


def _rwkv_branch(u_r, w0_f, w_up_f, w0_b, w_up_b, a0_f, a_up_f, a0_b, a_up_b,
                 g_up, k_k, k_a, r_k, ln_w, ln_b):
    t = u_r.shape[1]
    tb = _largest_divisor(t, (ROWS_PER_STEP, CHUNK))
    row = lambda a: a.reshape(1, -1).astype(F32)
    seg = np.arange(GROUP) // RWKV_HEAD
    bd = jnp.asarray(seg[:, None] == seg[None, :], BF16)
    ti = np.arange(tb)
    same_chunk = (ti[:, None] // CHUNK) == (ti[None, :] // CHUNK)
    tri_f = jnp.asarray(same_chunk & (ti[None, :] <= ti[:, None]), BF16)
    tri_b = jnp.asarray(same_chunk & (ti[None, :] >= ti[:, None]), BF16)
    common = lambda w0, wup, a0, aup, tri: [row(w0), wup.astype(BF16), row(a0), aup.astype(BF16), row(k_k), row(k_a),
                                            bd, tri]
    y_b = _rwkv_pass(u_r, common(w0_b, w_up_b, a0_b, a_up_b, tri_b), reverse=True, final=False)
    extra = [row(a0_b), a_up_b.astype(BF16), g_up.astype(BF16), row(r_k), row(ln_w), row(ln_b), y_b]
    return _rwkv_pass(u_r, common(w0_f, w_up_f, a0_f, a_up_f, tri_f), reverse=False, final=True, extra=extra)


Q_BLOCKS = 4


def _attn_kernel(sink_ref, *refs):
    bias_refs = refs[:Q_BLOCKS]
    q_ref, kp_ref, kc_ref, kn_ref, vp_ref, vc_ref, vn_ref, o_ref = refs[Q_BLOCKS:]
    span = BLOCK + 2 * WINDOW
    kk = jnp.concatenate([kp_ref[0], kc_ref[0], kn_ref[0]], axis=0)
    vv = jnp.concatenate([vp_ref[0], vc_ref[0], vn_ref[0]], axis=0)
    group = ATT_HEADS // ATT_KV_HEADS
    kv_lane = lax.broadcasted_iota(jnp.int32, (1, KV_DIM), 1) // ATT_HEAD
    v_ext = [jnp.where(kv_lane == kv, vv, 1.0) for kv in range(ATT_KV_HEADS)]
    qs = q_ref[0] * (ATT_HEAD ** -0.5)
    hsl = lambda h: slice(h * ATT_HEAD, (h + 1) * ATT_HEAD)
    qrows = lambda j: slice(j * BLOCK, (j + 1) * BLOCK)
    krows = lambda j: slice(j * BLOCK, j * BLOCK + span)
    jh = [(j, h) for j in range(Q_BLOCKS) for h in range(ATT_HEADS)]
    s = {c: _dot_nt(qs[qrows(c[0]), hsl(c[1])], kk[krows(c[0]), hsl(c[1] // group)]) + bias_refs[c[0]][0, c[1]]
         for c in jh}
    m = {c: jnp.maximum(jnp.max(s[c], axis=-1, keepdims=True), sink_ref[c[1]]) for c in jh}
    p = {c: jnp.exp(s[c] - m[c]).astype(BF16) for c in jh}
    pv = {c: _dot(p[c], v_ext[c[1] // group][krows(c[0])]) for c in jh}
    for j in range(Q_BLOCKS):
        outs = []
        for h in range(ATT_HEADS):
            kv = h // group
            other = (kv + 1) % ATT_KV_HEADS
            denom = pv[j, h][:, other * ATT_HEAD:other * ATT_HEAD + 1] + jnp.exp(sink_ref[h] - m[j, h])
            outs.append(pv[j, h][:, hsl(kv)] / denom)
        o_ref[0, qrows(j)] = jnp.concatenate(outs, axis=-1).astype(o_ref.dtype)


def _attention(qkv, sink):
    b, t, _ = qkv.shape
    nb = t // BLOCK
    assert nb % Q_BLOCKS == 0
    n_steps = nb // Q_BLOCKS
    rows = Q_BLOCKS * BLOCK
    span = BLOCK + 2 * WINDOW
    kcol = ATT_DIM // KV_DIM
    halo = lambda col, f: pl.BlockSpec((1, BLOCK, KV_DIM), lambda bi, i: (bi, f(i), col))
    prev = lambda i: jnp.maximum(i * Q_BLOCKS - 1, 0)
    nxt = lambda i: jnp.minimum((i + 1) * Q_BLOCKS, nb - 1)
    cur = lambda col: pl.BlockSpec((1, rows, KV_DIM), lambda bi, i: (bi, i, col))
    col = np.arange(span)[None, :]
    rel = np.abs(col - WINDOW - np.arange(BLOCK)[:, None])
    slopes = 2.0 ** (-8.0 * np.arange(1, ATT_HEADS + 1) / ATT_HEADS)
    base = np.where(rel[None] <= WINDOW, -slopes[:, None, None] * rel[None], NEG_INF)
    tables = []
    for variant in range(4):
        dead = ((col < WINDOW) & bool(variant & 1)) | ((col >= WINDOW + BLOCK) & bool(variant & 2))
        tables.append(np.where(dead[None], NEG_INF, base))
    bias = jnp.asarray(np.stack(tables).astype(np.float32))

    def bias_spec(j):
        variant = lambda i: (jnp.where(i * Q_BLOCKS + j == 0, 1, 0) + jnp.where(i * Q_BLOCKS + j == nb - 1, 2, 0))
        return pl.BlockSpec((1, ATT_HEADS, BLOCK, span), lambda bi, i: (variant(i), 0, 0, 0))

    return pl.pallas_call(
        _attn_kernel,
        grid=(b, n_steps),
        in_specs=[pl.BlockSpec(memory_space=pltpu.SMEM)] + [bias_spec(j) for j in range(Q_BLOCKS)]
        + [pl.BlockSpec((1, rows, ATT_DIM), lambda bi, i: (bi, i, 0)),
           halo(kcol, prev), cur(kcol), halo(kcol, nxt),
           halo(kcol + 1, prev), cur(kcol + 1), halo(kcol + 1, nxt)],
        out_specs=pl.BlockSpec((1, rows, ATT_DIM), lambda bi, i: (bi, i, 0)),
        out_shape=jax.ShapeDtypeStruct((b, t, ATT_DIM), BF16),
        compiler_params=_params("parallel", "parallel"),
    )(sink.astype(F32), *([bias] * Q_BLOCKS), qkv, qkv, qkv, qkv, qkv, qkv, qkv)


MERGE_SPLIT = 2


def _merge_kernel(yr_ref, ya_ref, gate_ref, x_ref, wpr_ref, wpa_ref, wo_ref, g_ref, wrt_ref,
                  x1_ref, h2_ref, aff_ref, afft_ref):
    tm, d = x_ref.shape
    part = tm // MERGE_SPLIT
    for j in range(MERGE_SPLIT):
        rs = slice(j * part, (j + 1) * part)
        gl_r = gate_ref[rs, 0:d].astype(F32)
        gl_a = gate_ref[rs, d:2 * d].astype(F32)
        merged = (_sigmoid(gl_r) * _dot(yr_ref[rs, :], wpr_ref[...])
                  + _sigmoid(gl_a) * _dot(ya_ref[rs, :], wpa_ref[...]))
        x1 = x_ref[rs, :] + _dot(merged.astype(BF16), wo_ref[...])
        x1_ref[rs, :] = x1
        h2 = (x1 * lax.rsqrt(jnp.mean(x1 * x1, axis=-1, keepdims=True) + NORM_EPS) * g_ref[...]).astype(BF16)
        h2_ref[rs, :] = h2
        logits_t = _dot_nt(wrt_ref[...], h2)
        mx = jnp.max(logits_t, axis=0, keepdims=True)
        ex = jnp.exp(logits_t - mx)
        aff_t = ex / jnp.sum(ex, axis=0, keepdims=True)
        afft_ref[0, :, rs] = aff_t
        aff_ref[rs, :] = aff_t.T


def _merge(y_r, y_a, gates, x2, wpr, wpa, wo, g, w_router, tm, seq_len):
    n, d = x2.shape
    e = w_router.shape[1]
    tiles_per_seq = seq_len // tm
    full = lambda shape: pl.BlockSpec(shape, lambda i: (0,) * len(shape))
    rows = lambda w: pl.BlockSpec((tm, w), lambda i: (i, 0))
    return pl.pallas_call(
        _merge_kernel,
        grid=(n // tm,),
        in_specs=[rows(RWKV_DIM), rows(ATT_DIM), rows(2 * d), rows(d),
                  full(wpr.shape), full(wpa.shape), full(wo.shape), full((1, d)), full((e, d))],
        out_specs=[rows(d), rows(d), rows(e),
                   pl.BlockSpec((1, e, tm), lambda i: (i // tiles_per_seq, 0, i % tiles_per_seq))],
        out_shape=[jax.ShapeDtypeStruct((n, d), F32), jax.ShapeDtypeStruct((n, d), BF16),
                   jax.ShapeDtypeStruct((n, e), F32),
                   jax.ShapeDtypeStruct((n // seq_len, e, seq_len), F32)],
        compiler_params=_params("parallel"),
    )(y_r, y_a, gates, x2, wpr.astype(BF16), wpa.astype(BF16), wo.astype(BF16), g.reshape(1, d).astype(F32),
      w_router.T.astype(BF16))


def _select_kernel(a_ref, gsame_ref, gprev_ref, triu_ref, slot_ref, start_ref, *, cap):
    nb, rows, _ = a_ref.shape
    affs = [a_ref[i] for i in range(nb)]
    gsame = gsame_ref[...]
    gprev = gprev_ref[...]
    triu = triu_ref[...]
    capf = float(cap)

    def group_count(mask):
        per_lane = _dot(gsame, jnp.where(mask, 1.0, 0.0).astype(BF16))
        return jnp.sum(per_lane, axis=-1, keepdims=True)

    def search(i, prefixes):
        out = []
        for aff, prefix in zip(affs, prefixes):
            cand = prefix | lax.shift_left(jnp.int32(1), 30 - i)
            out.append(jnp.where(group_count(aff >= pltpu.bitcast(cand, F32)) >= capf, cand, prefix))
        return tuple(out)

    thr_bits = lax.fori_loop(0, 31, search, tuple(jnp.zeros((rows, 1), jnp.int32) for _ in range(nb)))

    def prefix_count(mask):
        mb = jnp.where(mask, 1.0, 0.0)
        inc = _dot(mb.astype(BF16), triu)
        tot = jnp.broadcast_to(inc[:, LANES - 1:LANES], inc.shape).astype(BF16)
        before = _dot(gprev, tot)
        return inc - mb + before, before

    for i, aff in enumerate(affs):
        thr = pltpu.bitcast(thr_bits[i], F32)
        gt = aff > thr
        eq = aff == thr
        need = capf - group_count(gt)
        sel = gt | (eq & (prefix_count(eq)[0] < need))
        pos, before = prefix_count(sel)
        slot_ref[i] = jnp.where(sel, pos, -1.0).astype(jnp.int32)
        start_ref[i] = before.astype(jnp.int32)


def _select(aff_t, cap):
    b, e, t = aff_t.shape
    tiles = t // LANES
    rows = e * tiles
    nb = _largest_divisor(b, (4, 2))
    rid = np.arange(rows)
    same = (rid[:, None] // tiles) == (rid[None, :] // tiles)
    gsame = jnp.asarray(same, BF16)
    gprev = jnp.asarray(same & (rid[None, :] < rid[:, None]), BF16)
    li = np.arange(LANES)
    triu = jnp.asarray(li[:, None] <= li[None, :], BF16)
    full = lambda shape: pl.BlockSpec(shape, lambda bi: (0,) * len(shape))
    slot, start = pl.pallas_call(
        functools.partial(_select_kernel, cap=cap),
        grid=(b // nb,),
        in_specs=[pl.BlockSpec((nb, rows, LANES), lambda bi: (bi, 0, 0)),
                  full((rows, rows)), full((rows, rows)), full((LANES, LANES))],
        out_specs=[pl.BlockSpec((nb, rows, LANES), lambda bi: (bi, 0, 0))] * 2,
        out_shape=[jax.ShapeDtypeStruct((b, rows, LANES), jnp.int32)] * 2,
        compiler_params=_params("parallel"),
    )(aff_t.reshape(b, rows, LANES), gsame, gprev, triu)
    return slot.reshape(b, e, t), start[:, :, 0].reshape(b, e, tiles)


ROUTE_TILE = 256
ROUTE_WIN = 64
WIN_ALIGN = 16


def _route_tables(start128, cap):
    per = ROUTE_TILE // LANES
    start = start128[:, :, ::per]
    nxt = jnp.concatenate([start[:, :, 1:], jnp.full_like(start[:, :, :1], cap)], axis=-1)
    w0 = jnp.minimum((start // WIN_ALIGN) * WIN_ALIGN, cap - ROUTE_WIN)
    fits = jnp.all(nxt - w0 <= ROUTE_WIN, axis=1)
    return w0.reshape(-1).astype(jnp.int32), fits.reshape(-1).astype(jnp.int32)


def _gather_kernel(w0_ref, fits_ref, slot_ref, h_ref, xs_ref, *, col_chunk):
    n_exp, cap, d = xs_ref.shape[1], xs_ref.shape[2], xs_ref.shape[3]
    nt = h_ref.shape[1] // ROUTE_TILE
    bi = pl.program_id(0)
    xs_ref[...] = jnp.zeros_like(xs_ref)
    ridx = lax.broadcasted_iota(jnp.int32, (ROUTE_WIN, 1), 0)
    cidx = lax.broadcasted_iota(jnp.int32, (cap, 1), 0)
    for j in range(nt):
        tok = slice(j * ROUTE_TILE, (j + 1) * ROUTE_TILE)
        fit = fits_ref[bi * nt + j]

        @pl.when(fit != 0)
        def _():
            w0 = [pl.multiple_of(w0_ref[(bi * n_exp + e) * nt + j], WIN_ALIGN) for e in range(n_exp)]
            onehot = jnp.concatenate(
                [jnp.where(slot_ref[0, e:e + 1, tok] - w0[e] == ridx, 1.0, 0.0).astype(BF16) for e in range(n_exp)],
                axis=0)
            for c0 in range(0, d, col_chunk):
                rows = _dot(onehot, h_ref[0, tok, c0:c0 + col_chunk]).astype(BF16)
                for e in range(n_exp):
                    win = (0, e, pl.ds(w0[e], ROUTE_WIN), slice(c0, c0 + col_chunk))
                    xs_ref[win] = xs_ref[win] + rows[e * ROUTE_WIN:(e + 1) * ROUTE_WIN]

        @pl.when(fit == 0)
        def _():
            for e in range(n_exp):
                onehot = jnp.where(slot_ref[0, e:e + 1, tok] == cidx, 1.0, 0.0).astype(BF16)
                xs_ref[0, e] = xs_ref[0, e] + _dot(onehot, h_ref[0, tok, :]).astype(BF16)


def _gather(w0, fits, slot, h2, cap):
    b, t, d = h2.shape
    e = slot.shape[1]
    return pl.pallas_call(
        functools.partial(_gather_kernel, col_chunk=512),
        grid_spec=pltpu.PrefetchScalarGridSpec(
            num_scalar_prefetch=2,
            grid=(b,),
            in_specs=[pl.BlockSpec((1, e, t), lambda bi, *_: (bi, 0, 0)),
                      pl.BlockSpec((1, t, d), lambda bi, *_: (bi, 0, 0))],
            out_specs=pl.BlockSpec((1, e, cap, d), lambda bi, *_: (bi, 0, 0, 0))),
        out_shape=jax.ShapeDtypeStruct((b, e, cap, d), BF16),
        compiler_params=_params("parallel"),
    )(w0, fits, slot, h2)


def _expert_kernel(xs_ref, wg_ref, wu_ref, wd_ref, ys_ref):
    bb, _, cap, d = xs_ref.shape
    xs = xs_ref[...].reshape(bb * cap, d)
    gate = _dot(xs, wg_ref[0])
    up = _dot(xs, wu_ref[0])
    hid = (gate * _sigmoid(gate) * up).astype(BF16)
    ys_ref[...] = _dot(hid, wd_ref[0]).astype(ys_ref.dtype).reshape(bb, 1, cap, d)


def _experts(xs, wg, wu, wd, bb):
    b, e, cap, d = xs.shape
    ff = wg.shape[-1]
    tok = pl.BlockSpec((bb, 1, cap, d), lambda ei, bi: (bi, ei, 0, 0))
    return pl.pallas_call(
        _expert_kernel,
        grid=(e, b // bb),
        in_specs=[tok,
                  pl.BlockSpec((1, d, ff), lambda ei, bi: (ei, 0, 0)),
                  pl.BlockSpec((1, d, ff), lambda ei, bi: (ei, 0, 0)),
                  pl.BlockSpec((1, ff, d), lambda ei, bi: (ei, 0, 0))],
        out_specs=tok,
        out_shape=jax.ShapeDtypeStruct((b, e, cap, d), BF16),
        compiler_params=_params("parallel", "arbitrary"),
    )(xs, wg, wu, wd)


def _scatter_kernel(w0_ref, fits_ref, slot_ref, aff_ref, expand_ref, ys_ref, x1_ref, g_ref, o_ref, *, final_norm):
    n_exp, cap = ys_ref.shape[1], ys_ref.shape[2]
    bi, j = pl.program_id(0), pl.program_id(1)
    nt = pl.num_programs(1)
    slot = slot_ref[0]
    aff = aff_ref[0]
    fit = fits_ref[bi * nt + j]

    def finish(acc):
        if final_norm:
            acc = acc * lax.rsqrt(jnp.mean(acc * acc, axis=-1, keepdims=True) + NORM_EPS) * g_ref[...]
        o_ref[0] = acc

    @pl.when(fit != 0)
    def _():
        w0 = [pl.multiple_of(w0_ref[(bi * n_exp + e) * nt + j], WIN_ALIGN) for e in range(n_exp)]
        lane = lax.broadcasted_iota(jnp.int32, (1, n_exp * ROUTE_WIN), 1)
        w0_row = jnp.zeros((1, n_exp * ROUTE_WIN), jnp.int32)
        for e in range(n_exp):
            w0_row = jnp.where(lane // ROUTE_WIN == e, w0[e], w0_row)
        target = (w0_row + lane % ROUTE_WIN).astype(F32)
        slot_l = _dot(slot.astype(F32).astype(BF16), expand_ref[...])
        aff_l = _dot(aff.astype(BF16), expand_ref[...])
        onehot = jnp.where(slot_l == target, aff_l, 0.0).astype(BF16)
        rows = jnp.concatenate([ys_ref[0, e, pl.ds(w0[e], ROUTE_WIN), :] for e in range(n_exp)], axis=0)
        finish(x1_ref[0] + _dot(onehot, rows))

    @pl.when(fit == 0)
    def _():
        cidx = lax.broadcasted_iota(jnp.int32, (1, cap), 1)
        acc = x1_ref[0]
        for e in range(n_exp):
            onehot = jnp.where(slot[:, e:e + 1] == cidx, aff[:, e:e + 1], 0.0).astype(BF16)
            acc = acc + _dot(onehot, ys_ref[0, e])
        finish(acc)


def _scatter(w0, fits, slot_cols, aff, ys, x1, g, final_norm):
    b, t, d = x1.shape
    e, cap = ys.shape[1], ys.shape[2]
    assert cap <= 256, "slot ids must stay exact in bf16"
    lane_exp = np.arange(e * ROUTE_WIN) // ROUTE_WIN
    expand = jnp.asarray(np.arange(e)[:, None] == lane_exp[None, :], BF16)
    return pl.pallas_call(
        functools.partial(_scatter_kernel, final_norm=final_norm),
        grid_spec=pltpu.PrefetchScalarGridSpec(
            num_scalar_prefetch=2,
            grid=(b, t // ROUTE_TILE),
            in_specs=[pl.BlockSpec((1, ROUTE_TILE, e), lambda bi, ti, *_: (bi, ti, 0)),
                      pl.BlockSpec((1, ROUTE_TILE, e), lambda bi, ti, *_: (bi, ti, 0)),
                      pl.BlockSpec((e, e * ROUTE_WIN), lambda bi, ti, *_: (0, 0)),
                      pl.BlockSpec((1, e, cap, d), lambda bi, ti, *_: (bi, 0, 0, 0)),
                      pl.BlockSpec((1, ROUTE_TILE, d), lambda bi, ti, *_: (bi, ti, 0)),
                      pl.BlockSpec((1, d), lambda bi, ti, *_: (0, 0))],
            out_specs=pl.BlockSpec((1, ROUTE_TILE, d), lambda bi, ti, *_: (bi, ti, 0))),
        out_shape=jax.ShapeDtypeStruct((b, t, d), F32),
        compiler_params=_params("parallel", "arbitrary"),
    )(w0, fits, slot_cols, aff, expand, ys, x1, g.reshape(1, d).astype(F32))


def kernel(x, norm_mix_g, w_in, mu_prev, mu_next, w0_f, w_up_f, w0_b, w_up_b, a0_f, a_up_f, a0_b, a_up_b, g_up, k_k, k_a, r_k, ln_x_w, ln_x_b, attn_sink, w_proj_rwkv, w_proj_attn, w_out, norm_ffn_g, w_router, exp_w_gate, exp_w_up, exp_w_down, norm_final_g):
    b, t, d = x.shape
    depth = w_in.shape[0]
    n = b * t
    tm = _largest_divisor(t, (1024, 512, 256, 128))
    cap = CAPACITY_FACTOR * t // N_EXPERTS
    bb = _largest_divisor(b, (4, 2))
    xc = x
    for l in range(depth):
        x2 = xc.reshape(n, d)
        u_r, qkv, gates = _in_projection(x2, norm_mix_g[l].astype(F32), w_in[l].astype(BF16), mu_prev[l], mu_next[l],
                                         tm, t)
        y_r = _rwkv_branch(u_r.reshape(b, t, RWKV_COLS), w0_f[l], w_up_f[l], w0_b[l],
                           w_up_b[l], a0_f[l], a_up_f[l], a0_b[l], a_up_b[l], g_up[l], k_k[l], k_a[l], r_k[l],
                           ln_x_w[l], ln_x_b[l])
        y_a = _attention(qkv.reshape(b, t, QKV_COLS), attn_sink[l])
        last = l == depth - 1
        x1, h2, aff, aff_t = _merge(y_r.reshape(n, RWKV_DIM), y_a.reshape(n, ATT_DIM), gates, x2, w_proj_rwkv[l],
                                    w_proj_attn[l], w_out[l], norm_ffn_g[l], w_router[l],
                                    _largest_divisor(t, (1024, 512, 256)), t)
        slot, start128 = _select(aff_t, cap)
        w0, fits = _route_tables(start128, cap)
        xs = _gather(w0, fits, slot, h2.reshape(b, t, d), cap)
        ys = _experts(xs, exp_w_gate[l].astype(BF16), exp_w_up[l].astype(BF16), exp_w_down[l].astype(BF16), bb)
        xc = _scatter(w0, fits, jnp.swapaxes(slot, 1, 2), aff.reshape(b, t, N_EXPERTS), ys, x1.reshape(b, t, d),
                      norm_final_g, final_norm=last)
    return xc
```
